```python
import math
import jax
import jax.numpy as jnp
from jax import lax
import numpy as np

D_MODEL = 2048
BATCH = 2
SEQ = 4096
DEPTH = 2
DEC_BATCH = 32
DEC_SEQ = 8
PAST_LEN = 8192
PAGE_SIZE = 128

HEAD_DIM = 128
NSA_HEADS = 8
NSA_KV_HEADS = 2
NSA_GROUP = NSA_HEADS // NSA_KV_HEADS
CMP_BLOCK = 32
CMP_STRIDE = 16
SLC_BLOCK = 64
SLC_TOPN = 16
WINDOW = 512
DIFF_HEADS = 4
DIFF_QK_DIM = HEAD_DIM // 2
MLSTM_HEADS = 4
MLSTM_DIM = 128
MLSTM_CHUNK = 64
CONV_WIDTH = 4
D_FF = 5632
SEL_QBLOCK = 64
ATTN_QBLOCK = 128
NORM_EPS = 1e-6
NEG_INF = -1e30
FORCE_BONUS = 1e3

NSA_W = NSA_HEADS * HEAD_DIM
NSA_KV_W = NSA_KV_HEADS * HEAD_DIM
DIFF_W = DIFF_HEADS * HEAD_DIM
MLSTM_W = MLSTM_HEADS * MLSTM_DIM
MIX_W = NSA_W + DIFF_W + MLSTM_W
IN_SPLITS = (NSA_W, 6 * NSA_KV_W, 3 * NSA_HEADS, DIFF_W, DIFF_W, DIFF_W,
             MLSTM_W, MLSTM_W, MLSTM_W, MLSTM_HEADS, MLSTM_HEADS, MLSTM_W)
IN_COLS = sum(IN_SPLITS)

kernel_name = 'hymba_nsa_diff_mlstm_macaron_step'


def rms_norm(x, g):
    xf = x.astype(jnp.float32)
    y = xf * lax.rsqrt(jnp.mean(xf * xf, axis=-1, keepdims=True) + NORM_EPS)
    return (y * g.astype(jnp.float32)).astype(x.dtype)


def swiglu(x, w13, w2):
    a, b = jnp.split(x @ w13, 2, axis=-1)
    return (jax.nn.silu(a) * b) @ w2


def split_cols(u, sizes):
    cuts = [int(c) for c in np.cumsum(sizes)[:-1]]
    return jnp.split(u, cuts, axis=-1)


def alibi_slopes(n):
    return jnp.asarray(2.0 ** (-8.0 * np.arange(1, n + 1) / n), dtype=jnp.float32)


def q_block(n, pref):
    return pref if n % pref == 0 else n


def masked_softmax(s, mask):
    s = jnp.where(mask, s, NEG_INF)
    m = jnp.max(s, axis=-1, keepdims=True)
    p = jnp.where(mask, jnp.exp(s - m), 0.0)
    return p / jnp.maximum(jnp.sum(p, axis=-1, keepdims=True), 1e-30)


def gather_pages(pool, page_table):
    g = pool[page_table]
    return g.reshape((g.shape[0], g.shape[1] * g.shape[2]) + g.shape[3:])


def nsa_compress(x, pos, w1, w2):
    b, length = x.shape[0], x.shape[1]
    n_chunk = -(-length // CMP_STRIDE)
    ratio = CMP_BLOCK // CMP_STRIDE
    n_cmp = n_chunk - ratio + 1
    xp = jnp.pad(x, ((0, 0), (0, n_chunk * CMP_STRIDE - length), (0, 0), (0, 0)))
    ch = xp.reshape(b, n_chunk, CMP_STRIDE, NSA_KV_HEADS, HEAD_DIM)
    blk = jnp.concatenate([ch[:, i:i + n_cmp] for i in range(ratio)], axis=2)
    blk = blk + pos[None, None, :, None, :].astype(blk.dtype)
    z = jnp.swapaxes(blk, 2, 3).reshape(b, n_cmp, NSA_KV_HEADS, CMP_BLOCK * HEAD_DIM)
    return jax.nn.silu(z @ w1) @ w2


def block_cover(n_cmp, n_slc):
    i = np.arange(n_cmp)[:, None] * CMP_STRIDE
    j = np.arange(n_slc)[None, :] * SLC_BLOCK
    cover = (i < j + SLC_BLOCK) & (i + CMP_BLOCK > j)
    return jnp.asarray(cover.astype(np.float32))


def nsa_mixer(q, kv_all, kv_win, gates, lp):
    b, t = q.shape[0], q.shape[1]
    length = kv_all.shape[1]
    past = length - t
    dt = q.dtype
    f32 = jnp.float32
    q_pos = past + jnp.arange(t)
    slopes = alibi_slopes(NSA_HEADS).reshape(NSA_KV_HEADS, NSA_GROUP)
    qn = (rms_norm(q, lp['nsa_q_norm']) * HEAD_DIM ** -0.5).reshape(b, t, NSA_KV_HEADS, NSA_GROUP, HEAD_DIM)

    k_cmp = rms_norm(nsa_compress(kv_all[:, :, 0], lp['nsa_cmp_pos'][0], lp['nsa_cmp_w1'][0], lp['nsa_cmp_w2'][0]),
                     lp['nsa_k_norm'][0])
    v_cmp = nsa_compress(kv_all[:, :, 1], lp['nsa_cmp_pos'][1], lp['nsa_cmp_w1'][1], lp['nsa_cmp_w2'][1])
    n_cmp = k_cmp.shape[1]
    ends = jnp.arange(n_cmp) * CMP_STRIDE + (CMP_BLOCK - 1)
    d_cmp = q_pos[:, None] - ends[None, :]
    s = jnp.einsum('btkgd,bckd->btkgc', qn, k_cmp).astype(f32)
    s = s - slopes[None, None, :, :, None] * d_cmp.astype(f32)[None, :, None, None, :]
    p_cmp = masked_softmax(s, (d_cmp >= 0)[None, :, None, None, :])
    o_cmp = jnp.einsum('btkgc,bckd->btkgd', p_cmp.astype(dt), v_cmp)

    n_slc = -(-length // SLC_BLOCK)
    imp = jnp.einsum('btkc,cj->btkj', p_cmp.sum(axis=3), block_cover(n_cmp, n_slc))
    blk_id = jnp.arange(n_slc)[None, :]
    cur = (q_pos // SLC_BLOCK)[:, None]
    valid = blk_id * SLC_BLOCK <= q_pos[:, None]
    forced = (blk_id == 0) | (blk_id == cur) | (blk_id == cur - 1)
    score = jnp.where(valid[None, :, None, :], imp + FORCE_BONUS * forced[None, :, None, :].astype(f32), NEG_INF)
    n_sel = min(SLC_TOPN, n_slc)
    _, sel = lax.top_k(score, n_sel)
    kv_slc = jnp.pad(kv_all[:, :, 2:4], ((0, 0), (0, n_slc * SLC_BLOCK - length), (0, 0), (0, 0), (0, 0)))
    kv_slc = kv_slc.reshape(b, n_slc, SLC_BLOCK, 2, NSA_KV_HEADS, HEAD_DIM).transpose(0, 4, 1, 2, 3, 5)
    bi = jnp.arange(b)[:, None, None, None]
    ki = jnp.arange(NSA_KV_HEADS)[None, None, :, None]
    n_keys = n_sel * SLC_BLOCK

    def sel_block(args):
        qb_, sb_, pb_ = args
        nq = qb_.shape[1]
        g = kv_slc[bi, ki, sb_].reshape(b, nq, NSA_KV_HEADS, n_keys, 2, HEAD_DIM)
        kpos = (sb_[..., None] * SLC_BLOCK + jnp.arange(SLC_BLOCK)).reshape(b, nq, NSA_KV_HEADS, n_keys)
        dist = pb_[None, :, None, None] - kpos
        s_ = jnp.einsum('bqkgd,bqksd->bqkgs', qb_, g[..., 0, :]).astype(f32)
        s_ = s_ - slopes[None, None, :, :, None] * dist.astype(f32)[:, :, :, None, :]
        p_ = masked_softmax(s_, (dist >= 0)[:, :, :, None, :])
        return jnp.einsum('bqkgs,bqksd->bqkgd', p_.astype(dt), g[..., 1, :])

    qb = q_block(t, SEL_QBLOCK)
    nb = t // qb
    o_slc = lax.map(sel_block, (
        jnp.moveaxis(qn.reshape(b, nb, qb, NSA_KV_HEADS, NSA_GROUP, HEAD_DIM), 1, 0),
        jnp.moveaxis(sel.reshape(b, nb, qb, NSA_KV_HEADS, n_sel), 1, 0),
        q_pos.reshape(nb, qb)))
    o_slc = jnp.moveaxis(o_slc, 0, 1).reshape(b, t, NSA_KV_HEADS, NSA_GROUP, HEAD_DIM)

    buf_len = kv_win.shape[1] - t
    kv_w = jnp.pad(kv_win, ((0, 0), (WINDOW - buf_len, 0), (0, 0), (0, 0), (0, 0)))
    qbw = q_block(t, ATTN_QBLOCK)
    nbw = t // qbw
    span = WINDOW + qbw
    widx = jnp.arange(nbw)[:, None] * qbw + jnp.arange(span)[None, :]
    kv_wb = kv_w[:, widx]
    kpos_w = past - WINDOW + widx
    d_w = q_pos.reshape(nbw, qbw)[:, :, None] - kpos_w[:, None, :]
    m_w = (d_w >= 0) & (d_w <= WINDOW) & (kpos_w[:, None, :] >= 0)
    qw = qn.reshape(b, nbw, qbw, NSA_KV_HEADS, NSA_GROUP, HEAD_DIM)
    s = jnp.einsum('bnqkgd,bnskd->bnkgqs', qw, kv_wb[:, :, :, 0]).astype(f32)
    s = s - slopes[None, None, :, :, None, None] * d_w.astype(f32)[None, :, None, None]
    p_w = masked_softmax(s, m_w[None, :, None, None])
    o_win = jnp.einsum('bnkgqs,bnskd->bnqkgd', p_w.astype(dt), kv_wb[:, :, :, 1])
    o_win = o_win.reshape(b, t, NSA_KV_HEADS, NSA_GROUP, HEAD_DIM)

    g = jax.nn.sigmoid(gates.astype(f32)).reshape(b, t, NSA_KV_HEADS, NSA_GROUP, 3)
    o = g[..., 0:1] * o_cmp + g[..., 1:2] * o_slc + g[..., 2:3] * o_win
    return o.astype(dt).reshape(b, t, NSA_W)


def diff_mixer(q, kv_all, lp, lam_init):
    b, t = q.shape[0], q.shape[1]
    length = kv_all.shape[1]
    dt = q.dtype
    f32 = jnp.float32
    q_pos = (length - t) + jnp.arange(t)
    k_pos = jnp.arange(length)
    k = kv_all[:, :, 0].reshape(b, length, DIFF_HEADS, 2, DIFF_QK_DIM)
    v = kv_all[:, :, 1]
    lam_p = lp['diff_lambda'].astype(f32)
    lam = jnp.exp(jnp.sum(lam_p[0] * lam_p[1])) - jnp.exp(jnp.sum(lam_p[2] * lam_p[3])) + lam_init
    slopes = alibi_slopes(DIFF_HEADS)

    def attn_block(args):
        qb_, pb_ = args
        dist = pb_[:, None] - k_pos[None, :]
        s = jnp.einsum('bqhid,bkhid->bhiqk', qb_, k).astype(f32)
        s = s - slopes[None, :, None, None, None] * dist.astype(f32)
        p = masked_softmax(s, dist >= 0)
        pd = p[:, :, 0] - lam * p[:, :, 1]
        return jnp.einsum('bhqk,bkhd->bqhd', pd.astype(dt), v)

    qb = q_block(t, ATTN_QBLOCK)
    nb = t // qb
    o = lax.map(attn_block, (jnp.moveaxis(q.reshape(b, nb, qb, DIFF_HEADS, 2, DIFF_QK_DIM), 1, 0),
                             q_pos.reshape(nb, qb)))
    o = jnp.moveaxis(o, 0, 1).reshape(b, t, DIFF_HEADS, HEAD_DIM)
    o = rms_norm(o, lp['diff_out_norm']) * (1.0 - lam_init)
    return o.reshape(b, t, DIFF_W)


def causal_conv(u, buf, w, bias):
    t = u.shape[1]
    ext = jnp.concatenate([buf.astype(u.dtype), u], axis=1)
    y = bias
    for i in range(CONV_WIDTH):
        y = y + ext[:, i:i + t] * w[i]
    return jax.nn.silu(y), ext[:, ext.shape[1] - (CONV_WIDTH - 1):]


def mlstm_chunkwise(q, k, v, log_i, log_f, c0, n0, m0):
    b, t = q.shape[0], q.shape[1]
    lc = q_block(t, MLSTM_CHUNK)
    nc = t // lc

    def to_chunks(a):
        a = a.reshape((b, nc, lc) + a.shape[2:])
        return jnp.moveaxis(jnp.moveaxis(a, 1, 0), 2, 3)

    causal = jnp.tril(jnp.ones((lc, lc), dtype=bool))

    def step(carry, xs):
        c, n, m = carry
        qc, kc, vc, li, lf = xs
        bcum = jnp.cumsum(lf, axis=-1)
        dmat = jnp.where(causal, bcum[..., :, None] - bcum[..., None, :] + li[..., None, :], NEG_INF)
        inter = bcum + m[..., None]
        mj = jnp.maximum(inter, jnp.max(dmat, axis=-1))
        wts = jnp.exp(dmat - mj[..., None])
        a = jnp.exp(inter - mj)
        sqk = jnp.einsum('bhjd,bhsd->bhjs', qc, kc) * wts
        num = a[..., None] * jnp.einsum('bhjd,bhde->bhje', qc, c) + jnp.einsum('bhjs,bhse->bhje', sqk, vc)
        den = a * jnp.einsum('bhjd,bhd->bhj', qc, n) + jnp.sum(sqk, axis=-1)
        h = num / jnp.maximum(jnp.abs(den), jnp.exp(-mj))[..., None]
        b_last = bcum[..., -1]
        gl = b_last[..., None] - bcum + li
        m_new = jnp.maximum(b_last + m, jnp.max(gl, axis=-1))
        decay = jnp.exp(b_last + m - m_new)
        wg = jnp.exp(gl - m_new[..., None])
        c_new = decay[..., None, None] * c + jnp.einsum('bhs,bhsd,bhse->bhde', wg, kc, vc)
        n_new = decay[..., None] * n + jnp.einsum('bhs,bhsd->bhd', wg, kc)
        return (c_new, n_new, m_new), h

    (c, n, m), hs = lax.scan(step, (c0, n0, m0), tuple(to_chunks(a) for a in (q, k, v, log_i, log_f)))
    h = jnp.swapaxes(jnp.moveaxis(hs, 0, 1), 2, 3).reshape(b, t, MLSTM_HEADS, MLSTM_DIM)
    return h, c, n, m


def mlstm_mixer(mq, mk, mv, mi, mf, mo, c0, n0, m0, conv_buf, lp):
    b, t = mq.shape[0], mq.shape[1]
    f32 = jnp.float32
    qk, conv_new = causal_conv(jnp.concatenate([mq, mk], axis=-1), conv_buf, lp['ml_conv_w'], lp['ml_conv_b'])
    q, k = jnp.split(qk.astype(f32), 2, axis=-1)
    q = q.reshape(b, t, MLSTM_HEADS, MLSTM_DIM) * MLSTM_DIM ** -0.5
    k = k.reshape(b, t, MLSTM_HEADS, MLSTM_DIM)
    v = mv.astype(f32).reshape(b, t, MLSTM_HEADS, MLSTM_DIM)
    gb = lp['ml_gate_b'].astype(f32)
    log_i = mi.astype(f32) + gb[0]
    log_f = jax.nn.log_sigmoid(mf.astype(f32) + gb[1])
    h, c, n, m = mlstm_chunkwise(q, k, v, log_i, log_f, c0.astype(f32), n0.astype(f32), m0.astype(f32))
    h = rms_norm(h, lp['ml_out_norm']) * jax.nn.sigmoid(mo.astype(f32)).reshape(b, t, MLSTM_HEADS, MLSTM_DIM)
    return h.reshape(b, t, MLSTM_W).astype(mq.dtype), c, n, m, conv_new


def decoder_layer(x, nsa_past, diff_past, win_buf, ml_c, ml_n, ml_m, ml_conv, lp, lam_init):
    b, t = x.shape[0], x.shape[1]
    dt = x.dtype
    h = x + 0.5 * swiglu(rms_norm(x, lp['ffn1_norm']), lp['ffn1_w13'], lp['ffn1_w2'])
    u = rms_norm(h, lp['mix_norm']) @ lp['w_in']
    nq, nkv, ng, dq, dk, dv, mq, mk, mv, mi, mf, mo = split_cols(u, IN_SPLITS)
    nkv = nkv.reshape(b, t, 6, NSA_KV_HEADS, HEAD_DIM)
    k_slc = rms_norm(nkv[:, :, 2], lp['nsa_k_norm'][1])
    k_win = rms_norm(nkv[:, :, 4], lp['nsa_k_norm'][2])
    nsa_rows = jnp.stack([nkv[:, :, 0], nkv[:, :, 1], k_slc, nkv[:, :, 3]], axis=2)
    win_all = jnp.concatenate([win_buf.astype(dt), jnp.stack([k_win, nkv[:, :, 5]], axis=2)], axis=1)
    o_nsa = nsa_mixer(nq.reshape(b, t, NSA_HEADS, HEAD_DIM),
                      jnp.concatenate([nsa_past.astype(dt), nsa_rows], axis=1), win_all, ng, lp)
    win_new = win_all[:, win_all.shape[1] - min(WINDOW, win_all.shape[1]):]
    dqn = rms_norm(dq.reshape(b, t, DIFF_HEADS, 2, DIFF_QK_DIM), lp['diff_q_norm']) * DIFF_QK_DIM ** -0.5
    dkn = rms_norm(dk.reshape(b, t, DIFF_HEADS, 2, DIFF_QK_DIM), lp['diff_k_norm'])
    diff_rows = jnp.stack([dkn.reshape(b, t, DIFF_HEADS, HEAD_DIM), dv.reshape(b, t, DIFF_HEADS, HEAD_DIM)], axis=2)
    o_diff = diff_mixer(dqn, jnp.concatenate([diff_past.astype(dt), diff_rows], axis=1), lp, lam_init)
    o_ml, c, n, m, conv_new = mlstm_mixer(mq, mk, mv, mi, mf, mo, ml_c, ml_n, ml_m, ml_conv, lp)
    h = h + jnp.concatenate([o_nsa, o_diff.astype(dt), o_ml], axis=-1) @ lp['w_out']
    y = h + 0.5 * swiglu(rms_norm(h, lp['ffn2_norm']), lp['ffn2_w13'], lp['ffn2_w2'])
    return y, (nsa_rows, diff_rows, win_new, c, n, m, conv_new)


def setup_inputs(seed: int = 0) -> dict:
    key = jax.random.key(seed)
    keys = jax.random.split(key, 40)
    counter = iter(range(40))

    def nrm(shape, scale):
        return scale * jax.random.normal(keys[next(counter)], shape, jnp.float32)

    def gain(shape):
        return 1.0 + nrm(shape, 0.02)

    n_pages = PAST_LEN // PAGE_SIZE
    used = DEC_BATCH * n_pages
    n_pool = used + max(1, used // 4)
    win_len = min(WINDOW, PAST_LEN)
    x_prompt = nrm((BATCH, SEQ, D_MODEL), 1.0)
    x_sample = nrm((DEC_BATCH, DEC_SEQ, D_MODEL), 1.0)
    cache_nsa = nrm((DEPTH, n_pool, PAGE_SIZE, 4, NSA_KV_HEADS, HEAD_DIM), 1.0)
    cache_diff = nrm((DEPTH, n_pool, PAGE_SIZE, 2, DIFF_HEADS, HEAD_DIM), 1.0)
    state_win = nrm((DEPTH, DEC_BATCH, win_len, 2, NSA_KV_HEADS, HEAD_DIM), 1.0)
    state_ml_c = nrm((DEPTH, DEC_BATCH, MLSTM_HEADS, MLSTM_DIM, MLSTM_DIM), 1.0)
    state_ml_n = nrm((DEPTH, DEC_BATCH, MLSTM_HEADS, MLSTM_DIM), 1.0)
    state_ml_m = nrm((DEPTH, DEC_BATCH, MLSTM_HEADS), 1.0)
    state_ml_conv = nrm((DEPTH, DEC_BATCH, CONV_WIDTH - 1, 2 * MLSTM_W), 1.0)
    page_table = jax.random.permutation(keys[next(counter)], n_pool)[:used].reshape(DEC_BATCH, n_pages).astype(jnp.int32)
    forget_b = jnp.linspace(3.0, 6.0, MLSTM_HEADS, dtype=jnp.float32)[None, None, :] + nrm((DEPTH, 1, MLSTM_HEADS), 0.1)
    input_b = nrm((DEPTH, 1, MLSTM_HEADS), 0.1)
    return {
        'x_prompt': x_prompt,
        'x_sample': x_sample,
        'cache_nsa': cache_nsa,
        'cache_diff': cache_diff,
        'state_win': state_win,
        'state_ml_c': state_ml_c,
        'state_ml_n': state_ml_n,
        'state_ml_m': state_ml_m,
        'state_ml_conv': state_ml_conv,
        'page_table': page_table,
        'ffn1_norm': gain((DEPTH, D_MODEL)),
        'ffn1_w13': nrm((DEPTH, D_MODEL, 2 * D_FF), D_MODEL ** -0.5),
        'ffn1_w2': nrm((DEPTH, D_FF, D_MODEL), D_FF ** -0.5),
        'mix_norm': gain((DEPTH, D_MODEL)),
        'w_in': nrm((DEPTH, D_MODEL, IN_COLS), D_MODEL ** -0.5),
        'nsa_q_norm': gain((DEPTH, HEAD_DIM)),
        'nsa_k_norm': gain((DEPTH, 3, HEAD_DIM)),
        'nsa_cmp_pos': nrm((DEPTH, 2, CMP_BLOCK, HEAD_DIM), 0.1),
        'nsa_cmp_w1': nrm((DEPTH, 2, CMP_BLOCK * HEAD_DIM, HEAD_DIM), (CMP_BLOCK * HEAD_DIM) ** -0.5),
        'nsa_cmp_w2': nrm((DEPTH, 2, HEAD_DIM, HEAD_DIM), HEAD_DIM ** -0.5),
        'diff_q_norm': gain((DEPTH, DIFF_QK_DIM)),
        'diff_k_norm': gain((DEPTH, DIFF_QK_DIM)),
        'diff_lambda': nrm((DEPTH, 4, DIFF_QK_DIM), 0.1),
        'diff_out_norm': gain((DEPTH, HEAD_DIM)),
        'ml_conv_w': nrm((DEPTH, CONV_WIDTH, 2 * MLSTM_W), CONV_WIDTH ** -0.5),
        'ml_conv_b': nrm((DEPTH, 2 * MLSTM_W), 0.01),
        'ml_gate_b': jnp.concatenate([input_b, forget_b], axis=1),
        'ml_out_norm': gain((DEPTH, MLSTM_DIM)),
        'w_out': nrm((DEPTH, MIX_W, D_MODEL), MIX_W ** -0.5),
        'ffn2_norm': gain((DEPTH, D_MODEL)),
        'ffn2_w13': nrm((DEPTH, D_MODEL, 2 * D_FF), D_MODEL ** -0.5),
        'ffn2_w2': nrm((DEPTH, D_FF, D_MODEL), D_FF ** -0.5),
    }


def reference(x_prompt, x_sample, cache_nsa, cache_diff, state_win, state_ml_c, state_ml_n, state_ml_m,
              state_ml_conv, page_table, ffn1_norm, ffn1_w13, ffn1_w2, mix_norm, w_in, nsa_q_norm, nsa_k_norm,
              nsa_cmp_pos, nsa_cmp_w1, nsa_cmp_w2, diff_q_norm, diff_k_norm, diff_lambda, diff_out_norm,
              ml_conv_w, ml_conv_b, ml_gate_b, ml_out_norm, w_out, ffn2_norm, ffn2_w13, ffn2_w2):
    bp, dt = x_prompt.shape[0], x_prompt.dtype
    f32 = jnp.float32
    p_nsa0 = jnp.zeros((bp, 0, 4, NSA_KV_HEADS, HEAD_DIM), dt)
    p_diff0 = jnp.zeros((bp, 0, 2, DIFF_HEADS, HEAD_DIM), dt)
    p_win0 = jnp.zeros((bp, 0, 2, NSA_KV_HEADS, HEAD_DIM), dt)
    p_c0 = jnp.zeros((bp, MLSTM_HEADS, MLSTM_DIM, MLSTM_DIM), f32)
    p_n0 = jnp.zeros((bp, MLSTM_HEADS, MLSTM_DIM), f32)
    p_m0 = jnp.zeros((bp, MLSTM_HEADS), f32)
    p_conv0 = jnp.zeros((bp, CONV_WIDTH - 1, 2 * MLSTM_W), dt)
    y_prompt, y_sample = x_prompt, x_sample
    new_p, new_s = [], []
    for l in range(DEPTH):
        lp = {
            'ffn1_norm': ffn1_norm[l], 'ffn1_w13': ffn1_w13[l], 'ffn1_w2': ffn1_w2[l],
            'mix_norm': mix_norm[l], 'w_in': w_in[l],
            'nsa_q_norm': nsa_q_norm[l], 'nsa_k_norm': nsa_k_norm[l], 'nsa_cmp_pos': nsa_cmp_pos[l],
            'nsa_cmp_w1': nsa_cmp_w1[l], 'nsa_cmp_w2': nsa_cmp_w2[l],
            'diff_q_norm': diff_q_norm[l], 'diff_k_norm': diff_k_norm[l], 'diff_lambda': diff_lambda[l],
            'diff_out_norm': diff_out_norm[l],
            'ml_conv_w': ml_conv_w[l], 'ml_conv_b': ml_conv_b[l], 'ml_gate_b': ml_gate_b[l],
            'ml_out_norm': ml_out_norm[l], 'w_out': w_out[l],
            'ffn2_norm': ffn2_norm[l], 'ffn2_w13': ffn2_w13[l], 'ffn2_w2': ffn2_w2[l],
        }
        lam_init = 0.8 - 0.6 * math.exp(-0.3 * l)
        y_prompt, st_p = decoder_layer(y_prompt, p_nsa0, p_diff0, p_win0, p_c0, p_n0, p_m0, p_conv0, lp, lam_init)
        y_sample, st_s = decoder_layer(y_sample, gather_pages(cache_nsa[l], page_table),
                                       gather_pages(cache_diff[l], page_table), state_win[l], state_ml_c[l],
                                       state_ml_n[l], state_ml_m[l], state_ml_conv[l], lp, lam_init)
        new_p.append(st_p)
        new_s.append(st_s)
    nsa_p, diff_p, win_p, c_p, n_p, m_p, conv_p = [jnp.stack(z) for z in zip(*new_p)]
    nsa_s, diff_s, win_s, c_s, n_s, m_s, conv_s = [jnp.stack(z) for z in zip(*new_s)]
    return (y_prompt, y_sample, nsa_p, nsa_s, diff_p, diff_s, win_p, win_s, c_p, c_s, n_p, n_s, m_p, m_s, conv_p, conv_s)
```

```python
import functools
import math

import numpy as np
import jax
import jax.numpy as jnp
from jax import lax
from jax.experimental import pallas as pl
from jax.experimental.pallas import tpu as pltpu

F32 = jnp.float32
BF16 = jnp.bfloat16
HIGHEST = lax.Precision.HIGHEST

D_MODEL = 2048
D_FF = 5632
HEAD_DIM = 128
NSA_HEADS = 8
NSA_KV_HEADS = 2
NSA_GROUP = NSA_HEADS // NSA_KV_HEADS
CMP_BLOCK = 32
CMP_STRIDE = 16
SLC_BLOCK = 64
SLC_SHIFT = 6
SLC_TOPN = 16
WINDOW = 512
DIFF_HEADS = 4
DIFF_QK_DIM = HEAD_DIM // 2
MLSTM_HEADS = 4
MLSTM_DIM = 128
MLSTM_CHUNK = 64
CONV_WIDTH = 4
NORM_EPS = 1e-6
NEG_INF = -1e30
FORCE_BONUS = 1e3

NSA_W = NSA_HEADS * HEAD_DIM
NSA_KV_W = NSA_KV_HEADS * HEAD_DIM
DIFF_W = DIFF_HEADS * HEAD_DIM
MLSTM_W = MLSTM_HEADS * MLSTM_DIM
IN_SPLITS = (NSA_W, 6 * NSA_KV_W, 3 * NSA_HEADS, DIFF_W, DIFF_W, DIFF_W,
             MLSTM_W, MLSTM_W, MLSTM_W, MLSTM_HEADS, MLSTM_HEADS, MLSTM_W)
IN_COLS = sum(IN_SPLITS)

V7X_LANES = 128
V7X_VMEM_LIMIT_BYTES = 56 * 1024 * 1024


def _round_up(n, m):
    return -(-n // m) * m


def _cparams(semantics):
    return pltpu.CompilerParams(dimension_semantics=semantics, vmem_limit_bytes=V7X_VMEM_LIMIT_BYTES)


def _pick_tile(n, prefs):
    for p in prefs:
        if n % p == 0:
            return p
    return n


def _rms_rows(x, g):
    inv = lax.rsqrt(jnp.mean(x * x, axis=-1, keepdims=True) + NORM_EPS)
    return x * inv * g


def _ffn_body(x_ref, g_ref, wa_ref, wb_ref, w2_ref, o_ref, n_ref):
    @pl.when(pl.program_id(1) == 0)
    def _():
        x = x_ref[...]
        n_ref[...] = _rms_rows(x, g_ref[...]).astype(BF16)
        o_ref[...] = x

    n = n_ref[...]
    a = jnp.dot(n, wa_ref[...], preferred_element_type=F32)
    b = jnp.dot(n, wb_ref[...], preferred_element_type=F32)
    act = (0.5 * a * jax.nn.sigmoid(a) * b).astype(BF16)
    o_ref[...] += jnp.dot(act, w2_ref[...], preferred_element_type=F32)


def _ffn(x, g, w13, w2):
    t, d = x.shape
    f = w2.shape[0]
    tm = _pick_tile(t, (768, 512, 256, 128, 64, 32, 16, 8))
    tf = _pick_tile(f, (512, 256, 128))
    nf = f // tf
    return pl.pallas_call(
        _ffn_body,
        grid=(t // tm, nf),
        in_specs=[
            pl.BlockSpec((tm, d), lambda i, j: (i, 0)),
            pl.BlockSpec((1, d), lambda i, j: (0, 0)),
            pl.BlockSpec((d, tf), lambda i, j: (0, j)),
            pl.BlockSpec((d, tf), lambda i, j: (0, j + nf)),
            pl.BlockSpec((tf, d), lambda i, j: (j, 0)),
        ],
        out_specs=pl.BlockSpec((tm, d), lambda i, j: (i, 0)),
        out_shape=jax.ShapeDtypeStruct((t, d), F32),
        scratch_shapes=[pltpu.VMEM((tm, d), BF16)],
        compiler_params=_cparams(("parallel", "arbitrary")),
        name="ffn",
    )(x, g.reshape(1, d), w13, w13, w2)


def _normproj_body(x_ref, g_ref, w_ref, o_ref, n_ref):
    @pl.when(pl.program_id(1) == 0)
    def _():
        n_ref[...] = _rms_rows(x_ref[...], g_ref[...]).astype(BF16)

    o_ref[...] = jnp.dot(n_ref[...], w_ref[...], preferred_element_type=F32)


def _normproj(x, g, w):
    t, d = x.shape
    n = w.shape[1]
    tm = _pick_tile(t, (768, 512, 256, 128, 64, 32, 16, 8))
    tn = _pick_tile(n, (896, 512, 256, 128))
    return pl.pallas_call(
        _normproj_body,
        grid=(t // tm, n // tn),
        in_specs=[
            pl.BlockSpec((tm, d), lambda i, j: (i, 0)),
            pl.BlockSpec((1, d), lambda i, j: (0, 0)),
            pl.BlockSpec((d, tn), lambda i, j: (0, j)),
        ],
        out_specs=pl.BlockSpec((tm, tn), lambda i, j: (i, j)),
        out_shape=jax.ShapeDtypeStruct((t, n), F32),
        scratch_shapes=[pltpu.VMEM((tm, d), BF16)],
        compiler_params=_cparams(("parallel", "arbitrary")),
        name="normproj",
    )(x, g.reshape(1, d), w)


def _outproj_body(h_ref, o_ref, w_ref, y_ref):
    y_ref[...] = h_ref[...] + jnp.dot(o_ref[...].astype(BF16), w_ref[...], preferred_element_type=F32)


def _outproj(h, o, w):
    t, d = h.shape
    k = o.shape[1]
    tm = _pick_tile(t, (768, 512, 256, 128, 64, 32, 16, 8))
    return pl.pallas_call(
        _outproj_body,
        grid=(t // tm,),
        in_specs=[
            pl.BlockSpec((tm, d), lambda i: (i, 0)),
            pl.BlockSpec((tm, k), lambda i: (i, 0)),
            pl.BlockSpec((k, d), lambda i: (0, 0)),
        ],
        out_specs=pl.BlockSpec((tm, d), lambda i: (i, 0)),
        out_shape=jax.ShapeDtypeStruct((t, d), F32),
        compiler_params=_cparams(("parallel",)),
        name="outproj",
    )(h, o, w)


def _compress_body(x_ref, w1_ref, pos_ref, w2_ref, g_ref, o_ref, *, normed):
    n = x_ref.shape[1]
    w1 = w1_ref[...]
    p = jnp.dot(x_ref[0].astype(BF16), w1, preferred_element_type=F32)
    pb = jnp.dot(pos_ref[...].astype(BF16), w1, preferred_element_type=F32)
    first = p[:, :HEAD_DIM] + pb[0:1, :HEAD_DIM]
    second = p[:, HEAD_DIM:] + pb[1:2, HEAD_DIM:]
    z = first + pltpu.roll(second, n - 1, 0)
    y = jnp.dot((z * jax.nn.sigmoid(z)).astype(BF16), w2_ref[...].astype(BF16), preferred_element_type=F32)
    if normed:
        y = _rms_rows(y, g_ref[...])
    o_ref[0] = y


def _compress(x, pos, w1, w2, gain, normed):
    bk, n, cw = x.shape
    half = CMP_STRIDE * HEAD_DIM
    w1ab = jnp.concatenate([w1[:half], w1[half:]], axis=1).astype(BF16)
    pos2 = jnp.zeros((8, half), F32).at[0].set(pos[:CMP_STRIDE].reshape(-1)).at[1].set(pos[CMP_STRIDE:].reshape(-1))
    return pl.pallas_call(
        functools.partial(_compress_body, normed=normed),
        grid=(bk,),
        in_specs=[
            pl.BlockSpec((1, n, cw), lambda i: (i, 0, 0)),
            pl.BlockSpec((half, 2 * HEAD_DIM), lambda i: (0, 0)),
            pl.BlockSpec((8, half), lambda i: (0, 0)),
            pl.BlockSpec((HEAD_DIM, HEAD_DIM), lambda i: (0, 0)),
            pl.BlockSpec((1, HEAD_DIM), lambda i: (0, 0)),
        ],
        out_specs=pl.BlockSpec((1, n, HEAD_DIM), lambda i: (i, 0, 0)),
        out_shape=jax.ShapeDtypeStruct((bk, n, HEAD_DIM), F32),
        compiler_params=_cparams(("parallel",)),
        name="nsa_compress",
    )(x, w1ab, pos2, w2, gain.reshape(1, HEAD_DIM))


def _cmp_select_body(slope_ref, q_ref, kc_ref, vc_ref, o_ref, sel_ref, *, g_heads, tq, q_off, n_cmp, n_slc, n_sel):
    bh = pl.program_id(0)
    qi = pl.program_id(1)
    n_cpad = kc_ref.shape[1]
    n_spad = sel_ref.shape[2]
    q = q_ref[0].reshape(g_heads * tq, HEAD_DIM)
    s = lax.dot_general(q, kc_ref[0], (((1,), (1,)), ((), ())), precision=HIGHEST, preferred_element_type=F32)
    vc = vc_ref[0].astype(BF16)
    q_lo = q_off + qi * tq
    qpos = q_lo + lax.broadcasted_iota(jnp.int32, (tq, n_cpad), 0)
    ends = lax.broadcasted_iota(jnp.int32, (tq, n_cpad), 1) * CMP_STRIDE + (CMP_BLOCK - 1)
    d = qpos - ends
    mask = d >= 0
    df = d.astype(F32)
    psum = jnp.zeros((tq, n_cpad), F32)
    for g in range(g_heads):
        sg = jnp.where(mask, s[g * tq:(g + 1) * tq] - slope_ref[bh * g_heads + g] * df, NEG_INF)
        m = jnp.max(sg, axis=-1, keepdims=True)
        p = jnp.where(mask, jnp.exp(sg - m), 0.0)
        p = p / jnp.maximum(jnp.sum(p, axis=-1, keepdims=True), 1e-30)
        o_ref[0, g] = jnp.dot(p.astype(BF16), vc, preferred_element_type=F32)
        psum = psum + p
    c_lo = lax.broadcasted_iota(jnp.int32, (n_cpad, n_spad), 0) * CMP_STRIDE
    j_lo = lax.broadcasted_iota(jnp.int32, (n_cpad, n_spad), 1) * SLC_BLOCK
    cover = jnp.where(c_lo < j_lo + SLC_BLOCK, jnp.where(c_lo + CMP_BLOCK > j_lo, 1.0, 0.0), 0.0)
    cover = jnp.where(c_lo < n_cmp * CMP_STRIDE, cover, 0.0)
    imp = jnp.dot(psum, cover, precision=HIGHEST, preferred_element_type=F32)
    qp = q_lo + lax.broadcasted_iota(jnp.int32, (tq, n_spad), 0)
    j = lax.broadcasted_iota(jnp.int32, (tq, n_spad), 1)
    cur = lax.shift_right_logical(qp, SLC_SHIFT)
    forced = jnp.where(j == 0, 1.0, jnp.where(j == cur, 1.0, jnp.where(j == cur - 1, 1.0, 0.0)))
    valid = j * SLC_BLOCK <= qp
    score = jnp.where(valid, imp + FORCE_BONUS * forced, NEG_INF)
    cnt = jnp.zeros((tq, n_spad), F32)
    for i in range(n_slc):
        ci = score[:, i:i + 1]
        cnt = cnt + jnp.where(ci > score, 1.0, jnp.where(ci == score, jnp.where(j > i, 1.0, 0.0), 0.0))
    sel_ref[0] = jnp.where(cnt < n_sel, jnp.where(j < n_slc, 1.0, 0.0), 0.0)


def _cmp_select(q, kc, vc, slopes, *, q_off, n_cmp, n_slc):
    bh, g_heads, t, _ = q.shape
    n_cpad = kc.shape[1]
    n_spad = _round_up(n_slc, V7X_LANES)
    tq = _pick_tile(t, (256, 128, 64, 32, 16, 8))
    body = functools.partial(_cmp_select_body, g_heads=g_heads, tq=tq, q_off=q_off, n_cmp=n_cmp, n_slc=n_slc,
                             n_sel=min(SLC_TOPN, n_slc))
    return pl.pallas_call(
        body,
        grid=(bh, t // tq),
        in_specs=[
            pl.BlockSpec(memory_space=pltpu.SMEM),
            pl.BlockSpec((1, g_heads, tq, HEAD_DIM), lambda b, i: (b, 0, i, 0)),
            pl.BlockSpec((1, n_cpad, HEAD_DIM), lambda b, i: (b, 0, 0)),
            pl.BlockSpec((1, n_cpad, HEAD_DIM), lambda b, i: (b, 0, 0)),
        ],
        out_specs=[
            pl.BlockSpec((1, g_heads, tq, HEAD_DIM), lambda b, i: (b, 0, i, 0)),
            pl.BlockSpec((1, tq, n_spad), lambda b, i: (b, i, 0)),
        ],
        out_shape=[
            jax.ShapeDtypeStruct((bh, g_heads, t, HEAD_DIM), F32),
            jax.ShapeDtypeStruct((bh, t, n_spad), F32),
        ],
        compiler_params=_cparams(("parallel", "parallel")),
        name="nsa_cmp_select",
    )(slopes, q, kc, vc)


def _first_key_tile(qi, *, tq, tk, q_off, k_off, window, n_k):
    if window is None:
        return 0
    return jnp.clip((q_off + qi * tq - window - k_off) // tk, 0, n_k - 1)


def _flash_body(*refs, g_heads, tq, tk, q_off, k_off, window, use_sel, n_k):
    if use_sel:
        slope_ref, q_ref, k_ref, v_ref, sel_ref, o_ref, m_ref, l_ref, acc_ref = refs
    else:
        slope_ref, q_ref, k_ref, v_ref, o_ref, m_ref, l_ref, acc_ref = refs
        sel_ref = None
    bh = pl.program_id(0)
    qi = pl.program_id(1)
    step = pl.program_id(2)
    d_qk = q_ref.shape[3]

    @pl.when(step == 0)
    def _():
        m_ref[...] = jnp.full(m_ref.shape, NEG_INF, F32)
        l_ref[...] = jnp.zeros(l_ref.shape, F32)
        acc_ref[...] = jnp.zeros(acc_ref.shape, F32)

    ki = _first_key_tile(qi, tq=tq, tk=tk, q_off=q_off, k_off=k_off, window=window, n_k=n_k) + step
    q_lo = q_off + qi * tq
    k_lo = k_off + ki * tk
    active = jnp.logical_and(ki < n_k, k_lo <= q_lo + (tq - 1))
    if window is not None:
        active = jnp.logical_and(active, k_lo + (tk - 1) >= q_lo - window)

    @pl.when(active)
    def _():
        q = q_ref[0].reshape(g_heads * tq, d_qk).astype(BF16)
        k = k_ref[0].astype(BF16)
        v = v_ref[0].astype(BF16)
        s = lax.dot_general(q, k, (((1,), (1,)), ((), ())), preferred_element_type=F32)
        d = (q_lo - k_lo) + lax.broadcasted_iota(jnp.int32, (tq, tk), 0) - lax.broadcasted_iota(jnp.int32, (tq, tk), 1)
        df = d.astype(F32)
        keep = jnp.where(d >= 0, 1.0, 0.0)
        if window is not None:
            keep = jnp.where(d <= window, keep, 0.0)
        if use_sel:
            n_spad = sel_ref.shape[2]
            blk = lax.broadcasted_iota(jnp.int32, (n_spad, tk), 0)
            kidx = ki * tk + lax.broadcasted_iota(jnp.int32, (n_spad, tk), 1)
            expand = jnp.where(lax.shift_right_logical(kidx, SLC_SHIFT) == blk, 1.0, 0.0).astype(BF16)
            keep = keep * jnp.dot(sel_ref[0].astype(BF16), expand, preferred_element_type=F32)
        mask = keep > 0.5
        for g in range(g_heads):
            rows = slice(g * tq, (g + 1) * tq)
            sg = jnp.where(mask, s[rows] - slope_ref[bh * g_heads + g] * df, NEG_INF)
            m_old = m_ref[rows]
            m_new = jnp.maximum(m_old, jnp.max(sg, axis=-1, keepdims=True))
            p = jnp.where(mask, jnp.exp(sg - m_new), 0.0)
            alpha = jnp.exp(m_old - m_new)
            l_ref[rows] = alpha * l_ref[rows] + jnp.sum(p, axis=-1, keepdims=True)
            acc_ref[rows] = alpha * acc_ref[rows] + jnp.dot(p.astype(BF16), v, preferred_element_type=F32)
            m_ref[rows] = m_new

    @pl.when(step == pl.num_programs(2) - 1)
    def _():
        out = acc_ref[...] / jnp.maximum(l_ref[...], 1e-30)
        o_ref[0] = out.reshape(o_ref.shape[1:])


def _flash(q, k, v, slopes, *, q_off, k_off=0, window=None, sel=None, k_of_bh=None, v_of_bh=None, tq_pref=128,
           tk_pref=256):
    bh, g_heads, t, d_qk = q.shape
    tk_total = k.shape[1]
    d_v = v.shape[2]
    tq = _pick_tile(t, (tq_pref, 128, 64, 32, 16, 8))
    tk = _pick_tile(tk_total, (tk_pref, 512, 256, 128))
    n_k = tk_total // tk
    n_steps = n_k if window is None else min(n_k, (window + tq - 1) // tk + 2)
    k_of_bh = k_of_bh or (lambda b: b)
    v_of_bh = v_of_bh or (lambda b: b)
    first = functools.partial(_first_key_tile, tq=tq, tk=tk, q_off=q_off, k_off=k_off, window=window, n_k=n_k)

    def k_block(qi, step):
        hi = jnp.clip((q_off + qi * tq + (tq - 1) - k_off) // tk, 0, n_k - 1)
        return jnp.minimum(first(qi) + step, hi)

    in_specs = [
        pl.BlockSpec(memory_space=pltpu.SMEM),
        pl.BlockSpec((1, g_heads, tq, d_qk), lambda b, i, j: (b, 0, i, 0)),
        pl.BlockSpec((1, tk, d_qk), lambda b, i, j: (k_of_bh(b), k_block(i, j), 0)),
        pl.BlockSpec((1, tk, d_v), lambda b, i, j: (v_of_bh(b), k_block(i, j), 0)),
    ]
    args = [slopes, q, k, v]
    if sel is not None:
        in_specs.append(pl.BlockSpec((1, tq, sel.shape[2]), lambda b, i, j: (b, i, 0)))
        args.append(sel)
    body = functools.partial(_flash_body, g_heads=g_heads, tq=tq, tk=tk, q_off=q_off, k_off=k_off, window=window,
                             use_sel=sel is not None, n_k=n_k)
    return pl.pallas_call(
        body,
        grid=(bh, t // tq, n_steps),
        in_specs=in_specs,
        out_specs=pl.BlockSpec((1, g_heads, tq, d_v), lambda b, i, j: (b, 0, i, 0)),
        out_shape=jax.ShapeDtypeStruct((bh, g_heads, t, d_v), F32),
        scratch_shapes=[
            pltpu.VMEM((g_heads * tq, 1), F32),
            pltpu.VMEM((g_heads * tq, 1), F32),
            pltpu.VMEM((g_heads * tq, d_v), F32),
        ],
        compiler_params=_cparams(("parallel", "parallel", "arbitrary")),
        name="flash_attn",
    )(*args)


def _mlstm_body(q_ref, k_ref, v_ref, o_ref, li_ref, lf_ref, c0_ref, n0_ref, m0_ref, g_ref,
                h_ref, c_ref, n_ref, m_ref, *, lc, nc):
    row = lax.broadcasted_iota(jnp.int32, (lc, lc), 0)
    col = lax.broadcasted_iota(jnp.int32, (lc, lc), 1)
    tril = row >= col
    triu = row <= col
    eye = row == col
    gain = g_ref[...]

    def to_col(r):
        return jnp.sum(jnp.where(eye, jnp.broadcast_to(r, (lc, lc)), 0.0), axis=1, keepdims=True)

    def step(ci, carry):
        c, n, m = carry
        rows = pl.ds(ci * lc if nc == 1 else pl.multiple_of(ci * lc, lc), lc)
        qc = q_ref[0, rows, :]
        kc = k_ref[0, rows, :]
        vc = v_ref[0, rows, :]
        li = li_ref[0, pl.ds(ci, 1), :]
        lf = lf_ref[0, pl.ds(ci, 1), :]
        lf_b = jnp.broadcast_to(lf, (lc, lc))
        bcum_c = jnp.sum(jnp.where(tril, lf_b, 0.0), axis=1, keepdims=True)
        bcum_r = jnp.sum(jnp.where(triu, jnp.broadcast_to(to_col(lf), (lc, lc)), 0.0), axis=0, keepdims=True)
        dmat = jnp.where(tril, bcum_c - bcum_r + li, NEG_INF)
        inter = bcum_c + m
        mj = jnp.maximum(inter, jnp.max(dmat, axis=1, keepdims=True))
        wts = jnp.exp(dmat - mj)
        a = jnp.exp(inter - mj)
        sqk = lax.dot_general(qc, kc, (((1,), (1,)), ((), ())), precision=HIGHEST, preferred_element_type=F32) * wts
        num = a * jnp.dot(qc, c, precision=HIGHEST, preferred_element_type=F32) + \
            jnp.dot(sqk, vc, precision=HIGHEST, preferred_element_type=F32)
        den = a * jnp.sum(qc * n, axis=1, keepdims=True) + jnp.sum(sqk, axis=1, keepdims=True)
        h = num / jnp.maximum(jnp.abs(den), jnp.exp(-mj))
        hn = _rms_rows(h, gain) * jax.nn.sigmoid(o_ref[0, rows, :])
        h_ref[0, rows, :] = hn
        b_last = jnp.sum(lf, axis=1, keepdims=True)
        gl = b_last - bcum_r + li
        m_new = jnp.maximum(b_last + m, jnp.max(gl, axis=1, keepdims=True))
        decay = jnp.exp(b_last + m - m_new)
        kw = kc * to_col(jnp.exp(gl - m_new))
        c_new = decay * c + lax.dot_general(kw, vc, (((0,), (0,)), ((), ())), precision=HIGHEST,
                                            preferred_element_type=F32)
        n_new = decay * n + jnp.sum(kw, axis=0, keepdims=True)
        return c_new, n_new, m_new

    init = (c0_ref[0], n0_ref[0], m0_ref[0][:, 0:1])
    if nc == 1:
        c, n, m = step(0, init)
    else:
        c, n, m = lax.fori_loop(0, nc, step, init)
    c_ref[0] = c
    n_ref[0] = n
    m_ref[0] = jnp.broadcast_to(m, (1, MLSTM_DIM))


def _mlstm(q, k, v, o_gate, log_i, log_f, c0, n0, m0, gain):
    bh, t, hd = q.shape
    lc = MLSTM_CHUNK if t % MLSTM_CHUNK == 0 else t
    nc = t // lc
    seq = pl.BlockSpec((1, t, hd), lambda b: (b, 0, 0))
    gate = pl.BlockSpec((1, nc, lc), lambda b: (b, 0, 0))
    vec = pl.BlockSpec((1, 1, hd), lambda b: (b, 0, 0))
    mat = pl.BlockSpec((1, hd, hd), lambda b: (b, 0, 0))
    h, c, n, m = pl.pallas_call(
        functools.partial(_mlstm_body, lc=lc, nc=nc),
        grid=(bh,),
        in_specs=[seq, seq, seq, seq, gate, gate, mat, vec, vec, pl.BlockSpec((1, hd), lambda b: (0, 0))],
        out_specs=[seq, mat, vec, vec],
        out_shape=[
            jax.ShapeDtypeStruct((bh, t, hd), F32),
            jax.ShapeDtypeStruct((bh, hd, hd), F32),
            jax.ShapeDtypeStruct((bh, 1, hd), F32),
            jax.ShapeDtypeStruct((bh, 1, hd), F32),
        ],
        compiler_params=_cparams(("parallel",)),
        name="mlstm",
    )(q, k, v, o_gate, log_i.reshape(bh, nc, lc), log_f.reshape(bh, nc, lc), c0, n0.reshape(bh, 1, hd),
      jnp.broadcast_to(m0.reshape(bh, 1, 1), (bh, 1, hd)), gain.reshape(1, hd))
    return h, c, n[:, 0], m[:, 0, 0]


def _rms(x, g):
    return x * lax.rsqrt(jnp.mean(x * x, axis=-1, keepdims=True) + NORM_EPS) * g


def _alibi(n):
    return np.asarray(2.0 ** (-8.0 * np.arange(1, n + 1) / n), dtype=np.float32)


def _pad_rows(x, n, axis=1):
    extra = n - x.shape[axis]
    if extra == 0:
        return x
    pads = [(0, 0)] * x.ndim
    pads[axis] = (0, extra)
    return jnp.pad(x, pads)


def _heads_major(x):
    b, length, h, d = x.shape
    return jnp.swapaxes(x, 1, 2).reshape(b * h, length, d)


def _nsa(nq, rows_all, win_all, gates, lp, *, past):
    b, t = nq.shape[0], nq.shape[1]
    length = rows_all.shape[1]
    bk = b * NSA_KV_HEADS
    slopes = jnp.asarray(np.tile(_alibi(NSA_HEADS), b))
    qn = _rms(nq.reshape(b, t, NSA_KV_HEADS, NSA_GROUP, HEAD_DIM), lp['nsa_q_norm']) * HEAD_DIM ** -0.5
    qn = qn.transpose(0, 2, 3, 1, 4).reshape(bk, NSA_GROUP, t, HEAD_DIM)

    n_chunk = -(-length // CMP_STRIDE)
    n_cmp = n_chunk - CMP_BLOCK // CMP_STRIDE + 1
    n_cpad = _round_up(n_chunk, V7X_LANES)

    def chunks(x):
        x = _pad_rows(_heads_major(x), n_cpad * CMP_STRIDE)
        return x.reshape(bk, n_cpad, CMP_STRIDE * HEAD_DIM)

    k_cmp = _compress(chunks(rows_all[:, :, 0]), lp['nsa_cmp_pos'][0], lp['nsa_cmp_w1'][0], lp['nsa_cmp_w2'][0],
                      lp['nsa_k_norm'][0], True)
    v_cmp = _compress(chunks(rows_all[:, :, 1]), lp['nsa_cmp_pos'][1], lp['nsa_cmp_w1'][1], lp['nsa_cmp_w2'][1],
                      lp['nsa_k_norm'][0], False)
    n_slc = -(-length // SLC_BLOCK)
    o_cmp, sel = _cmp_select(qn, k_cmp, v_cmp, slopes, q_off=past, n_cmp=n_cmp, n_slc=n_slc)

    tk_pad = _round_up(length, 1024 if t < 128 else 256)
    k_slc = _pad_rows(_heads_major(rows_all[:, :, 2]), tk_pad)
    v_slc = _pad_rows(_heads_major(rows_all[:, :, 3]), tk_pad)
    o_slc = _flash(qn, k_slc, v_slc, slopes, q_off=past, sel=sel, tk_pref=1024 if t < 128 else 256)

    lw = win_all.shape[1]
    lw_pad = _round_up(lw, 128)
    k_win = _pad_rows(_heads_major(win_all[:, :, 0]), lw_pad)
    v_win = _pad_rows(_heads_major(win_all[:, :, 1]), lw_pad)
    o_win = _flash(qn, k_win, v_win, slopes, q_off=past, k_off=past + t - lw, window=WINDOW, tk_pref=128)

    g = jax.nn.sigmoid(gates).reshape(b, t, NSA_KV_HEADS, NSA_GROUP, 3)

    def back(o):
        return o.reshape(b, NSA_KV_HEADS, NSA_GROUP, t, HEAD_DIM).transpose(0, 3, 1, 2, 4)

    o = g[..., 0:1] * back(o_cmp) + g[..., 1:2] * back(o_slc) + g[..., 2:3] * back(o_win)
    return o.reshape(b, t, NSA_W)


def _diff(dqn, rows_all, lp, lam_init, *, past):
    b, t = dqn.shape[0], dqn.shape[1]
    length = rows_all.shape[1]
    lam_p = lp['diff_lambda']
    lam = jnp.exp(jnp.sum(lam_p[0] * lam_p[1])) - jnp.exp(jnp.sum(lam_p[2] * lam_p[3])) + lam_init
    slopes = jnp.asarray(np.tile(np.repeat(_alibi(DIFF_HEADS), 2), b))
    tk_pad = _round_up(length, 1024 if t < 128 else 256)
    q = dqn.transpose(0, 2, 3, 1, 4).reshape(b * DIFF_HEADS * 2, 1, t, DIFF_QK_DIM)
    k = rows_all[:, :, 0].reshape(b, length, DIFF_HEADS, 2, DIFF_QK_DIM).transpose(0, 2, 3, 1, 4)
    k = _pad_rows(k.reshape(b * DIFF_HEADS * 2, length, DIFF_QK_DIM), tk_pad)
    v = _pad_rows(_heads_major(rows_all[:, :, 1]), tk_pad)
    o = _flash(q, k, v, slopes, q_off=past, v_of_bh=lambda i: i // 2, tk_pref=1024 if t < 128 else 256)
    o = o.reshape(b, DIFF_HEADS, 2, t, HEAD_DIM)
    o = (o[:, :, 0] - lam * o[:, :, 1]).transpose(0, 2, 1, 3)
    o = _rms(o, lp['diff_out_norm']) * (1.0 - lam_init)
    return o.reshape(b, t, DIFF_W)


def _mlstm_mixer(mq, mk, mv, mi, mf, mo, c0, n0, m0, conv_buf, lp):
    b, t = mq.shape[0], mq.shape[1]
    ext = jnp.concatenate([conv_buf, jnp.concatenate([mq, mk], axis=-1)], axis=1)
    y = lp['ml_conv_b']
    for i in range(CONV_WIDTH):
        y = y + ext[:, i:i + t] * lp['ml_conv_w'][i]
    qk = jax.nn.silu(y)
    conv_new = ext[:, ext.shape[1] - (CONV_WIDTH - 1):]
    bh = b * MLSTM_HEADS

    def heads(x):
        return _heads_major(x.reshape(b, t, MLSTM_HEADS, MLSTM_DIM))

    q = heads(qk[..., :MLSTM_W]) * MLSTM_DIM ** -0.5
    k = heads(qk[..., MLSTM_W:])
    log_i = (mi + lp['ml_gate_b'][0]).transpose(0, 2, 1).reshape(bh, t)
    log_f = jax.nn.log_sigmoid(mf + lp['ml_gate_b'][1]).transpose(0, 2, 1).reshape(bh, t)
    h, c, n, m = _mlstm(q, k, heads(mv), heads(mo), log_i, log_f, c0.reshape(bh, MLSTM_DIM, MLSTM_DIM),
                        n0.reshape(bh, MLSTM_DIM), m0.reshape(bh), lp['ml_out_norm'])
    h = h.reshape(b, MLSTM_HEADS, t, MLSTM_DIM).transpose(0, 2, 1, 3).reshape(b, t, MLSTM_W)
    return (h, c.reshape(b, MLSTM_HEADS, MLSTM_DIM, MLSTM_DIM), n.reshape(b, MLSTM_HEADS, MLSTM_DIM),
            m.reshape(b, MLSTM_HEADS), conv_new)


_MAIN_ORDER = (0, 1, 3, 4, 5, 6, 7, 8, 11)
_SMALL_ORDER = (2, 9, 10)


def _pack_w_in(w_in):
    cuts = np.cumsum((0,) + IN_SPLITS)
    cols = [w_in[..., cuts[i]:cuts[i + 1]] for i in range(len(IN_SPLITS))]
    packed = jnp.concatenate([cols[i] for i in _MAIN_ORDER] + [cols[i] for i in _SMALL_ORDER], axis=-1)
    return _pad_rows(packed, _round_up(IN_COLS, V7X_LANES), axis=packed.ndim - 1).astype(BF16)


def _split_u(u):
    sizes = [IN_SPLITS[i] for i in _MAIN_ORDER + _SMALL_ORDER]
    cuts = np.cumsum([0] + sizes)
    parts = [u[..., cuts[i]:cuts[i + 1]] for i in range(len(sizes))]
    out = [None] * len(IN_SPLITS)
    for p, i in zip(parts, _MAIN_ORDER + _SMALL_ORDER):
        out[i] = p
    return out


def _mixers(u, b, t, nsa_past, diff_past, win_buf, ml_c, ml_n, ml_m, ml_conv, lp, lam_init):
    nq, nkv, ng, dq, dk, dv, mq, mk, mv, mi, mf, mo = _split_u(u.reshape(b, t, -1))
    past = nsa_past.shape[1]
    nkv = nkv.reshape(b, t, 6, NSA_KV_HEADS, HEAD_DIM)
    k_slc = _rms(nkv[:, :, 2], lp['nsa_k_norm'][1])
    k_win = _rms(nkv[:, :, 4], lp['nsa_k_norm'][2])
    nsa_rows = jnp.stack([nkv[:, :, 0], nkv[:, :, 1], k_slc, nkv[:, :, 3]], axis=2)
    win_all = jnp.concatenate([win_buf, jnp.stack([k_win, nkv[:, :, 5]], axis=2)], axis=1)
    o_nsa = _nsa(nq, jnp.concatenate([nsa_past, nsa_rows], axis=1), win_all, ng, lp, past=past)
    win_new = win_all[:, win_all.shape[1] - min(WINDOW, win_all.shape[1]):]
    dqn = _rms(dq.reshape(b, t, DIFF_HEADS, 2, DIFF_QK_DIM), lp['diff_q_norm']) * DIFF_QK_DIM ** -0.5
    dkn = _rms(dk.reshape(b, t, DIFF_HEADS, 2, DIFF_QK_DIM), lp['diff_k_norm'])
    diff_rows = jnp.stack([dkn.reshape(b, t, DIFF_HEADS, HEAD_DIM), dv.reshape(b, t, DIFF_HEADS, HEAD_DIM)], axis=2)
    o_diff = _diff(dqn, jnp.concatenate([diff_past, diff_rows], axis=1), lp, lam_init, past=past)
    o_ml, c, n, m, conv_new = _mlstm_mixer(mq, mk, mv, mi, mf, mo, ml_c, ml_n, ml_m, ml_conv, lp)
    o = jnp.concatenate([o_nsa, o_diff, o_ml], axis=-1).reshape(b * t, -1)
    return o, (nsa_rows, diff_rows, win_new, c, n, m, conv_new)


def _trunk(x_prompt, x_sample, cache_nsa, cache_diff, state_win, state_ml_c, state_ml_n, state_ml_m, state_ml_conv,
           page_table, p):
    bp, tp, d = x_prompt.shape
    bs, ts, _ = x_sample.shape
    depth = p['w_in'].shape[0]
    n_p = bp * tp
    w_in = _pack_w_in(p['w_in'])
    w_out = p['w_out'].astype(BF16)
    w13 = (p['ffn1_w13'].astype(BF16), p['ffn2_w13'].astype(BF16))
    w2 = (p['ffn1_w2'].astype(BF16), p['ffn2_w2'].astype(BF16))
    per_layer = ('mix_norm', 'nsa_q_norm', 'nsa_k_norm', 'nsa_cmp_pos', 'nsa_cmp_w1', 'nsa_cmp_w2', 'diff_q_norm',
                 'diff_k_norm', 'diff_lambda', 'diff_out_norm', 'ml_conv_w', 'ml_conv_b', 'ml_gate_b', 'ml_out_norm')
    p_nsa0 = jnp.zeros((bp, 0, 4, NSA_KV_HEADS, HEAD_DIM), F32)
    p_diff0 = jnp.zeros((bp, 0, 2, DIFF_HEADS, HEAD_DIM), F32)
    p_win0 = jnp.zeros((bp, 0, 2, NSA_KV_HEADS, HEAD_DIM), F32)
    p_c0 = jnp.zeros((bp, MLSTM_HEADS, MLSTM_DIM, MLSTM_DIM), F32)
    p_n0 = jnp.zeros((bp, MLSTM_HEADS, MLSTM_DIM), F32)
    p_m0 = jnp.zeros((bp, MLSTM_HEADS), F32)
    p_conv0 = jnp.zeros((bp, CONV_WIDTH - 1, 2 * MLSTM_W), F32)

    def gather_pages(pool):
        g = pool[page_table]
        return g.reshape((g.shape[0], g.shape[1] * g.shape[2]) + g.shape[3:])

    x = jnp.concatenate([x_prompt.reshape(n_p, d), x_sample.reshape(bs * ts, d)], axis=0)
    new_p, new_s = [], []
    for l in range(depth):
        lp = {name: p[name][l] for name in per_layer}
        lam_init = 0.8 - 0.6 * math.exp(-0.3 * l)
        h = _ffn(x, p['ffn1_norm'][l], w13[0][l], w2[0][l])
        u = _normproj(h, lp['mix_norm'], w_in[l])
        o_p, st_p = _mixers(u[:n_p], bp, tp, p_nsa0, p_diff0, p_win0, p_c0, p_n0, p_m0, p_conv0, lp, lam_init)
        o_s, st_s = _mixers(u[n_p:], bs, ts, gather_pages(cache_nsa[l]), gather_pages(cache_diff[l]), state_win[l],
                            state_ml_c[l], state_ml_n[l], state_ml_m[l], state_ml_conv[l], lp, lam_init)
        h = _outproj(h, jnp.concatenate([o_p, o_s], axis=0), w_out[l])
        x = _ffn(h, p['ffn2_norm'][l], w13[1][l], w2[1][l])
        new_p.append(st_p)
        new_s.append(st_s)
    nsa_p, diff_p, win_p, c_p, nn_p, m_p, conv_p = [jnp.stack(z) for z in zip(*new_p)]
    nsa_s, diff_s, win_s, c_s, nn_s, m_s, conv_s = [jnp.stack(z) for z in zip(*new_s)]
    return (x[:n_p].reshape(bp, tp, d), x[n_p:].reshape(bs, ts, d), nsa_p, nsa_s, diff_p, diff_s, win_p, win_s,
            c_p, c_s, nn_p, nn_s, m_p, m_s, conv_p, conv_s)


def kernel(x_prompt, x_sample, cache_nsa, cache_diff, state_win, state_ml_c, state_ml_n, state_ml_m, state_ml_conv,
           page_table, ffn1_norm, ffn1_w13, ffn1_w2, mix_norm, w_in, nsa_q_norm, nsa_k_norm, nsa_cmp_pos, nsa_cmp_w1,
           nsa_cmp_w2, diff_q_norm, diff_k_norm, diff_lambda, diff_out_norm, ml_conv_w, ml_conv_b, ml_gate_b,
           ml_out_norm, w_out, ffn2_norm, ffn2_w13, ffn2_w2):
    params = dict(ffn1_norm=ffn1_norm, ffn1_w13=ffn1_w13, ffn1_w2=ffn1_w2, mix_norm=mix_norm, w_in=w_in,
                  nsa_q_norm=nsa_q_norm, nsa_k_norm=nsa_k_norm, nsa_cmp_pos=nsa_cmp_pos, nsa_cmp_w1=nsa_cmp_w1,
                  nsa_cmp_w2=nsa_cmp_w2, diff_q_norm=diff_q_norm, diff_k_norm=diff_k_norm, diff_lambda=diff_lambda,
                  diff_out_norm=diff_out_norm, ml_conv_w=ml_conv_w, ml_conv_b=ml_conv_b, ml_gate_b=ml_gate_b,
                  ml_out_norm=ml_out_norm, w_out=w_out, ffn2_norm=ffn2_norm, ffn2_w13=ffn2_w13, ffn2_w2=ffn2_w2)
    return _trunk(x_prompt, x_sample, cache_nsa, cache_diff, state_win, state_ml_c, state_ml_n, state_ml_m,
                  state_ml_conv, page_table, params)
```

```python
import functools
import math
from typing import NamedTuple

import numpy as np
import jax
import jax.numpy as jnp
from jax import lax
from jax.experimental import pallas as pl
from jax.experimental.pallas import tpu as pltpu

F32 = jnp.float32
BF16 = jnp.bfloat16
HIGHEST = lax.Precision.HIGHEST
NT = (((1,), (1,)), ((), ()))

D_MODEL = 2048
D_FF = 5632
PAGE_SIZE = 128
HEAD_DIM = 128
NSA_HEADS = 8
NSA_KV_HEADS = 2
NSA_GROUP = NSA_HEADS // NSA_KV_HEADS
CMP_BLOCK = 32
CMP_STRIDE = 16
SLC_BLOCK = 64
SLC_SHIFT = 6
SLC_TOPN = 16
WINDOW = 512
DIFF_HEADS = 4
DIFF_QK_DIM = HEAD_DIM // 2
MLSTM_HEADS = 4
MLSTM_DIM = 128
MLSTM_CHUNK = 64
CONV_WIDTH = 4
NORM_EPS = 1e-6
NEG_INF = -1e30
FORCE_BONUS = 1e3

NSA_W = NSA_HEADS * HEAD_DIM
NSA_KV_W = NSA_KV_HEADS * HEAD_DIM
DIFF_W = DIFF_HEADS * HEAD_DIM
MLSTM_W = MLSTM_HEADS * MLSTM_DIM
IN_SPLITS = (NSA_W, 6 * NSA_KV_W, 3 * NSA_HEADS, DIFF_W, DIFF_W, DIFF_W,
             MLSTM_W, MLSTM_W, MLSTM_W, MLSTM_HEADS, MLSTM_HEADS, MLSTM_W)
IN_COLS = sum(IN_SPLITS)

V7X_LANES = 128
V7X_SUBLANES = 8
V7X_VMEM_LIMIT_BYTES = 56 * 1024 * 1024

U_TILE = 256
C_NQ = 0
C_ROWS = C_NQ + NSA_W
C_WIN = C_ROWS + 4 * NSA_KV_W
C_DQ = C_WIN + 2 * NSA_KV_W
C_DROWS = C_DQ + DIFF_W
C_MQ = C_DROWS + 2 * DIFF_W
C_MK = C_MQ + MLSTM_W
C_MV = C_MK + MLSTM_W
C_MO = C_MV + MLSTM_W
C_SMALL = C_MO + MLSTM_W
U_COLS = C_SMALL + U_TILE
G_NSA, G_MI, G_MF = 0, 3 * NSA_HEADS, 3 * NSA_HEADS + MLSTM_HEADS
U_KINDS = (1, 1, 1, 1, 0, 0, 1, 0, 1, 0, 2, 2, 2, 2, 0, 0, 0, 0, 0, 0, 0, 0, 0, 0, 0)
_MAIN_ORDER = (0, 1, 3, 4, 5, 6, 7, 8, 11)
_SMALL_ORDER = (2, 9, 10)


class _Group(NamedTuple):
    b: int
    t: int
    row0: int
    past: int


def _round_up(n, m):
    return -(-n // m) * m


def _cparams(semantics):
    return pltpu.CompilerParams(dimension_semantics=semantics, vmem_limit_bytes=V7X_VMEM_LIMIT_BYTES)


def _pick_tile(n, prefs):
    for p in prefs:
        if n % p == 0:
            return p
    return n


def _alibi(n):
    return [float(2.0 ** (-8.0 * i / n)) for i in range(1, n + 1)]


def _rms_rows(x, g):
    inv = lax.rsqrt(jnp.mean(x * x, axis=-1, keepdims=True) + NORM_EPS)
    return x * inv * g


def _ffn_body(x_ref, g_ref, wa_ref, wb_ref, w2_ref, o_ref, n_ref):
    @pl.when(pl.program_id(1) == 0)
    def _():
        x = x_ref[...]
        n_ref[...] = _rms_rows(x, g_ref[...]).astype(BF16)
        o_ref[...] = x

    n = n_ref[...]
    a = jnp.dot(n, wa_ref[...], preferred_element_type=F32)
    b = jnp.dot(n, wb_ref[...], preferred_element_type=F32)
    act = (0.5 * a * jax.nn.sigmoid(a) * b).astype(BF16)
    o_ref[...] += jnp.dot(act, w2_ref[...], preferred_element_type=F32)


def _ffn(x, g, w13, w2):
    t, d = x.shape
    f = w2.shape[0]
    tm = _pick_tile(t, (768, 512, 256, 128, 64, 32, 16, 8))
    tf = _pick_tile(f, (512, 256, 128))
    nf = f // tf
    return pl.pallas_call(
        _ffn_body,
        grid=(t // tm, nf),
        in_specs=[
            pl.BlockSpec((tm, d), lambda i, j: (i, 0)),
            pl.BlockSpec((1, d), lambda i, j: (0, 0)),
            pl.BlockSpec((d, tf), lambda i, j: (0, j)),
            pl.BlockSpec((d, tf), lambda i, j: (0, j + nf)),
            pl.BlockSpec((tf, d), lambda i, j: (j, 0)),
        ],
        out_specs=pl.BlockSpec((tm, d), lambda i, j: (i, 0)),
        out_shape=jax.ShapeDtypeStruct((t, d), F32),
        scratch_shapes=[pltpu.VMEM((tm, d), BF16)],
        compiler_params=_cparams(("parallel", "arbitrary")),
        name="ffn",
    )(x, g.reshape(1, d), w13, w13, w2)


def _inproj_body(kind_ref, x_ref, g_ref, w_ref, gain_ref, o_ref, n_ref):
    j = pl.program_id(1)

    @pl.when(j == 0)
    def _():
        n_ref[...] = _rms_rows(x_ref[...], g_ref[...]).astype(BF16)

    y = jnp.dot(n_ref[...], w_ref[...], preferred_element_type=F32)
    kind = kind_ref[j]
    gain = gain_ref[...]
    halves = [slice(h * HEAD_DIM, (h + 1) * HEAD_DIM) for h in range(U_TILE // HEAD_DIM)]

    @pl.when(kind == 0)
    def _():
        o_ref[...] = y

    @pl.when(kind == 1)
    def _():
        for cols in halves:
            o_ref[:, cols] = _rms_rows(y[:, cols], gain[:, cols])

    @pl.when(kind == 2)
    def _():
        low = lax.broadcasted_iota(jnp.int32, (y.shape[0], HEAD_DIM), 1) < DIFF_QK_DIM
        for cols in halves:
            yh = y[:, cols]
            sq = yh * yh
            s_lo = jnp.sum(jnp.where(low, sq, 0.0), axis=-1, keepdims=True)
            s_hi = jnp.sum(jnp.where(low, 0.0, sq), axis=-1, keepdims=True)
            inv = lax.rsqrt(jnp.where(low, s_lo, s_hi) * (1.0 / DIFF_QK_DIM) + NORM_EPS)
            o_ref[:, cols] = yh * inv * gain[:, cols]


def _inproj(x, g, w, gain):
    t, d = x.shape
    tm = _pick_tile(t, (768, 512, 256, 128, 64, 32, 16, 8))
    return pl.pallas_call(
        _inproj_body,
        grid=(t // tm, U_COLS // U_TILE),
        in_specs=[
            pl.BlockSpec(memory_space=pltpu.SMEM),
            pl.BlockSpec((tm, d), lambda i, j: (i, 0)),
            pl.BlockSpec((1, d), lambda i, j: (0, 0)),
            pl.BlockSpec((d, U_TILE), lambda i, j: (0, j)),
            pl.BlockSpec((1, U_TILE), lambda i, j: (0, j)),
        ],
        out_specs=pl.BlockSpec((tm, U_TILE), lambda i, j: (i, j)),
        out_shape=jax.ShapeDtypeStruct((t, U_COLS), F32),
        scratch_shapes=[pltpu.VMEM((tm, d), BF16)],
        compiler_params=_cparams(("parallel", "arbitrary")),
        name="inproj",
    )(jnp.asarray(U_KINDS, jnp.int32), x, g.reshape(1, d), w, gain)


def _outproj_body(h_ref, a_ref, b_ref, c_ref, w_ref, y_ref):
    ka, kb = a_ref.shape[1], b_ref.shape[1]
    y = h_ref[...] + jnp.dot(a_ref[...].astype(BF16), w_ref[0:ka, :], preferred_element_type=F32)
    y = y + jnp.dot(b_ref[...].astype(BF16), w_ref[ka:ka + kb, :], preferred_element_type=F32)
    y_ref[...] = y + jnp.dot(c_ref[...].astype(BF16), w_ref[ka + kb:, :], preferred_element_type=F32)


def _outproj(h, o_nsa, o_diff, o_ml, w):
    t, d = h.shape
    tm = _pick_tile(t, (768, 512, 256, 128, 64, 32, 16, 8))
    row = lambda a: pl.BlockSpec((tm, a.shape[1]), lambda i: (i, 0))
    return pl.pallas_call(
        _outproj_body,
        grid=(t // tm,),
        in_specs=[row(h), row(o_nsa), row(o_diff), row(o_ml), pl.BlockSpec(w.shape, lambda i: (0, 0))],
        out_specs=row(h),
        out_shape=jax.ShapeDtypeStruct((t, d), F32),
        compiler_params=_cparams(("parallel",)),
        name="outproj",
    )(h, o_nsa, o_diff, o_ml, w)


def _online_update(s, keep, df, slopes, v, m_ref, l_ref, acc_ref, base, tq):
    for g, slope in enumerate(slopes):
        rows = slice(base + g * tq, base + (g + 1) * tq)
        sg = s[g * tq:(g + 1) * tq] - slope * df
        if keep is not None:
            sg = jnp.where(keep, sg, NEG_INF)
        m_old = m_ref[rows]
        m_new = jnp.maximum(m_old, jnp.max(sg, axis=-1, keepdims=True))
        p = jnp.exp(sg - m_new)
        if keep is not None:
            p = jnp.where(keep, p, 0.0)
        alpha = jnp.exp(m_old - m_new)
        l_ref[rows] = alpha * l_ref[rows] + jnp.sum(p, axis=-1, keepdims=True)
        acc_ref[rows] = alpha * acc_ref[rows] + jnp.dot(p.astype(BF16), v, preferred_element_type=F32)
        m_ref[rows] = m_new


def _init_state(m_ref, l_ref, acc_ref):
    m_ref[...] = jnp.full(m_ref.shape, NEG_INF, F32)
    l_ref[...] = jnp.zeros(l_ref.shape, F32)
    acc_ref[...] = jnp.zeros(acc_ref.shape, F32)


def _normalised(l_ref, acc_ref, rows):
    return acc_ref[rows] / jnp.maximum(l_ref[rows], 1e-30)


def _distance(tq, n, offset):
    return offset + lax.broadcasted_iota(jnp.int32, (tq, n), 0) - lax.broadcasted_iota(jnp.int32, (tq, n), 1)


def _block_expand(n_spad, n, first_key):
    blk = lax.broadcasted_iota(jnp.int32, (n_spad, n), 0)
    key = first_key + lax.broadcasted_iota(jnp.int32, (n_spad, n), 1)
    return jnp.where(lax.shift_right_logical(key, SLC_SHIFT) == blk, 1.0, 0.0).astype(BF16)


def _stack_heads(q_ref, k):
    return jnp.concatenate([q_ref[:, (k * NSA_GROUP + g) * HEAD_DIM:(k * NSA_GROUP + g + 1) * HEAD_DIM]
                            for g in range(NSA_GROUP)], axis=0).astype(BF16)


def _pad_keys(x, n):
    return jnp.concatenate([x, jnp.zeros((n - x.shape[0], x.shape[1]), x.dtype)], axis=0)


def _nsa_combine(gt_ref, oc_ref, o_ref, l_ref, acc_ref, tq):
    gt = jax.nn.sigmoid(gt_ref[:, 0:V7X_LANES])
    for h in range(NSA_HEADS):
        cols = slice(h * HEAD_DIM, (h + 1) * HEAD_DIM)
        o_slc = _normalised(l_ref, acc_ref, slice(h * tq, (h + 1) * tq))
        o_win = _normalised(l_ref, acc_ref, slice((NSA_HEADS + h) * tq, (NSA_HEADS + h + 1) * tq))
        c0 = G_NSA + 3 * h
        o_ref[:, cols] = gt[:, c0:c0 + 1] * oc_ref[:, cols] + gt[:, c0 + 1:c0 + 2] * o_slc + gt[:, c0 + 2:c0 + 3] * o_win


_NSA_SLOPES = _alibi(NSA_HEADS)
_DIFF_SLOPES = _alibi(DIFF_HEADS)


def _nsa_state(tq):
    rows = 2 * NSA_HEADS * tq
    return [pltpu.VMEM((rows, 1), F32), pltpu.VMEM((rows, 1), F32), pltpu.VMEM((rows, HEAD_DIM), F32)]


def _diff_state(tq):
    rows = 2 * DIFF_HEADS * tq
    return [pltpu.VMEM((rows, 1), F32), pltpu.VMEM((rows, 1), F32), pltpu.VMEM((rows, HEAD_DIM), F32)]


def _cmp_accumulate(rows_of, w_ref, o_ref):
    for c in range(4):
        for r in range(CMP_STRIDE):
            part = jnp.dot(rows_of(c, r).astype(BF16), w_ref[c // 2, r * HEAD_DIM:(r + 1) * HEAD_DIM, :],
                           preferred_element_type=F32)
            if r == 0:
                o_ref[0, c] = part
            else:
                o_ref[0, c] += part


def _cmp_partial_body(x0_ref, x1_ref, x2_ref, x3_ref, w_ref, o_ref):
    srcs = (x0_ref, x1_ref, x2_ref, x3_ref)
    per = x0_ref.shape[0] // CMP_STRIDE
    _cmp_accumulate(lambda c, r: srcs[c][pl.ds(r, per, stride=CMP_STRIDE), :], w_ref, o_ref)


def _cmp_partial_paged_body(pt_ref, *refs, pps):
    del pt_ref
    pages, w_ref, o_ref, slab_ref = refs[:pps], refs[pps], refs[pps + 1], refs[pps + 2]
    for i, pg in enumerate(pages):
        for c in range(4):
            slab_ref[c, i * PAGE_SIZE:(i + 1) * PAGE_SIZE, :] = pg[0, 0, :, c * HEAD_DIM:(c + 1) * HEAD_DIM]
    per = pps * PAGE_SIZE // CMP_STRIDE
    _cmp_accumulate(lambda c, r: slab_ref[c, pl.ds(r, per, stride=CMP_STRIDE), :], w_ref, o_ref)


def _cmp_finish_body(*refs, has_tail, n_out):
    if has_tail:
        p_ref, tail_ref, w1_ref, pos_ref, w2_ref, g_ref, o_ref = refs
    else:
        p_ref, w1_ref, pos_ref, w2_ref, g_ref, o_ref = refs
    c = pl.program_id(1)
    p = p_ref[0, 0]
    n = p.shape[0]
    w1 = w1_ref[0]
    pb = jnp.dot(pos_ref[0].astype(BF16), w1, preferred_element_type=F32)
    nxt = pltpu.roll(p[:, HEAD_DIM:], n - 1, 0)
    if has_tail:
        rid = lax.broadcasted_iota(jnp.int32, (V7X_SUBLANES, HEAD_DIM), 0)
        tail8 = jnp.zeros((V7X_SUBLANES, HEAD_DIM), F32)
        xt = tail_ref[...].astype(BF16)
        for r in range(tail_ref.shape[0]):
            d = jnp.dot(xt, w1[r * HEAD_DIM:(r + 1) * HEAD_DIM, HEAD_DIM:], preferred_element_type=F32)
            tail8 = tail8 + jnp.where(rid == r, d, 0.0)
        tail = jnp.sum(tail8, axis=0, keepdims=True)
    else:
        tail = jnp.zeros((1, HEAD_DIM), F32)
    last = lax.broadcasted_iota(jnp.int32, (n, HEAD_DIM), 0) == n - 1
    z = p[:, :HEAD_DIM] + pb[0:1, :HEAD_DIM] + jnp.where(last, tail, nxt) + pb[1:2, HEAD_DIM:]
    y = jnp.dot((z * jax.nn.sigmoid(z)).astype(BF16), w2_ref[0].astype(BF16), preferred_element_type=F32)
    y = jnp.where(c < NSA_KV_HEADS, _rms_rows(y, g_ref[...]), y)
    o_ref[0, 0, 0:n] = y
    if n_out > n:
        o_ref[0, 0, n:n_out] = jnp.zeros((n_out - n, HEAD_DIM), F32)


def _cmp_weights(lp):
    half = CMP_STRIDE * HEAD_DIM
    w1 = lp['nsa_cmp_w1']
    w1ab = jnp.concatenate([w1[:, :half], w1[:, half:]], axis=2).astype(BF16)
    pos = lp['nsa_cmp_pos']
    pos2 = jnp.zeros((2, V7X_SUBLANES, half), F32)
    pos2 = pos2.at[:, 0].set(pos[:, :CMP_STRIDE].reshape(2, half)).at[:, 1].set(pos[:, CMP_STRIDE:].reshape(2, half))
    return w1ab, pos2


def _cmp_finish(p, tail_src, w1ab, pos2, lp, n_out, grp, u):
    b, _, n, _ = p.shape
    half = CMP_STRIDE * HEAD_DIM
    has_tail = tail_src is not None
    in_specs = [pl.BlockSpec((1, 1, n, 2 * HEAD_DIM), lambda i, c: (i, c, 0, 0))]
    args = [p]
    if has_tail:
        rb = grp.row0 // grp.t
        in_specs.append(pl.BlockSpec((grp.t, HEAD_DIM), lambda i, c: (rb + i, C_ROWS // HEAD_DIM + c)))
        args.append(u)
    in_specs += [
        pl.BlockSpec((1, half, 2 * HEAD_DIM), lambda i, c: (c // NSA_KV_HEADS, 0, 0)),
        pl.BlockSpec((1, V7X_SUBLANES, half), lambda i, c: (c // NSA_KV_HEADS, 0, 0)),
        pl.BlockSpec((1, HEAD_DIM, HEAD_DIM), lambda i, c: (c // NSA_KV_HEADS, 0, 0)),
        pl.BlockSpec((1, HEAD_DIM), lambda i, c: (0, 0)),
    ]
    args += [w1ab, pos2, lp['nsa_cmp_w2'], lp['nsa_k_norm'][0:1]]
    return pl.pallas_call(
        functools.partial(_cmp_finish_body, has_tail=has_tail, n_out=n_out),
        grid=(b, 4),
        in_specs=in_specs,
        out_specs=pl.BlockSpec((1, 1, n_out, HEAD_DIM), lambda i, c: (i, c, 0, 0)),
        out_shape=jax.ShapeDtypeStruct((b, 4, n_out, HEAD_DIM), F32),
        compiler_params=_cparams(("parallel", "parallel")),
        name="nsa_cmp_finish",
    )(*args)


def _compress_prompt(u, grp, lp):
    assert grp.past == 0 and grp.row0 == 0 and grp.t % (CMP_STRIDE * V7X_SUBLANES) == 0
    w1ab, pos2 = _cmp_weights(lp)
    n = grp.t // CMP_STRIDE
    group = lambda c: pl.BlockSpec((grp.t, HEAD_DIM), lambda i: (i, C_ROWS // HEAD_DIM + c))
    p = pl.pallas_call(
        _cmp_partial_body,
        grid=(grp.b,),
        in_specs=[group(0), group(1), group(2), group(3), pl.BlockSpec(w1ab.shape, lambda i: (0, 0, 0))],
        out_specs=pl.BlockSpec((1, 4, n, 2 * HEAD_DIM), lambda i: (i, 0, 0, 0)),
        out_shape=jax.ShapeDtypeStruct((grp.b, 4, n, 2 * HEAD_DIM), F32),
        compiler_params=_cparams(("parallel",)),
        name="nsa_cmp_partial",
    )(u, u, u, u, w1ab)
    return _cmp_finish(p, None, w1ab, pos2, lp, _round_up(n, V7X_LANES), grp, u), n - 1


def _pages_per_step(n_pages):
    return _pick_tile(n_pages, (16, 8, 4, 2, 1))


def _page_specs(layer, n_pages, pps, width, col_block):
    def spec(i):
        return pl.BlockSpec((1, 1, PAGE_SIZE, width),
                            lambda b, c, pt: (layer, pt[b * n_pages + c * pps + i], 0, col_block))
    return [spec(i) for i in range(pps)]


def _compress_sample(u, grp, lp, cache, page_table, layer):
    assert grp.past % PAGE_SIZE == 0 and grp.t < CMP_STRIDE
    w1ab, pos2 = _cmp_weights(lp)
    n_pages = grp.past // PAGE_SIZE
    pps = _pages_per_step(n_pages)
    per = pps * PAGE_SIZE // CMP_STRIDE
    n = grp.past // CMP_STRIDE
    p = pl.pallas_call(
        functools.partial(_cmp_partial_paged_body, pps=pps),
        grid_spec=pltpu.PrefetchScalarGridSpec(
            num_scalar_prefetch=1,
            grid=(grp.b, n_pages // pps),
            in_specs=_page_specs(layer, n_pages, pps, 4 * HEAD_DIM, 0) + [
                pl.BlockSpec(w1ab.shape, lambda b, c, pt: (0, 0, 0))],
            out_specs=pl.BlockSpec((1, 4, per, 2 * HEAD_DIM), lambda b, c, pt: (b, 0, c, 0)),
            scratch_shapes=[pltpu.VMEM((4, pps * PAGE_SIZE, HEAD_DIM), F32)],
        ),
        out_shape=jax.ShapeDtypeStruct((grp.b, 4, n, 2 * HEAD_DIM), F32),
        compiler_params=_cparams(("parallel", "arbitrary")),
        name="nsa_cmp_partial_paged",
    )(page_table, *([cache] * pps), w1ab)
    return _cmp_finish(p, u, w1ab, pos2, lp, _round_up(n + 1, V7X_LANES), grp, u), n


def _cmp_select_body(q_ref, kv_ref, o_ref, sel_ref, *, tq, q_off, n_cmp, n_slc, n_sel):
    qi = pl.program_id(1)
    n_cpad = kv_ref.shape[2]
    n_spad = sel_ref.shape[1] // NSA_KV_HEADS
    q_lo = q_off + qi * tq
    qpos = q_lo + lax.broadcasted_iota(jnp.int32, (tq, n_cpad), 0)
    ends = lax.broadcasted_iota(jnp.int32, (tq, n_cpad), 1) * CMP_STRIDE + (CMP_BLOCK - 1)
    d = qpos - ends
    mask = d >= 0
    df = d.astype(F32)
    c_lo = lax.broadcasted_iota(jnp.int32, (n_cpad, n_spad), 0) * CMP_STRIDE
    j_lo = lax.broadcasted_iota(jnp.int32, (n_cpad, n_spad), 1) * SLC_BLOCK
    cover = jnp.where(c_lo < j_lo + SLC_BLOCK, jnp.where(c_lo + CMP_BLOCK > j_lo, 1.0, 0.0), 0.0)
    cover = jnp.where(c_lo < n_cmp * CMP_STRIDE, cover, 0.0)
    qp = q_lo + lax.broadcasted_iota(jnp.int32, (tq, n_spad), 0)
    j = lax.broadcasted_iota(jnp.int32, (tq, n_spad), 1)
    cur = lax.shift_right_logical(qp, SLC_SHIFT)
    bonus = FORCE_BONUS * jnp.where(j == 0, 1.0, jnp.where(j == cur, 1.0, jnp.where(j == cur - 1, 1.0, 0.0)))
    valid = j * SLC_BLOCK <= qp
    for k in range(NSA_KV_HEADS):
        q = jnp.concatenate([q_ref[:, (k * NSA_GROUP + g) * HEAD_DIM:(k * NSA_GROUP + g + 1) * HEAD_DIM]
                             for g in range(NSA_GROUP)], axis=0)
        s = lax.dot_general(q, kv_ref[0, k], NT, precision=HIGHEST, preferred_element_type=F32)
        vc = kv_ref[0, NSA_KV_HEADS + k].astype(BF16)
        psum = jnp.zeros((tq, n_cpad), F32)
        for g in range(NSA_GROUP):
            h = k * NSA_GROUP + g
            sg = jnp.where(mask, s[g * tq:(g + 1) * tq] - _NSA_SLOPES[h] * df, NEG_INF)
            m = jnp.max(sg, axis=-1, keepdims=True)
            p = jnp.where(mask, jnp.exp(sg - m), 0.0)
            p = p / jnp.maximum(jnp.sum(p, axis=-1, keepdims=True), 1e-30)
            o_ref[:, h * HEAD_DIM:(h + 1) * HEAD_DIM] = jnp.dot(p.astype(BF16), vc, preferred_element_type=F32)
            psum = psum + p
        imp = jnp.dot(psum, cover, precision=HIGHEST, preferred_element_type=F32)
        score = jnp.where(valid, imp + bonus, NEG_INF)
        cnt = jnp.zeros((tq, n_spad), F32)
        for i in range(n_slc):
            ci = score[:, i:i + 1]
            cnt = cnt + jnp.where(ci > score, 1.0, jnp.where(ci == score, jnp.where(j > i, 1.0, 0.0), 0.0))
        sel_ref[:, k * n_spad:(k + 1) * n_spad] = jnp.where(cnt < n_sel, jnp.where(j < n_slc, 1.0, 0.0), 0.0)


def _cmp_select(u, grp, summaries, n_cmp):
    n_slc = -(-(grp.past + grp.t) // SLC_BLOCK)
    n_spad = _round_up(n_slc, V7X_LANES)
    n_cpad = summaries.shape[2]
    tq = _pick_tile(grp.t, (256, 128, 64, 32, 16, 8))
    nq = grp.t // tq
    rb = grp.row0 // tq
    body = functools.partial(_cmp_select_body, tq=tq, q_off=grp.past, n_cmp=n_cmp, n_slc=n_slc,
                             n_sel=min(SLC_TOPN, n_slc))
    return pl.pallas_call(
        body,
        grid=(grp.b, nq),
        in_specs=[
            pl.BlockSpec((tq, NSA_W), lambda b, i: (rb + b * nq + i, 0)),
            pl.BlockSpec((1, 4, n_cpad, HEAD_DIM), lambda b, i: (b, 0, 0, 0)),
        ],
        out_specs=[
            pl.BlockSpec((tq, NSA_W), lambda b, i: (b * nq + i, 0)),
            pl.BlockSpec((tq, NSA_KV_HEADS * n_spad), lambda b, i: (b * nq + i, 0)),
        ],
        out_shape=[
            jax.ShapeDtypeStruct((grp.b * grp.t, NSA_W), F32),
            jax.ShapeDtypeStruct((grp.b * grp.t, NSA_KV_HEADS * n_spad), F32),
        ],
        compiler_params=_cparams(("parallel", "parallel")),
        name="nsa_cmp_select",
    )(u, summaries)


def _nsa_prompt_body(q_ref, ks_ref, kw_ref, sel_ref, oc_ref, gt_ref, o_ref, m_ref, l_ref, acc_ref, *, tq, tk):
    qi = pl.program_id(1)
    kj = pl.program_id(2)
    n_spad = sel_ref.shape[1] // NSA_KV_HEADS

    @pl.when(kj == 0)
    def _():
        _init_state(m_ref, l_ref, acc_ref)

    q_lo = qi * tq
    k_lo = kj * tk
    slc_on = k_lo <= q_lo + (tq - 1)
    win_on = jnp.logical_and(slc_on, k_lo + (tk - 1) >= q_lo - WINDOW)

    @pl.when(slc_on)
    def _():
        d = _distance(tq, tk, q_lo - k_lo)
        df = d.astype(F32)
        expand = _block_expand(n_spad, tk, k_lo)
        kv = ks_ref[...].astype(BF16)
        for k in range(NSA_KV_HEADS):
            s = lax.dot_general(_stack_heads(q_ref, k), kv[:, k * HEAD_DIM:(k + 1) * HEAD_DIM], NT,
                                preferred_element_type=F32)
            chosen = jnp.dot(sel_ref[:, k * n_spad:(k + 1) * n_spad].astype(BF16), expand,
                             preferred_element_type=F32)
            keep = jnp.logical_and(d >= 0, chosen > 0.5)
            _online_update(s, keep, df, _NSA_SLOPES[k * NSA_GROUP:(k + 1) * NSA_GROUP],
                           kv[:, (NSA_KV_HEADS + k) * HEAD_DIM:(NSA_KV_HEADS + k + 1) * HEAD_DIM],
                           m_ref, l_ref, acc_ref, k * NSA_GROUP * tq, tq)

    @pl.when(win_on)
    def _():
        d = _distance(tq, tk, q_lo - k_lo)
        df = d.astype(F32)
        keep = jnp.logical_and(d >= 0, d <= WINDOW)
        kv = kw_ref[...].astype(BF16)
        for k in range(NSA_KV_HEADS):
            s = lax.dot_general(_stack_heads(q_ref, k), kv[:, k * HEAD_DIM:(k + 1) * HEAD_DIM], NT,
                                preferred_element_type=F32)
            _online_update(s, keep, df, _NSA_SLOPES[k * NSA_GROUP:(k + 1) * NSA_GROUP],
                           kv[:, (NSA_KV_HEADS + k) * HEAD_DIM:(NSA_KV_HEADS + k + 1) * HEAD_DIM],
                           m_ref, l_ref, acc_ref, (NSA_KV_HEADS + k) * NSA_GROUP * tq, tq)

    @pl.when(kj == pl.num_programs(2) - 1)
    def _():
        _nsa_combine(gt_ref, oc_ref, o_ref, l_ref, acc_ref, tq)


def _nsa_prompt(u, grp, o_cmp, sel):
    assert grp.past == 0 and grp.row0 == 0
    t = grp.t
    tq = _pick_tile(t, (256, 128, 64, 32, 16, 8))
    tk = _pick_tile(t, (512, 256, 128))
    nq, nk = t // tq, t // tk
    kvw = 2 * NSA_KV_W

    def last_tile(i):
        return (i * tq + (tq - 1)) // tk

    def slc_rows(b, i, j):
        return (b * nk + jnp.minimum(j, last_tile(i)), (C_ROWS + kvw) // kvw)

    def win_rows(b, i, j):
        first = jnp.maximum(i * tq - WINDOW, 0) // tk
        return (b * nk + jnp.clip(j, first, last_tile(i)), C_WIN // kvw)

    return pl.pallas_call(
        functools.partial(_nsa_prompt_body, tq=tq, tk=tk),
        grid=(grp.b, nq, nk),
        in_specs=[
            pl.BlockSpec((tq, NSA_W), lambda b, i, j: (b * nq + i, 0)),
            pl.BlockSpec((tk, kvw), slc_rows),
            pl.BlockSpec((tk, kvw), win_rows),
            pl.BlockSpec((tq, sel.shape[1]), lambda b, i, j: (b * nq + i, 0)),
            pl.BlockSpec((tq, NSA_W), lambda b, i, j: (b * nq + i, 0)),
            pl.BlockSpec((tq, U_TILE), lambda b, i, j: (b * nq + i, C_SMALL // U_TILE)),
        ],
        out_specs=pl.BlockSpec((tq, NSA_W), lambda b, i, j: (b * nq + i, 0)),
        out_shape=jax.ShapeDtypeStruct((grp.b * t, NSA_W), F32),
        scratch_shapes=_nsa_state(tq),
        compiler_params=_cparams(("parallel", "parallel", "arbitrary")),
        name="nsa_attn_prompt",
    )(u, u, u, sel, o_cmp, u)


def _nsa_sample_body(pt_ref, q_ref, *rest, pps, past, n_win):
    pages = rest[:pps]
    knew_ref, wst_ref, wnew_ref, sel_ref, oc_ref, gt_ref, o_ref, m_ref, l_ref, acc_ref = rest[pps:]
    del pt_ref
    c = pl.program_id(1)
    tq = q_ref.shape[0]
    n_spad = sel_ref.shape[1] // NSA_KV_HEADS
    n = pps * PAGE_SIZE

    @pl.when(c == 0)
    def _():
        _init_state(m_ref, l_ref, acc_ref)

    def attend(kv, keep, df, chosen_of, branch):
        for k in range(NSA_KV_HEADS):
            s = lax.dot_general(_stack_heads(q_ref, k), kv[:, k * HEAD_DIM:(k + 1) * HEAD_DIM], NT,
                                preferred_element_type=F32)
            keep_k = keep
            if chosen_of is not None:
                chosen = chosen_of(k)
                keep_k = chosen if keep is None else jnp.logical_and(keep, chosen)
            _online_update(s, keep_k, df, _NSA_SLOPES[k * NSA_GROUP:(k + 1) * NSA_GROUP],
                           kv[:, (NSA_KV_HEADS + k) * HEAD_DIM:(NSA_KV_HEADS + k + 1) * HEAD_DIM],
                           m_ref, l_ref, acc_ref, (branch * NSA_KV_HEADS + k) * NSA_GROUP * tq, tq)

    first_key = c * n
    kv = jnp.concatenate([pg[0, 0].astype(BF16) for pg in pages], axis=0)
    expand = _block_expand(n_spad, n, first_key)
    attend(kv, None, _distance(tq, n, past - first_key).astype(F32),
           lambda k: jnp.dot(sel_ref[:, k * n_spad:(k + 1) * n_spad].astype(BF16), expand,
                             preferred_element_type=F32) > 0.5, 0)

    @pl.when(c == pl.num_programs(1) - 1)
    def _():
        npad = V7X_LANES
        d_new = _distance(tq, npad, 0)
        real = lax.broadcasted_iota(jnp.int32, (tq, npad), 1) < tq
        causal = jnp.logical_and(d_new >= 0, real)
        lane = lax.broadcasted_iota(jnp.int32, (tq, n_spad), 1)

        def chosen_new(k):
            col = jnp.where(lane == past // SLC_BLOCK, sel_ref[:, k * n_spad:(k + 1) * n_spad], 0.0)
            return jnp.sum(col, axis=-1, keepdims=True) > 0.5

        attend(_pad_keys(knew_ref[...], npad).astype(BF16), causal, d_new.astype(F32), chosen_new, 0)
        d_buf = _distance(tq, n_win, n_win)
        attend(wst_ref[0, 0].astype(BF16), d_buf <= WINDOW, d_buf.astype(F32), None, 1)
        attend(_pad_keys(wnew_ref[...], npad).astype(BF16), causal, d_new.astype(F32), None, 1)
        _nsa_combine(gt_ref, oc_ref, o_ref, l_ref, acc_ref, tq)


def _nsa_sample(u, grp, o_cmp, sel, cache, win_state, page_table, layer):
    t = grp.t
    n_win = win_state.shape[2]
    assert n_win == WINDOW and grp.past % SLC_BLOCK == 0 and t <= SLC_BLOCK and grp.row0 % t == 0
    n_pages = grp.past // PAGE_SIZE
    pps = _pages_per_step(n_pages)
    kvw = 2 * NSA_KV_W
    rb = grp.row0 // t
    tok = lambda width, col: pl.BlockSpec((t, width), lambda b, c, pt: (rb + b, col))
    own = lambda width: pl.BlockSpec((t, width), lambda b, c, pt: (b, 0))
    return pl.pallas_call(
        functools.partial(_nsa_sample_body, pps=pps, past=grp.past, n_win=n_win),
        grid_spec=pltpu.PrefetchScalarGridSpec(
            num_scalar_prefetch=1,
            grid=(grp.b, n_pages // pps),
            in_specs=[tok(NSA_W, 0)] + _page_specs(layer, n_pages, pps, kvw, 1) + [
                tok(kvw, (C_ROWS + kvw) // kvw),
                pl.BlockSpec((1, 1, n_win, kvw), lambda b, c, pt: (layer, b, 0, 0)),
                tok(kvw, C_WIN // kvw),
                own(sel.shape[1]),
                own(NSA_W),
                tok(U_TILE, C_SMALL // U_TILE),
            ],
            out_specs=own(NSA_W),
            scratch_shapes=_nsa_state(t),
        ),
        out_shape=jax.ShapeDtypeStruct((grp.b * t, NSA_W), F32),
        compiler_params=_cparams(("parallel", "arbitrary")),
        name="nsa_attn_paged",
    )(page_table, u, *([cache] * pps), u, win_state, u, sel, o_cmp, u)


def _diff_update(q_ref, kv, keep, df, m_ref, l_ref, acc_ref):
    tq = q_ref.shape[0]
    low = lax.broadcasted_iota(jnp.int32, (tq, HEAD_DIM), 1) < DIFF_QK_DIM
    for h in range(DIFF_HEADS):
        cols = slice(h * HEAD_DIM, (h + 1) * HEAD_DIM)
        qh = q_ref[:, cols]
        q2 = jnp.concatenate([jnp.where(low, qh, 0.0), jnp.where(low, 0.0, qh)], axis=0).astype(BF16)
        s = lax.dot_general(q2, kv[:, cols], NT, preferred_element_type=F32)
        _online_update(s, keep, df, (_DIFF_SLOPES[h],) * 2, kv[:, DIFF_W + h * HEAD_DIM:DIFF_W + (h + 1) * HEAD_DIM],
                       m_ref, l_ref, acc_ref, 2 * h * tq, tq)


def _diff_finish(lam_ref, g_ref, o_ref, l_ref, acc_ref, post):
    tq = o_ref.shape[0]
    lam = lam_ref[0]
    for h in range(DIFF_HEADS):
        o1 = _normalised(l_ref, acc_ref, slice(2 * h * tq, (2 * h + 1) * tq))
        o2 = _normalised(l_ref, acc_ref, slice((2 * h + 1) * tq, (2 * h + 2) * tq))
        o_ref[:, h * HEAD_DIM:(h + 1) * HEAD_DIM] = _rms_rows(o1 - lam * o2, g_ref[...]) * post


def _diff_prompt_body(lam_ref, q_ref, kv_ref, g_ref, o_ref, m_ref, l_ref, acc_ref, *, tq, tk, post):
    qi = pl.program_id(1)
    kj = pl.program_id(2)

    @pl.when(kj == 0)
    def _():
        _init_state(m_ref, l_ref, acc_ref)

    @pl.when(kj * tk <= qi * tq + (tq - 1))
    def _():
        d = _distance(tq, tk, qi * tq - kj * tk)
        _diff_update(q_ref, kv_ref[...].astype(BF16), d >= 0, d.astype(F32), m_ref, l_ref, acc_ref)

    @pl.when(kj == pl.num_programs(2) - 1)
    def _():
        _diff_finish(lam_ref, g_ref, o_ref, l_ref, acc_ref, post)


def _diff_prompt(u, grp, lam, gain, post):
    assert grp.past == 0 and grp.row0 == 0
    t = grp.t
    tq = _pick_tile(t, (256, 128, 64, 32, 16, 8))
    tk = _pick_tile(t, (512, 256, 128))
    nq, nk = t // tq, t // tk
    return pl.pallas_call(
        functools.partial(_diff_prompt_body, tq=tq, tk=tk, post=post),
        grid=(grp.b, nq, nk),
        in_specs=[
            pl.BlockSpec(memory_space=pltpu.SMEM),
            pl.BlockSpec((tq, DIFF_W), lambda b, i, j: (b * nq + i, C_DQ // DIFF_W)),
            pl.BlockSpec((tk, 2 * DIFF_W),
                         lambda b, i, j: (b * nk + jnp.minimum(j, (i * tq + (tq - 1)) // tk), C_DROWS // (2 * DIFF_W))),
            pl.BlockSpec((1, HEAD_DIM), lambda b, i, j: (0, 0)),
        ],
        out_specs=pl.BlockSpec((tq, DIFF_W), lambda b, i, j: (b * nq + i, 0)),
        out_shape=jax.ShapeDtypeStruct((grp.b * t, DIFF_W), F32),
        scratch_shapes=_diff_state(tq),
        compiler_params=_cparams(("parallel", "parallel", "arbitrary")),
        name="diff_attn_prompt",
    )(lam, u, u, gain)


def _diff_sample_body(pt_ref, lam_ref, q_ref, *rest, pps, past, post):
    pages = rest[:pps]
    new_ref, g_ref, o_ref, m_ref, l_ref, acc_ref = rest[pps:]
    del pt_ref
    c = pl.program_id(1)
    tq = q_ref.shape[0]
    n = pps * PAGE_SIZE

    @pl.when(c == 0)
    def _():
        _init_state(m_ref, l_ref, acc_ref)

    kv = jnp.concatenate([pg[0, 0].astype(BF16) for pg in pages], axis=0)
    _diff_update(q_ref, kv, None, _distance(tq, n, past - c * n).astype(F32), m_ref, l_ref, acc_ref)

    @pl.when(c == pl.num_programs(1) - 1)
    def _():
        npad = V7X_LANES
        d = _distance(tq, npad, 0)
        keep = jnp.logical_and(d >= 0, lax.broadcasted_iota(jnp.int32, (tq, npad), 1) < tq)
        _diff_update(q_ref, _pad_keys(new_ref[...], npad).astype(BF16), keep, d.astype(F32), m_ref, l_ref, acc_ref)
        _diff_finish(lam_ref, g_ref, o_ref, l_ref, acc_ref, post)


def _diff_sample(u, grp, lam, gain, post, cache, page_table, layer):
    t = grp.t
    assert grp.past % PAGE_SIZE == 0 and grp.row0 % t == 0
    n_pages = grp.past // PAGE_SIZE
    pps = _pages_per_step(n_pages)
    rb = grp.row0 // t
    return pl.pallas_call(
        functools.partial(_diff_sample_body, pps=pps, past=grp.past, post=post),
        grid_spec=pltpu.PrefetchScalarGridSpec(
            num_scalar_prefetch=1,
            grid=(grp.b, n_pages // pps),
            in_specs=[
                pl.BlockSpec(memory_space=pltpu.SMEM),
                pl.BlockSpec((t, DIFF_W), lambda b, c, pt: (rb + b, C_DQ // DIFF_W)),
            ] + _page_specs(layer, n_pages, pps, 2 * DIFF_W, 0) + [
                pl.BlockSpec((t, 2 * DIFF_W), lambda b, c, pt: (rb + b, C_DROWS // (2 * DIFF_W))),
                pl.BlockSpec((1, HEAD_DIM), lambda b, c, pt: (0, 0)),
            ],
            out_specs=pl.BlockSpec((t, DIFF_W), lambda b, c, pt: (b, 0)),
            scratch_shapes=_diff_state(t),
        ),
        out_shape=jax.ShapeDtypeStruct((grp.b * t, DIFF_W), F32),
        compiler_params=_cparams(("parallel", "arbitrary")),
        name="diff_attn_paged",
    )(page_table, lam, u, *([cache] * pps), u, gain)


def _mlstm_body(xq_ref, xk_ref, v_ref, og_ref, bq_ref, bk_ref, wq_ref, wk_ref, cq_ref, ck_ref, li_ref, lf_ref,
                c0_ref, n0_ref, m0_ref, g_ref, h_ref, c_ref, n_ref, m_ref, q_s, k_s, *, lc, nc):
    t = xq_ref.shape[0]
    row = lax.broadcasted_iota(jnp.int32, (lc, lc), 0)
    col = lax.broadcasted_iota(jnp.int32, (lc, lc), 1)
    tril = row >= col
    triu = row <= col
    eye = row == col
    gain = g_ref[...]

    def conv(x_ref, buf_ref, w_ref, b_ref):
        ext = jnp.concatenate([buf_ref[0], x_ref[...]], axis=0)
        y = b_ref[...]
        for i in range(CONV_WIDTH):
            shifted = ext if i == CONV_WIDTH - 1 else pltpu.roll(ext, CONV_WIDTH - 1 - i, 0)
            y = y + shifted[V7X_SUBLANES:V7X_SUBLANES + t] * w_ref[i:i + 1, :]
        return y * jax.nn.sigmoid(y)

    q_s[...] = conv(xq_ref, bq_ref, wq_ref, cq_ref) * MLSTM_DIM ** -0.5
    k_s[...] = conv(xk_ref, bk_ref, wk_ref, ck_ref)

    def to_col(r):
        return jnp.sum(jnp.where(eye, jnp.broadcast_to(r, (lc, lc)), 0.0), axis=1, keepdims=True)

    def step(ci, carry):
        c, n, m = carry
        rows = pl.ds(ci * lc if nc == 1 else pl.multiple_of(ci * lc, lc), lc)
        qc = q_s[rows, :]
        kc = k_s[rows, :]
        vc = v_ref[rows, :]
        li = li_ref[0, pl.ds(ci, 1), :]
        lf = lf_ref[0, pl.ds(ci, 1), :]
        lf_b = jnp.broadcast_to(lf, (lc, lc))
        bcum_c = jnp.sum(jnp.where(tril, lf_b, 0.0), axis=1, keepdims=True)
        bcum_r = jnp.sum(jnp.where(triu, jnp.broadcast_to(to_col(lf), (lc, lc)), 0.0), axis=0, keepdims=True)
        dmat = jnp.where(tril, bcum_c - bcum_r + li, NEG_INF)
        inter = bcum_c + m
        mj = jnp.maximum(inter, jnp.max(dmat, axis=1, keepdims=True))
        wts = jnp.exp(dmat - mj)
        a = jnp.exp(inter - mj)
        sqk = lax.dot_general(qc, kc, NT, precision=HIGHEST, preferred_element_type=F32) * wts
        num = a * jnp.dot(qc, c, precision=HIGHEST, preferred_element_type=F32) + \
            jnp.dot(sqk, vc, precision=HIGHEST, preferred_element_type=F32)
        den = a * jnp.sum(qc * n, axis=1, keepdims=True) + jnp.sum(sqk, axis=1, keepdims=True)
        h = num / jnp.maximum(jnp.abs(den), jnp.exp(-mj))
        h_ref[rows, :] = _rms_rows(h, gain) * jax.nn.sigmoid(og_ref[rows, :])
        b_last = jnp.sum(lf, axis=1, keepdims=True)
        gl = b_last - bcum_r + li
        m_new = jnp.maximum(b_last + m, jnp.max(gl, axis=1, keepdims=True))
        decay = jnp.exp(b_last + m - m_new)
        kw = kc * to_col(jnp.exp(gl - m_new))
        c_new = decay * c + lax.dot_general(kw, vc, (((0,), (0,)), ((), ())), precision=HIGHEST,
                                            preferred_element_type=F32)
        n_new = decay * n + jnp.sum(kw, axis=0, keepdims=True)
        return c_new, n_new, m_new

    init = (c0_ref[0], n0_ref[0], m0_ref[0][:, 0:1])
    c, n, m = step(0, init) if nc == 1 else lax.fori_loop(0, nc, step, init)
    c_ref[0] = c
    n_ref[0] = n
    m_ref[0] = jnp.broadcast_to(m, (1, MLSTM_DIM))


def _mlstm(u, grp, lp, c0, n0, m0, conv_buf):
    b, t = grp.b, grp.t
    assert grp.row0 % t == 0 and t % V7X_SUBLANES == 0
    bh = b * MLSTM_HEADS
    hd = MLSTM_DIM
    lc = MLSTM_CHUNK if t % MLSTM_CHUNK == 0 else t
    nc = t // lc
    rb = grp.row0 // t
    gates = u[grp.row0:grp.row0 + b * t, C_SMALL:C_SMALL + U_TILE].reshape(b, t, U_TILE)
    log_i = gates[:, :, G_MI:G_MI + MLSTM_HEADS] + lp['ml_gate_b'][0]
    log_f = jax.nn.log_sigmoid(gates[:, :, G_MF:G_MF + MLSTM_HEADS] + lp['ml_gate_b'][1])
    log_i = log_i.transpose(0, 2, 1).reshape(bh, nc, lc)
    log_f = log_f.transpose(0, 2, 1).reshape(bh, nc, lc)
    buf8 = jnp.pad(conv_buf, ((0, 0), (V7X_SUBLANES - (CONV_WIDTH - 1), 0), (0, 0)))
    w8 = jnp.pad(lp['ml_conv_w'], ((0, V7X_SUBLANES - CONV_WIDTH), (0, 0)))
    cb = lp['ml_conv_b'].reshape(1, 2 * MLSTM_W)
    tok = lambda col0: pl.BlockSpec((t, hd), lambda i, h: (rb + i, col0 // hd + h))
    buf = lambda off: pl.BlockSpec((1, V7X_SUBLANES, hd), lambda i, h: (i, 0, off + h))
    wsp = lambda off: pl.BlockSpec((V7X_SUBLANES, hd), lambda i, h: (0, off + h))
    bsp = lambda off: pl.BlockSpec((1, hd), lambda i, h: (0, off + h))
    gate = pl.BlockSpec((1, nc, lc), lambda i, h: (i * MLSTM_HEADS + h, 0, 0))
    vec = pl.BlockSpec((1, 1, hd), lambda i, h: (i * MLSTM_HEADS + h, 0, 0))
    mat = pl.BlockSpec((1, hd, hd), lambda i, h: (i * MLSTM_HEADS + h, 0, 0))
    h, c, n, m = pl.pallas_call(
        functools.partial(_mlstm_body, lc=lc, nc=nc),
        grid=(b, MLSTM_HEADS),
        in_specs=[tok(C_MQ), tok(C_MK), tok(C_MV), tok(C_MO), buf(0), buf(MLSTM_HEADS), wsp(0), wsp(MLSTM_HEADS),
                  bsp(0), bsp(MLSTM_HEADS), gate, gate, mat, vec, vec, pl.BlockSpec((1, hd), lambda i, h: (0, 0))],
        out_specs=[pl.BlockSpec((t, hd), lambda i, h: (i, h)), mat, vec, vec],
        out_shape=[
            jax.ShapeDtypeStruct((b * t, MLSTM_W), F32),
            jax.ShapeDtypeStruct((bh, hd, hd), F32),
            jax.ShapeDtypeStruct((bh, 1, hd), F32),
            jax.ShapeDtypeStruct((bh, 1, hd), F32),
        ],
        scratch_shapes=[pltpu.VMEM((t, hd), F32), pltpu.VMEM((t, hd), F32)],
        compiler_params=_cparams(("parallel", "parallel")),
        name="mlstm",
    )(u, u, u, u, buf8, buf8, w8, w8, cb, cb, log_i, log_f, c0.reshape(bh, hd, hd), n0.reshape(bh, 1, hd),
      jnp.broadcast_to(m0.reshape(bh, 1, 1), (bh, 1, hd)), lp['ml_out_norm'].reshape(1, hd))
    return (h, c.reshape(b, MLSTM_HEADS, hd, hd), n.reshape(b, MLSTM_HEADS, hd), m[:, 0, 0].reshape(b, MLSTM_HEADS))


def _pack_w_in(w_in):
    cuts = np.cumsum((0,) + IN_SPLITS)
    cols = [w_in[..., cuts[i]:cuts[i + 1]] for i in range(len(IN_SPLITS))]
    packed = jnp.concatenate([cols[i] for i in _MAIN_ORDER] + [cols[i] for i in _SMALL_ORDER], axis=-1)
    pad = [(0, 0)] * (packed.ndim - 1) + [(0, U_COLS - packed.shape[-1])]
    return jnp.pad(packed, pad).astype(BF16)


def _u_gain(lp):
    g = jnp.ones((U_COLS,), F32)
    g = g.at[C_NQ:C_NQ + NSA_W].set(jnp.tile(lp['nsa_q_norm'], NSA_HEADS) * HEAD_DIM ** -0.5)
    g = g.at[C_ROWS + 2 * NSA_KV_W:C_ROWS + 3 * NSA_KV_W].set(jnp.tile(lp['nsa_k_norm'][1], NSA_KV_HEADS))
    g = g.at[C_WIN:C_WIN + NSA_KV_W].set(jnp.tile(lp['nsa_k_norm'][2], NSA_KV_HEADS))
    g = g.at[C_DQ:C_DQ + DIFF_W].set(jnp.tile(lp['diff_q_norm'], 2 * DIFF_HEADS) * DIFF_QK_DIM ** -0.5)
    g = g.at[C_DROWS:C_DROWS + DIFF_W].set(jnp.tile(lp['diff_k_norm'], 2 * DIFF_HEADS))
    return g.reshape(1, U_COLS)


def _group_rows(u, grp, col0, width):
    return u[grp.row0:grp.row0 + grp.b * grp.t, col0:col0 + width].reshape(grp.b, grp.t, width)


def _tail_rows(buf, new, keep):
    n_new = min(keep, new.shape[1])
    parts = [buf[:, buf.shape[1] - (keep - n_new):], new[:, new.shape[1] - n_new:]] if keep > n_new else \
        [new[:, new.shape[1] - n_new:]]
    return jnp.concatenate(parts, axis=1)


def _trunk(x_prompt, x_sample, cache_nsa, cache_diff, state_win, state_ml_c, state_ml_n, state_ml_m, state_ml_conv,
           page_table, p):
    bp, tp, d = x_prompt.shape
    bs, ts, _ = x_sample.shape
    depth = p['w_in'].shape[0]
    past = page_table.shape[1] * PAGE_SIZE
    gp = _Group(bp, tp, 0, 0)
    gs = _Group(bs, ts, bp * tp, past)
    n_p = bp * tp
    w_in = _pack_w_in(p['w_in'])
    w_out = p['w_out'].astype(BF16)
    w13 = (p['ffn1_w13'].astype(BF16), p['ffn2_w13'].astype(BF16))
    w2 = (p['ffn1_w2'].astype(BF16), p['ffn2_w2'].astype(BF16))
    per_layer = ('mix_norm', 'nsa_q_norm', 'nsa_k_norm', 'nsa_cmp_pos', 'nsa_cmp_w1', 'nsa_cmp_w2', 'diff_q_norm',
                 'diff_k_norm', 'diff_lambda', 'diff_out_norm', 'ml_conv_w', 'ml_conv_b', 'ml_gate_b', 'ml_out_norm')
    cache_nsa = cache_nsa.reshape(cache_nsa.shape[:3] + (4 * NSA_KV_W,))
    cache_diff = cache_diff.reshape(cache_diff.shape[:3] + (2 * DIFF_W,))
    win_state = state_win.reshape(state_win.shape[:3] + (2 * NSA_KV_W,))
    pt = page_table.reshape(-1)
    zeros = lambda *s: jnp.zeros(s, F32)
    p_state = (zeros(bp, MLSTM_HEADS, MLSTM_DIM, MLSTM_DIM), zeros(bp, MLSTM_HEADS, MLSTM_DIM),
               zeros(bp, MLSTM_HEADS), zeros(bp, CONV_WIDTH - 1, 2 * MLSTM_W))
    p_win0 = zeros(bp, 0, 2 * NSA_KV_W)

    x = jnp.concatenate([x_prompt.reshape(n_p, d), x_sample.reshape(bs * ts, d)], axis=0)
    outs = []
    for l in range(depth):
        lp = {name: p[name][l] for name in per_layer}
        lam_init = 0.8 - 0.6 * math.exp(-0.3 * l)
        lam_p = lp['diff_lambda']
        lam = (jnp.exp(jnp.sum(lam_p[0] * lam_p[1])) - jnp.exp(jnp.sum(lam_p[2] * lam_p[3])) + lam_init).reshape(1)
        dgain = lp['diff_out_norm'].reshape(1, HEAD_DIM)
        h = _ffn(x, p['ffn1_norm'][l], w13[0][l], w2[0][l])
        u = _inproj(h, lp['mix_norm'], w_in[l], _u_gain(lp))

        summ_p, n_cmp_p = _compress_prompt(u, gp, lp)
        oc_p, sel_p = _cmp_select(u, gp, summ_p, n_cmp_p)
        nsa_p = _nsa_prompt(u, gp, oc_p, sel_p)
        diff_p = _diff_prompt(u, gp, lam, dgain, 1.0 - lam_init)
        ml_p, c_p, nn_p, m_p = _mlstm(u, gp, lp, *p_state)

        summ_s, n_cmp_s = _compress_sample(u, gs, lp, cache_nsa, pt, l)
        oc_s, sel_s = _cmp_select(u, gs, summ_s, n_cmp_s)
        nsa_s = _nsa_sample(u, gs, oc_s, sel_s, cache_nsa, win_state, pt, l)
        diff_s = _diff_sample(u, gs, lam, dgain, 1.0 - lam_init, cache_diff, pt, l)
        ml_s, c_s, nn_s, m_s = _mlstm(u, gs, lp, state_ml_c[l], state_ml_n[l], state_ml_m[l], state_ml_conv[l])

        h = _outproj(h, jnp.concatenate([nsa_p, nsa_s]), jnp.concatenate([diff_p, diff_s]),
                     jnp.concatenate([ml_p, ml_s]), w_out[l])
        x = _ffn(h, p['ffn2_norm'][l], w13[1][l], w2[1][l])

        def rows(grp, col0, shape):
            return _group_rows(u, grp, col0, int(np.prod(shape))).reshape((grp.b, grp.t) + shape)

        nsa_shape, diff_shape = (4, NSA_KV_HEADS, HEAD_DIM), (2, DIFF_HEADS, HEAD_DIM)
        win_shape = (2, NSA_KV_HEADS, HEAD_DIM)
        win_rows_p = _tail_rows(p_win0, _group_rows(u, gp, C_WIN, 2 * NSA_KV_W), min(WINDOW, tp))
        win_rows_s = _tail_rows(win_state[l], _group_rows(u, gs, C_WIN, 2 * NSA_KV_W),
                                min(WINDOW, win_state.shape[2] + ts))
        conv_p = _tail_rows(p_state[3], _group_rows(u, gp, C_MQ, 2 * MLSTM_W), CONV_WIDTH - 1)
        conv_s = _tail_rows(state_ml_conv[l], _group_rows(u, gs, C_MQ, 2 * MLSTM_W), CONV_WIDTH - 1)
        outs.append((rows(gp, C_ROWS, nsa_shape), rows(gs, C_ROWS, nsa_shape),
                     rows(gp, C_DROWS, diff_shape), rows(gs, C_DROWS, diff_shape),
                     win_rows_p.reshape(win_rows_p.shape[:2] + win_shape),
                     win_rows_s.reshape(win_rows_s.shape[:2] + win_shape),
                     c_p, c_s, nn_p, nn_s, m_p, m_s, conv_p, conv_s))
    stacked = [jnp.stack(z) for z in zip(*outs)]
    return (x[:n_p].reshape(bp, tp, d), x[n_p:].reshape(bs, ts, d)) + tuple(stacked)


def kernel(x_prompt, x_sample, cache_nsa, cache_diff, state_win, state_ml_c, state_ml_n, state_ml_m, state_ml_conv,
           page_table, ffn1_norm, ffn1_w13, ffn1_w2, mix_norm, w_in, nsa_q_norm, nsa_k_norm, nsa_cmp_pos, nsa_cmp_w1,
           nsa_cmp_w2, diff_q_norm, diff_k_norm, diff_lambda, diff_out_norm, ml_conv_w, ml_conv_b, ml_gate_b,
           ml_out_norm, w_out, ffn2_norm, ffn2_w13, ffn2_w2):
    params = dict(ffn1_norm=ffn1_norm, ffn1_w13=ffn1_w13, ffn1_w2=ffn1_w2, mix_norm=mix_norm, w_in=w_in,
                  nsa_q_norm=nsa_q_norm, nsa_k_norm=nsa_k_norm, nsa_cmp_pos=nsa_cmp_pos, nsa_cmp_w1=nsa_cmp_w1,
                  nsa_cmp_w2=nsa_cmp_w2, diff_q_norm=diff_q_norm, diff_k_norm=diff_k_norm, diff_lambda=diff_lambda,
                  diff_out_norm=diff_out_norm, ml_conv_w=ml_conv_w, ml_conv_b=ml_conv_b, ml_gate_b=ml_gate_b,
                  ml_out_norm=ml_out_norm, w_out=w_out, ffn2_norm=ffn2_norm, ffn2_w13=ffn2_w13, ffn2_w2=ffn2_w2)
    return _trunk(x_prompt, x_sample, cache_nsa, cache_diff, state_win, state_ml_c, state_ml_n, state_ml_m,
                  state_ml_conv, page_table, params)
```

```python
import functools
import math
from typing import NamedTuple

import numpy as np
import jax
import jax.numpy as jnp
from jax import lax
from jax.experimental import pallas as pl
from jax.experimental.pallas import tpu as pltpu

F32 = jnp.float32
BF16 = jnp.bfloat16
HIGHEST = lax.Precision.HIGHEST
NT = (((1,), (1,)), ((), ()))

D_MODEL = 2048
D_FF = 5632
PAGE_SIZE = 128
HEAD_DIM = 128
ROW_GROUPS = 8
NSA_HEADS = 8
NSA_KV_HEADS = 2
NSA_GROUP = NSA_HEADS // NSA_KV_HEADS
CMP_BLOCK = 32
CMP_STRIDE = 16
SLC_BLOCK = 64
SLC_SHIFT = 6
SLC_TOPN = 16
WINDOW = 512
DIFF_HEADS = 4
DIFF_QK_DIM = HEAD_DIM // 2
MLSTM_HEADS = 4
MLSTM_DIM = 128
MLSTM_CHUNK = 64
CONV_WIDTH = 4
NORM_EPS = 1e-6
NEG_INF = -1e30
M_INIT = -1e29
LOG2E = 1.4426950408889634
FORCE_BONUS = 1e3

NSA_W = NSA_HEADS * HEAD_DIM
NSA_KV_W = NSA_KV_HEADS * HEAD_DIM
DIFF_W = DIFF_HEADS * HEAD_DIM
MLSTM_W = MLSTM_HEADS * MLSTM_DIM
IN_SPLITS = (NSA_W, 6 * NSA_KV_W, 3 * NSA_HEADS, DIFF_W, DIFF_W, DIFF_W,
             MLSTM_W, MLSTM_W, MLSTM_W, MLSTM_HEADS, MLSTM_HEADS, MLSTM_W)
IN_COLS = sum(IN_SPLITS)

V7X_LANES = 128
V7X_SUBLANES = 8
V7X_VMEM_LIMIT_BYTES = 56 * 1024 * 1024

U_TILE = 256
C_NQ = 0
C_ROWS = C_NQ + NSA_W
C_WIN = C_ROWS + 4 * NSA_KV_W
C_DQ = C_WIN + 2 * NSA_KV_W
C_DROWS = C_DQ + DIFF_W
C_MQ = C_DROWS + 2 * DIFF_W
C_MK = C_MQ + MLSTM_W
C_MV = C_MK + MLSTM_W
C_MO = C_MV + MLSTM_W
C_SMALL = C_MO + MLSTM_W
U_COLS = C_SMALL + U_TILE
G_NSA, G_MI, G_MF = 0, 3 * NSA_HEADS, 3 * NSA_HEADS + MLSTM_HEADS
U_KINDS = (1, 1, 1, 1, 0, 0, 1, 0, 1, 0, 2, 2, 2, 2, 0, 0, 0, 0, 0, 0, 0, 0, 0, 0, 0)
_MAIN_ORDER = (0, 1, 3, 4, 5, 6, 7, 8, 11)
_SMALL_ORDER = (2, 9, 10)


class _Group(NamedTuple):
    b: int
    t: int
    row0: int
    past: int


def _round_up(n, m):
    return -(-n // m) * m


def _cparams(semantics):
    return pltpu.CompilerParams(dimension_semantics=semantics, vmem_limit_bytes=V7X_VMEM_LIMIT_BYTES)


def _pick_tile(n, prefs):
    for p in prefs:
        if n % p == 0:
            return p
    return n


def _alibi(n):
    return [float(2.0 ** (-8.0 * i / n)) for i in range(1, n + 1)]


def _rms_rows(x, g):
    inv = lax.rsqrt(jnp.mean(x * x, axis=-1, keepdims=True) + NORM_EPS)
    return x * inv * g


def _ffn_body(x_ref, g_ref, wa_ref, wb_ref, w2_ref, o_ref, n_ref):
    @pl.when(pl.program_id(1) == 0)
    def _():
        x = x_ref[...]
        n_ref[...] = _rms_rows(x, g_ref[...]).astype(BF16)
        o_ref[...] = x

    n = n_ref[...]
    a = jnp.dot(n, wa_ref[...], preferred_element_type=F32)
    b = jnp.dot(n, wb_ref[...], preferred_element_type=F32)
    act = (0.5 * a * jax.nn.sigmoid(a) * b).astype(BF16)
    o_ref[...] += jnp.dot(act, w2_ref[...], preferred_element_type=F32)


def _ffn(x, g, w13, w2):
    t, d = x.shape
    f = w2.shape[0]
    tm = _pick_tile(t, (768, 512, 256, 128, 64, 32, 16, 8))
    tf = _pick_tile(f, (512, 256, 128))
    nf = f // tf
    return pl.pallas_call(
        _ffn_body,
        grid=(t // tm, nf),
        in_specs=[
            pl.BlockSpec((tm, d), lambda i, j: (i, 0)),
            pl.BlockSpec((1, d), lambda i, j: (0, 0)),
            pl.BlockSpec((d, tf), lambda i, j: (0, j)),
            pl.BlockSpec((d, tf), lambda i, j: (0, j + nf)),
            pl.BlockSpec((tf, d), lambda i, j: (j, 0)),
        ],
        out_specs=pl.BlockSpec((tm, d), lambda i, j: (i, 0)),
        out_shape=jax.ShapeDtypeStruct((t, d), F32),
        scratch_shapes=[pltpu.VMEM((tm, d), BF16)],
        compiler_params=_cparams(("parallel", "arbitrary")),
        name="ffn",
    )(x, g.reshape(1, d), w13, w13, w2)


def _inproj_body(kind_ref, x_ref, g_ref, w_ref, gain_ref, o_ref, n_ref):
    j = pl.program_id(1)

    @pl.when(j == 0)
    def _():
        n_ref[...] = _rms_rows(x_ref[...], g_ref[...]).astype(BF16)

    y = jnp.dot(n_ref[...], w_ref[...], preferred_element_type=F32)
    kind = kind_ref[j]
    gain = gain_ref[...]
    halves = [slice(h * HEAD_DIM, (h + 1) * HEAD_DIM) for h in range(U_TILE // HEAD_DIM)]

    @pl.when(kind == 0)
    def _():
        o_ref[...] = y

    @pl.when(kind == 1)
    def _():
        for cols in halves:
            o_ref[:, cols] = _rms_rows(y[:, cols], gain[:, cols])

    @pl.when(kind == 2)
    def _():
        low = lax.broadcasted_iota(jnp.int32, (y.shape[0], HEAD_DIM), 1) < DIFF_QK_DIM
        for cols in halves:
            yh = y[:, cols]
            sq = yh * yh
            s_lo = jnp.sum(jnp.where(low, sq, 0.0), axis=-1, keepdims=True)
            s_hi = jnp.sum(jnp.where(low, 0.0, sq), axis=-1, keepdims=True)
            inv = lax.rsqrt(jnp.where(low, s_lo, s_hi) * (1.0 / DIFF_QK_DIM) + NORM_EPS)
            o_ref[:, cols] = yh * inv * gain[:, cols]


def _inproj(x, g, w, gain):
    t, d = x.shape
    tm = _pick_tile(t, (768, 512, 256, 128, 64, 32, 16, 8))
    return pl.pallas_call(
        _inproj_body,
        grid=(t // tm, U_COLS // U_TILE),
        in_specs=[
            pl.BlockSpec(memory_space=pltpu.SMEM),
            pl.BlockSpec((tm, d), lambda i, j: (i, 0)),
            pl.BlockSpec((1, d), lambda i, j: (0, 0)),
            pl.BlockSpec((d, U_TILE), lambda i, j: (0, j)),
            pl.BlockSpec((1, U_TILE), lambda i, j: (0, j)),
        ],
        out_specs=pl.BlockSpec((tm, U_TILE), lambda i, j: (i, j)),
        out_shape=jax.ShapeDtypeStruct((t, U_COLS), F32),
        scratch_shapes=[pltpu.VMEM((tm, d), BF16)],
        compiler_params=_cparams(("parallel", "arbitrary")),
        name="inproj",
    )(jnp.asarray(U_KINDS, jnp.int32), x, g.reshape(1, d), w, gain)


def _outproj_body(h_ref, a_ref, b_ref, c_ref, w_ref, y_ref):
    ka, kb = a_ref.shape[1], b_ref.shape[1]
    y = h_ref[...] + jnp.dot(a_ref[...].astype(BF16), w_ref[0:ka, :], preferred_element_type=F32)
    y = y + jnp.dot(b_ref[...].astype(BF16), w_ref[ka:ka + kb, :], preferred_element_type=F32)
    y_ref[...] = y + jnp.dot(c_ref[...].astype(BF16), w_ref[ka + kb:, :], preferred_element_type=F32)


def _outproj(h, o_nsa, o_diff, o_ml, w):
    t, d = h.shape
    tm = _pick_tile(t, (768, 512, 256, 128, 64, 32, 16, 8))
    row = lambda a: pl.BlockSpec((tm, a.shape[1]), lambda i: (i, 0))
    return pl.pallas_call(
        _outproj_body,
        grid=(t // tm,),
        in_specs=[row(h), row(o_nsa), row(o_diff), row(o_ml), pl.BlockSpec(w.shape, lambda i: (0, 0))],
        out_specs=row(h),
        out_shape=jax.ShapeDtypeStruct((t, d), F32),
        compiler_params=_cparams(("parallel",)),
        name="outproj",
    )(h, o_nsa, o_diff, o_ml, w)


def _online_update(s, bias, krel, slopes, v, m_ref, l_ref, acc_ref, base, tq):
    def shift_of(slope):
        shift = (slope * LOG2E) * krel
        return jnp.broadcast_to(shift, (tq, shift.shape[1])) if bias is None else bias + shift

    if tq <= V7X_LANES:
        rows = slice(base, base + len(slopes) * tq)
        p, alpha = _online_softmax(s, jnp.concatenate([shift_of(sl) for sl in slopes], axis=0), m_ref, l_ref, rows)
        acc_ref[rows] = alpha * acc_ref[rows] + jnp.dot(p, v, preferred_element_type=F32)
        return
    for g, slope in enumerate(slopes):
        rows = slice(base + g * tq, base + (g + 1) * tq)
        p, alpha = _online_softmax(s[g * tq:(g + 1) * tq], shift_of(slope), m_ref, l_ref, rows)
        acc_ref[rows] = alpha * acc_ref[rows] + jnp.dot(p, v, preferred_element_type=F32)


def _online_softmax(s, shift, m_ref, l_ref, rows):
    sg = s + shift
    m_old = m_ref[rows]
    m_new = jnp.maximum(m_old, jnp.max(sg, axis=-1, keepdims=True))
    p = jnp.exp2(sg - m_new)
    alpha = jnp.exp2(m_old - m_new)
    l_ref[rows] = alpha * l_ref[rows] + jnp.sum(p, axis=-1, keepdims=True)
    m_ref[rows] = m_new
    return p.astype(BF16), alpha


def _init_state(m_ref, l_ref, acc_ref):
    m_ref[...] = jnp.full(m_ref.shape, M_INIT, F32)
    l_ref[...] = jnp.zeros(l_ref.shape, F32)
    acc_ref[...] = jnp.zeros(acc_ref.shape, F32)


def _mask_bias(keep):
    return jnp.where(keep, 0.0, NEG_INF)


def _key_offsets(n, first):
    return (first + lax.broadcasted_iota(jnp.int32, (1, n), 1)).astype(F32)


def _normalised(l_ref, acc_ref, rows):
    return acc_ref[rows] / jnp.maximum(l_ref[rows], 1e-30)


def _distance(tq, n, offset):
    return offset + lax.broadcasted_iota(jnp.int32, (tq, n), 0) - lax.broadcasted_iota(jnp.int32, (tq, n), 1)


def _block_expand(n_spad, n, first_key):
    blk = lax.broadcasted_iota(jnp.int32, (n_spad, n), 0)
    key = first_key + lax.broadcasted_iota(jnp.int32, (n_spad, n), 1)
    return jnp.where(lax.shift_right_logical(key, SLC_SHIFT) == blk, 1.0, 0.0).astype(BF16)


def _stack_heads(q_ref, k):
    return jnp.concatenate([q_ref[:, (k * NSA_GROUP + g) * HEAD_DIM:(k * NSA_GROUP + g + 1) * HEAD_DIM]
                            for g in range(NSA_GROUP)], axis=0).astype(BF16)


def _pad_keys(x, n):
    return jnp.concatenate([x, jnp.zeros((n - x.shape[0], x.shape[1]), x.dtype)], axis=0)


def _nsa_combine(gt_ref, oc_ref, o_ref, l_ref, acc_ref, tq):
    gt = jax.nn.sigmoid(gt_ref[:, 0:V7X_LANES])
    for h in range(NSA_HEADS):
        cols = slice(h * HEAD_DIM, (h + 1) * HEAD_DIM)
        o_slc = _normalised(l_ref, acc_ref, slice(h * tq, (h + 1) * tq))
        o_win = _normalised(l_ref, acc_ref, slice((NSA_HEADS + h) * tq, (NSA_HEADS + h + 1) * tq))
        c0 = G_NSA + 3 * h
        o_ref[:, cols] = gt[:, c0:c0 + 1] * oc_ref[:, cols] + gt[:, c0 + 1:c0 + 2] * o_slc + gt[:, c0 + 2:c0 + 3] * o_win


_NSA_SLOPES = _alibi(NSA_HEADS)
_DIFF_SLOPES = _alibi(DIFF_HEADS)


def _nsa_state(tq):
    rows = 2 * NSA_HEADS * tq
    return [pltpu.VMEM((rows, 1), F32), pltpu.VMEM((rows, 1), F32), pltpu.VMEM((rows, HEAD_DIM), F32)]


def _diff_state(tq):
    rows = 2 * DIFF_HEADS * tq
    return [pltpu.VMEM((rows, 1), F32), pltpu.VMEM((rows, 1), F32), pltpu.VMEM((rows, HEAD_DIM), F32)]


def _cmp_accumulate(rows_of, w_ref, o_ref):
    for c in range(4):
        for r in range(CMP_STRIDE):
            part = jnp.dot(rows_of(c, r).astype(BF16), w_ref[c // 2, r * HEAD_DIM:(r + 1) * HEAD_DIM, :],
                           preferred_element_type=F32)
            if r == 0:
                o_ref[0, c] = part
            else:
                o_ref[0, c] += part


def _cmp_partial_body(x0_ref, x1_ref, x2_ref, x3_ref, w_ref, o_ref):
    srcs = (x0_ref, x1_ref, x2_ref, x3_ref)
    per = x0_ref.shape[0] // CMP_STRIDE
    _cmp_accumulate(lambda c, r: srcs[c][pl.ds(r, per, stride=CMP_STRIDE), :], w_ref, o_ref)


def _cmp_partial_paged_body(pt_ref, *refs, pps):
    del pt_ref
    pages, w_ref, o_ref, slab_ref = refs[:pps], refs[pps], refs[pps + 1], refs[pps + 2]
    for i, pg in enumerate(pages):
        for c in range(4):
            slab_ref[c, i * PAGE_SIZE:(i + 1) * PAGE_SIZE, :] = _page_rows(pg, c)
    per = pps * PAGE_SIZE // CMP_STRIDE
    _cmp_accumulate(lambda c, r: slab_ref[c, pl.ds(r, per, stride=CMP_STRIDE), :], w_ref, o_ref)


def _cmp_finish_body(*refs, has_tail, n_out):
    if has_tail:
        p_ref, tail_ref, w1_ref, pos_ref, w2_ref, g_ref, o_ref = refs
    else:
        p_ref, w1_ref, pos_ref, w2_ref, g_ref, o_ref = refs
    c = pl.program_id(1)
    p = p_ref[0, 0]
    n = p.shape[0]
    w1 = w1_ref[0]
    pb = jnp.dot(pos_ref[0].astype(BF16), w1, preferred_element_type=F32)
    nxt = pltpu.roll(p[:, HEAD_DIM:], n - 1, 0)
    if has_tail:
        rid = lax.broadcasted_iota(jnp.int32, (V7X_SUBLANES, HEAD_DIM), 0)
        tail8 = jnp.zeros((V7X_SUBLANES, HEAD_DIM), F32)
        xt = tail_ref[...].astype(BF16)
        for r in range(tail_ref.shape[0]):
            d = jnp.dot(xt, w1[r * HEAD_DIM:(r + 1) * HEAD_DIM, HEAD_DIM:], preferred_element_type=F32)
            tail8 = tail8 + jnp.where(rid == r, d, 0.0)
        tail = jnp.sum(tail8, axis=0, keepdims=True)
    else:
        tail = jnp.zeros((1, HEAD_DIM), F32)
    last = lax.broadcasted_iota(jnp.int32, (n, HEAD_DIM), 0) == n - 1
    z = p[:, :HEAD_DIM] + pb[0:1, :HEAD_DIM] + jnp.where(last, tail, nxt) + pb[1:2, HEAD_DIM:]
    y = jnp.dot((z * jax.nn.sigmoid(z)).astype(BF16), w2_ref[0].astype(BF16), preferred_element_type=F32)
    y = jnp.where(c < NSA_KV_HEADS, _rms_rows(y, g_ref[...]), y)
    o_ref[0, 0, 0:n] = y
    if n_out > n:
        o_ref[0, 0, n:n_out] = jnp.zeros((n_out - n, HEAD_DIM), F32)


def _cmp_weights(lp):
    half = CMP_STRIDE * HEAD_DIM
    w1 = lp['nsa_cmp_w1']
    w1ab = jnp.concatenate([w1[:, :half], w1[:, half:]], axis=2).astype(BF16)
    pos = lp['nsa_cmp_pos']
    pos2 = jnp.zeros((2, V7X_SUBLANES, half), F32)
    pos2 = pos2.at[:, 0].set(pos[:, :CMP_STRIDE].reshape(2, half)).at[:, 1].set(pos[:, CMP_STRIDE:].reshape(2, half))
    return w1ab, pos2


def _cmp_finish(p, tail_src, w1ab, pos2, lp, n_out, grp, u):
    b, _, n, _ = p.shape
    half = CMP_STRIDE * HEAD_DIM
    has_tail = tail_src is not None
    in_specs = [pl.BlockSpec((1, 1, n, 2 * HEAD_DIM), lambda i, c: (i, c, 0, 0))]
    args = [p]
    if has_tail:
        rb = grp.row0 // grp.t
        in_specs.append(pl.BlockSpec((grp.t, HEAD_DIM), lambda i, c: (rb + i, C_ROWS // HEAD_DIM + c)))
        args.append(u)
    in_specs += [
        pl.BlockSpec((1, half, 2 * HEAD_DIM), lambda i, c: (c // NSA_KV_HEADS, 0, 0)),
        pl.BlockSpec((1, V7X_SUBLANES, half), lambda i, c: (c // NSA_KV_HEADS, 0, 0)),
        pl.BlockSpec((1, HEAD_DIM, HEAD_DIM), lambda i, c: (c // NSA_KV_HEADS, 0, 0)),
        pl.BlockSpec((1, HEAD_DIM), lambda i, c: (0, 0)),
    ]
    args += [w1ab, pos2, lp['nsa_cmp_w2'], lp['nsa_k_norm'][0:1]]
    return pl.pallas_call(
        functools.partial(_cmp_finish_body, has_tail=has_tail, n_out=n_out),
        grid=(b, 4),
        in_specs=in_specs,
        out_specs=pl.BlockSpec((1, 1, n_out, HEAD_DIM), lambda i, c: (i, c, 0, 0)),
        out_shape=jax.ShapeDtypeStruct((b, 4, n_out, HEAD_DIM), F32),
        compiler_params=_cparams(("parallel", "parallel")),
        name="nsa_cmp_finish",
    )(*args)


def _compress_prompt(u, grp, lp):
    assert grp.past == 0 and grp.row0 == 0 and grp.t % (CMP_STRIDE * V7X_SUBLANES) == 0
    w1ab, pos2 = _cmp_weights(lp)
    n = grp.t // CMP_STRIDE
    group = lambda c: pl.BlockSpec((grp.t, HEAD_DIM), lambda i: (i, C_ROWS // HEAD_DIM + c))
    p = pl.pallas_call(
        _cmp_partial_body,
        grid=(grp.b,),
        in_specs=[group(0), group(1), group(2), group(3), pl.BlockSpec(w1ab.shape, lambda i: (0, 0, 0))],
        out_specs=pl.BlockSpec((1, 4, n, 2 * HEAD_DIM), lambda i: (i, 0, 0, 0)),
        out_shape=jax.ShapeDtypeStruct((grp.b, 4, n, 2 * HEAD_DIM), F32),
        compiler_params=_cparams(("parallel",)),
        name="nsa_cmp_partial",
    )(u, u, u, u, w1ab)
    return _cmp_finish(p, None, w1ab, pos2, lp, _round_up(n, V7X_LANES), grp, u), n - 1


def _pages_per_step(n_pages):
    return _pick_tile(n_pages, (16, 8, 4, 2, 1))


def _page_view(cache):
    assert cache.shape[3] * cache.shape[4] == ROW_GROUPS and cache.shape[5] == HEAD_DIM
    return cache.reshape(cache.shape[0], cache.shape[1], PAGE_SIZE * ROW_GROUPS, HEAD_DIM)


def _page_specs(layer, n_pages, pps):
    def spec(i):
        return pl.BlockSpec((1, 1, PAGE_SIZE * ROW_GROUPS, HEAD_DIM),
                            lambda b, c, pt: (layer, pt[b * n_pages + c * pps + i], 0, 0))
    return [spec(i) for i in range(pps)]


def _page_rows(page, j):
    return page[0, 0, pl.ds(j, PAGE_SIZE, stride=ROW_GROUPS), :]


def _page_piece(pages, j):
    return jnp.concatenate([_page_rows(pg, j).astype(BF16) for pg in pages], axis=0)


def _compress_sample(u, grp, lp, cache, page_table, layer):
    assert grp.past % PAGE_SIZE == 0 and grp.t < CMP_STRIDE
    w1ab, pos2 = _cmp_weights(lp)
    n_pages = grp.past // PAGE_SIZE
    pps = _pages_per_step(n_pages)
    per = pps * PAGE_SIZE // CMP_STRIDE
    n = grp.past // CMP_STRIDE
    p = pl.pallas_call(
        functools.partial(_cmp_partial_paged_body, pps=pps),
        grid_spec=pltpu.PrefetchScalarGridSpec(
            num_scalar_prefetch=1,
            grid=(grp.b, n_pages // pps),
            in_specs=_page_specs(layer, n_pages, pps) + [
                pl.BlockSpec(w1ab.shape, lambda b, c, pt: (0, 0, 0))],
            out_specs=pl.BlockSpec((1, 4, per, 2 * HEAD_DIM), lambda b, c, pt: (b, 0, c, 0)),
            scratch_shapes=[pltpu.VMEM((4, pps * PAGE_SIZE, HEAD_DIM), F32)],
        ),
        out_shape=jax.ShapeDtypeStruct((grp.b, 4, n, 2 * HEAD_DIM), F32),
        compiler_params=_cparams(("parallel", "arbitrary")),
        name="nsa_cmp_partial_paged",
    )(page_table, *([cache] * pps), w1ab)
    return _cmp_finish(p, u, w1ab, pos2, lp, _round_up(n + 1, V7X_LANES), grp, u), n


def _cmp_select_body(q_ref, kv_ref, o_ref, sel_ref, *, tq, q_off, n_cmp, n_slc, n_sel):
    qi = pl.program_id(1)
    n_cpad = kv_ref.shape[2]
    n_spad = sel_ref.shape[1] // NSA_KV_HEADS
    q_lo = q_off + qi * tq
    qpos = q_lo + lax.broadcasted_iota(jnp.int32, (tq, n_cpad), 0)
    ends = lax.broadcasted_iota(jnp.int32, (tq, n_cpad), 1) * CMP_STRIDE + (CMP_BLOCK - 1)
    d = qpos - ends
    mask = d >= 0
    df = d.astype(F32)
    c_lo = lax.broadcasted_iota(jnp.int32, (n_cpad, n_spad), 0) * CMP_STRIDE
    j_lo = lax.broadcasted_iota(jnp.int32, (n_cpad, n_spad), 1) * SLC_BLOCK
    cover = jnp.where(c_lo < j_lo + SLC_BLOCK, jnp.where(c_lo + CMP_BLOCK > j_lo, 1.0, 0.0), 0.0)
    cover = jnp.where(c_lo < n_cmp * CMP_STRIDE, cover, 0.0)
    qp = q_lo + lax.broadcasted_iota(jnp.int32, (tq, n_spad), 0)
    j = lax.broadcasted_iota(jnp.int32, (tq, n_spad), 1)
    cur = lax.shift_right_logical(qp, SLC_SHIFT)
    bonus = FORCE_BONUS * jnp.where(j == 0, 1.0, jnp.where(j == cur, 1.0, jnp.where(j == cur - 1, 1.0, 0.0)))
    valid = j * SLC_BLOCK <= qp
    for k in range(NSA_KV_HEADS):
        q = jnp.concatenate([q_ref[:, (k * NSA_GROUP + g) * HEAD_DIM:(k * NSA_GROUP + g + 1) * HEAD_DIM]
                             for g in range(NSA_GROUP)], axis=0)
        s = lax.dot_general(q, kv_ref[0, k], NT, precision=HIGHEST, preferred_element_type=F32)
        vc = kv_ref[0, NSA_KV_HEADS + k].astype(BF16)
        psum = jnp.zeros((tq, n_cpad), F32)
        for g in range(NSA_GROUP):
            h = k * NSA_GROUP + g
            sg = jnp.where(mask, s[g * tq:(g + 1) * tq] - (_NSA_SLOPES[h] * LOG2E) * df, NEG_INF)
            m = jnp.max(sg, axis=-1, keepdims=True)
            p = jnp.where(mask, jnp.exp2(sg - m), 0.0)
            p = p / jnp.maximum(jnp.sum(p, axis=-1, keepdims=True), 1e-30)
            o_ref[:, h * HEAD_DIM:(h + 1) * HEAD_DIM] = jnp.dot(p.astype(BF16), vc, preferred_element_type=F32)
            psum = psum + p
        imp = jnp.dot(psum, cover, precision=HIGHEST, preferred_element_type=F32)
        score = jnp.where(valid, imp + bonus, NEG_INF)
        cnt = jnp.zeros((tq, n_spad), F32)
        for i in range(n_slc):
            ci = score[:, i:i + 1]
            cnt = cnt + jnp.where(ci > score, 1.0, jnp.where(ci == score, jnp.where(j > i, 1.0, 0.0), 0.0))
        sel_ref[:, k * n_spad:(k + 1) * n_spad] = jnp.where(cnt < n_sel, jnp.where(j < n_slc, 0.0, NEG_INF), NEG_INF)


def _cmp_select(u, grp, summaries, n_cmp):
    n_slc = -(-(grp.past + grp.t) // SLC_BLOCK)
    n_spad = _round_up(n_slc, V7X_LANES)
    n_cpad = summaries.shape[2]
    tq = _pick_tile(grp.t, (256, 128, 64, 32, 16, 8))
    nq = grp.t // tq
    rb = grp.row0 // tq
    body = functools.partial(_cmp_select_body, tq=tq, q_off=grp.past, n_cmp=n_cmp, n_slc=n_slc,
                             n_sel=min(SLC_TOPN, n_slc))
    return pl.pallas_call(
        body,
        grid=(grp.b, nq),
        in_specs=[
            pl.BlockSpec((tq, NSA_W), lambda b, i: (rb + b * nq + i, 0)),
            pl.BlockSpec((1, 4, n_cpad, HEAD_DIM), lambda b, i: (b, 0, 0, 0)),
        ],
        out_specs=[
            pl.BlockSpec((tq, NSA_W), lambda b, i: (b * nq + i, 0)),
            pl.BlockSpec((tq, NSA_KV_HEADS * n_spad), lambda b, i: (b * nq + i, 0)),
        ],
        out_shape=[
            jax.ShapeDtypeStruct((grp.b * grp.t, NSA_W), F32),
            jax.ShapeDtypeStruct((grp.b * grp.t, NSA_KV_HEADS * n_spad), F32),
        ],
        compiler_params=_cparams(("parallel", "parallel")),
        name="nsa_cmp_select",
    )(u, summaries)


def _nsa_prompt_body(q_ref, ks_ref, kw_ref, sel_ref, oc_ref, gt_ref, o_ref, m_ref, l_ref, acc_ref, *, tq, tk):
    qi = pl.program_id(1)
    kj = pl.program_id(2)
    n_spad = sel_ref.shape[1] // NSA_KV_HEADS

    @pl.when(kj == 0)
    def _():
        _init_state(m_ref, l_ref, acc_ref)

    q_lo = qi * tq
    k_lo = kj * tk
    slc_on = k_lo <= q_lo + (tq - 1)
    win_on = jnp.logical_and(slc_on, k_lo + (tk - 1) >= q_lo - WINDOW)

    krel = _key_offsets(tk, k_lo - q_lo)

    @pl.when(slc_on)
    def _():
        causal = _mask_bias(_distance(tq, tk, q_lo - k_lo) >= 0)
        expand = _block_expand(n_spad, tk, k_lo)
        kv = ks_ref[...].astype(BF16)
        for k in range(NSA_KV_HEADS):
            s = lax.dot_general(_stack_heads(q_ref, k), kv[:, k * HEAD_DIM:(k + 1) * HEAD_DIM], NT,
                                preferred_element_type=F32)
            chosen = jnp.dot(sel_ref[:, k * n_spad:(k + 1) * n_spad].astype(BF16), expand,
                             preferred_element_type=F32)
            _online_update(s, causal + chosen, krel, _NSA_SLOPES[k * NSA_GROUP:(k + 1) * NSA_GROUP],
                           kv[:, (NSA_KV_HEADS + k) * HEAD_DIM:(NSA_KV_HEADS + k + 1) * HEAD_DIM],
                           m_ref, l_ref, acc_ref, k * NSA_GROUP * tq, tq)

    @pl.when(win_on)
    def _():
        d = _distance(tq, tk, q_lo - k_lo)
        bias = _mask_bias(jnp.logical_and(d >= 0, d <= WINDOW))
        kv = kw_ref[...].astype(BF16)
        for k in range(NSA_KV_HEADS):
            s = lax.dot_general(_stack_heads(q_ref, k), kv[:, k * HEAD_DIM:(k + 1) * HEAD_DIM], NT,
                                preferred_element_type=F32)
            _online_update(s, bias, krel, _NSA_SLOPES[k * NSA_GROUP:(k + 1) * NSA_GROUP],
                           kv[:, (NSA_KV_HEADS + k) * HEAD_DIM:(NSA_KV_HEADS + k + 1) * HEAD_DIM],
                           m_ref, l_ref, acc_ref, (NSA_KV_HEADS + k) * NSA_GROUP * tq, tq)

    @pl.when(kj == pl.num_programs(2) - 1)
    def _():
        _nsa_combine(gt_ref, oc_ref, o_ref, l_ref, acc_ref, tq)


def _nsa_prompt(u, grp, o_cmp, sel):
    assert grp.past == 0 and grp.row0 == 0
    t = grp.t
    tq = _pick_tile(t, (256, 128, 64, 32, 16, 8))
    tk = _pick_tile(t, (512, 256, 128))
    nq, nk = t // tq, t // tk
    kvw = 2 * NSA_KV_W

    def last_tile(i):
        return (i * tq + (tq - 1)) // tk

    def slc_rows(b, i, j):
        return (b * nk + jnp.minimum(j, last_tile(i)), (C_ROWS + kvw) // kvw)

    def win_rows(b, i, j):
        first = jnp.maximum(i * tq - WINDOW, 0) // tk
        return (b * nk + jnp.clip(j, first, last_tile(i)), C_WIN // kvw)

    return pl.pallas_call(
        functools.partial(_nsa_prompt_body, tq=tq, tk=tk),
        grid=(grp.b, nq, nk),
        in_specs=[
            pl.BlockSpec((tq, NSA_W), lambda b, i, j: (b * nq + i, 0)),
            pl.BlockSpec((tk, kvw), slc_rows),
            pl.BlockSpec((tk, kvw), win_rows),
            pl.BlockSpec((tq, sel.shape[1]), lambda b, i, j: (b * nq + i, 0)),
            pl.BlockSpec((tq, NSA_W), lambda b, i, j: (b * nq + i, 0)),
            pl.BlockSpec((tq, U_TILE), lambda b, i, j: (b * nq + i, C_SMALL // U_TILE)),
        ],
        out_specs=pl.BlockSpec((tq, NSA_W), lambda b, i, j: (b * nq + i, 0)),
        out_shape=jax.ShapeDtypeStruct((grp.b * t, NSA_W), F32),
        scratch_shapes=_nsa_state(tq),
        compiler_params=_cparams(("parallel", "parallel", "arbitrary")),
        name="nsa_attn_prompt",
    )(u, u, u, sel, o_cmp, u)


def _nsa_sample_body(pt_ref, q_ref, *rest, pps, past, n_win):
    pages = rest[:pps]
    knew_ref, wst_ref, wnew_ref, sel_ref, oc_ref, gt_ref, o_ref, m_ref, l_ref, acc_ref = rest[pps:]
    del pt_ref
    c = pl.program_id(1)
    tq = q_ref.shape[0]
    n_spad = sel_ref.shape[1] // NSA_KV_HEADS
    n = pps * PAGE_SIZE

    @pl.when(c == 0)
    def _():
        _init_state(m_ref, l_ref, acc_ref)

    def attend(keys_of, vals_of, bias_of, krel, branch):
        for k in range(NSA_KV_HEADS):
            s = lax.dot_general(_stack_heads(q_ref, k), keys_of(k), NT, preferred_element_type=F32)
            _online_update(s, bias_of(k), krel, _NSA_SLOPES[k * NSA_GROUP:(k + 1) * NSA_GROUP], vals_of(k),
                           m_ref, l_ref, acc_ref, (branch * NSA_KV_HEADS + k) * NSA_GROUP * tq, tq)

    def slab(ref, j):
        return ref[:, j * HEAD_DIM:(j + 1) * HEAD_DIM]

    first_key = c * n
    expand = _block_expand(n_spad, n, first_key)
    attend(lambda k: _page_piece(pages, 2 * NSA_KV_HEADS + k), lambda k: _page_piece(pages, 3 * NSA_KV_HEADS + k),
           lambda k: jnp.dot(sel_ref[:, k * n_spad:(k + 1) * n_spad].astype(BF16), expand,
                             preferred_element_type=F32),
           _key_offsets(n, first_key - past), 0)

    @pl.when(c == pl.num_programs(1) - 1)
    def _():
        npad = V7X_LANES
        real = lax.broadcasted_iota(jnp.int32, (tq, npad), 1) < tq
        causal = _mask_bias(jnp.logical_and(_distance(tq, npad, 0) >= 0, real))
        lane = lax.broadcasted_iota(jnp.int32, (tq, n_spad), 1)

        def chosen_new(k):
            col = jnp.where(lane == past // SLC_BLOCK, sel_ref[:, k * n_spad:(k + 1) * n_spad], 0.0)
            return causal + jnp.sum(col, axis=-1, keepdims=True)

        knew = _pad_keys(knew_ref[...], npad).astype(BF16)
        attend(lambda k: slab(knew, k), lambda k: slab(knew, NSA_KV_HEADS + k), chosen_new, _key_offsets(npad, 0), 0)
        wbuf = wst_ref[0, 0].astype(BF16)
        in_window = _mask_bias(_distance(tq, n_win, n_win) <= WINDOW)
        attend(lambda k: slab(wbuf, k), lambda k: slab(wbuf, NSA_KV_HEADS + k), lambda k: in_window,
               _key_offsets(n_win, -n_win), 1)
        wnew = _pad_keys(wnew_ref[...], npad).astype(BF16)
        attend(lambda k: slab(wnew, k), lambda k: slab(wnew, NSA_KV_HEADS + k), lambda k: causal,
               _key_offsets(npad, 0), 1)
        _nsa_combine(gt_ref, oc_ref, o_ref, l_ref, acc_ref, tq)


def _nsa_sample(u, grp, o_cmp, sel, cache, win_state, page_table, layer):
    t = grp.t
    n_win = win_state.shape[2]
    assert n_win == WINDOW and grp.past % SLC_BLOCK == 0 and t <= SLC_BLOCK and grp.row0 % t == 0
    n_pages = grp.past // PAGE_SIZE
    pps = _pages_per_step(n_pages)
    kvw = 2 * NSA_KV_W
    rb = grp.row0 // t
    tok = lambda width, col: pl.BlockSpec((t, width), lambda b, c, pt: (rb + b, col))
    own = lambda width: pl.BlockSpec((t, width), lambda b, c, pt: (b, 0))
    return pl.pallas_call(
        functools.partial(_nsa_sample_body, pps=pps, past=grp.past, n_win=n_win),
        grid_spec=pltpu.PrefetchScalarGridSpec(
            num_scalar_prefetch=1,
            grid=(grp.b, n_pages // pps),
            in_specs=[tok(NSA_W, 0)] + _page_specs(layer, n_pages, pps) + [
                tok(kvw, (C_ROWS + kvw) // kvw),
                pl.BlockSpec((1, 1, n_win, kvw), lambda b, c, pt: (layer, b, 0, 0)),
                tok(kvw, C_WIN // kvw),
                own(sel.shape[1]),
                own(NSA_W),
                tok(U_TILE, C_SMALL // U_TILE),
            ],
            out_specs=own(NSA_W),
            scratch_shapes=_nsa_state(t),
        ),
        out_shape=jax.ShapeDtypeStruct((grp.b * t, NSA_W), F32),
        compiler_params=_cparams(("parallel", "arbitrary")),
        name="nsa_attn_paged",
    )(page_table, u, *([cache] * pps), u, win_state, u, sel, o_cmp, u)


def _diff_update(q_ref, group_of, bias, krel, m_ref, l_ref, acc_ref):
    tq = q_ref.shape[0]
    low = lax.broadcasted_iota(jnp.int32, (tq, HEAD_DIM), 1) < DIFF_QK_DIM
    scores, shifts = [], []
    for h in range(DIFF_HEADS):
        qh = q_ref[:, h * HEAD_DIM:(h + 1) * HEAD_DIM]
        q2 = jnp.concatenate([jnp.where(low, qh, 0.0), jnp.where(low, 0.0, qh)], axis=0).astype(BF16)
        scores.append(lax.dot_general(q2, group_of(h), NT, preferred_element_type=F32))
        shift = (_DIFF_SLOPES[h] * LOG2E) * krel
        shift = jnp.broadcast_to(shift, (tq, shift.shape[1])) if bias is None else bias + shift
        shifts += [shift, shift]
    p, alpha = _online_softmax(jnp.concatenate(scores, axis=0), jnp.concatenate(shifts, axis=0), m_ref, l_ref,
                               slice(0, 2 * DIFF_HEADS * tq))
    for h in range(DIFF_HEADS):
        rows = slice(2 * h * tq, 2 * (h + 1) * tq)
        acc_ref[rows] = alpha[rows] * acc_ref[rows] + jnp.dot(p[rows], group_of(DIFF_HEADS + h),
                                                               preferred_element_type=F32)


def _diff_finish(lam_ref, g_ref, o_ref, l_ref, acc_ref, post):
    tq = o_ref.shape[0]
    lam = lam_ref[0]
    for h in range(DIFF_HEADS):
        o1 = _normalised(l_ref, acc_ref, slice(2 * h * tq, (2 * h + 1) * tq))
        o2 = _normalised(l_ref, acc_ref, slice((2 * h + 1) * tq, (2 * h + 2) * tq))
        o_ref[:, h * HEAD_DIM:(h + 1) * HEAD_DIM] = _rms_rows(o1 - lam * o2, g_ref[...]) * post


def _diff_prompt_body(lam_ref, q_ref, kv_ref, g_ref, o_ref, m_ref, l_ref, acc_ref, *, tq, tk, post):
    qi = pl.program_id(1)
    kj = pl.program_id(2)

    @pl.when(kj == 0)
    def _():
        _init_state(m_ref, l_ref, acc_ref)

    @pl.when(kj * tk <= qi * tq + (tq - 1))
    def _():
        kv = kv_ref[...].astype(BF16)
        _diff_update(q_ref, lambda j: kv[:, j * HEAD_DIM:(j + 1) * HEAD_DIM],
                     _mask_bias(_distance(tq, tk, qi * tq - kj * tk) >= 0), _key_offsets(tk, kj * tk - qi * tq),
                     m_ref, l_ref, acc_ref)

    @pl.when(kj == pl.num_programs(2) - 1)
    def _():
        _diff_finish(lam_ref, g_ref, o_ref, l_ref, acc_ref, post)


def _diff_prompt(u, grp, lam, gain, post):
    assert grp.past == 0 and grp.row0 == 0
    t = grp.t
    tq = _pick_tile(t, (256, 128, 64, 32, 16, 8))
    tk = _pick_tile(t, (512, 256, 128))
    nq, nk = t // tq, t // tk
    return pl.pallas_call(
        functools.partial(_diff_prompt_body, tq=tq, tk=tk, post=post),
        grid=(grp.b, nq, nk),
        in_specs=[
            pl.BlockSpec(memory_space=pltpu.SMEM),
            pl.BlockSpec((tq, DIFF_W), lambda b, i, j: (b * nq + i, C_DQ // DIFF_W)),
            pl.BlockSpec((tk, 2 * DIFF_W),
                         lambda b, i, j: (b * nk + jnp.minimum(j, (i * tq + (tq - 1)) // tk), C_DROWS // (2 * DIFF_W))),
            pl.BlockSpec((1, HEAD_DIM), lambda b, i, j: (0, 0)),
        ],
        out_specs=pl.BlockSpec((tq, DIFF_W), lambda b, i, j: (b * nq + i, 0)),
        out_shape=jax.ShapeDtypeStruct((grp.b * t, DIFF_W), F32),
        scratch_shapes=_diff_state(tq),
        compiler_params=_cparams(("parallel", "parallel", "arbitrary")),
        name="diff_attn_prompt",
    )(lam, u, u, gain)


def _diff_sample_body(pt_ref, lam_ref, q_ref, *rest, pps, past, post):
    pages = rest[:pps]
    new_ref, g_ref, o_ref, m_ref, l_ref, acc_ref = rest[pps:]
    del pt_ref
    c = pl.program_id(1)
    tq = q_ref.shape[0]
    n = pps * PAGE_SIZE

    @pl.when(c == 0)
    def _():
        _init_state(m_ref, l_ref, acc_ref)

    _diff_update(q_ref, lambda j: _page_piece(pages, j), None, _key_offsets(n, c * n - past), m_ref, l_ref, acc_ref)

    @pl.when(c == pl.num_programs(1) - 1)
    def _():
        npad = V7X_LANES
        keep = jnp.logical_and(_distance(tq, npad, 0) >= 0, lax.broadcasted_iota(jnp.int32, (tq, npad), 1) < tq)
        new = _pad_keys(new_ref[...], npad).astype(BF16)
        _diff_update(q_ref, lambda j: new[:, j * HEAD_DIM:(j + 1) * HEAD_DIM], _mask_bias(keep),
                     _key_offsets(npad, 0), m_ref, l_ref, acc_ref)
        _diff_finish(lam_ref, g_ref, o_ref, l_ref, acc_ref, post)


def _diff_sample(u, grp, lam, gain, post, cache, page_table, layer):
    t = grp.t
    assert grp.past % PAGE_SIZE == 0 and grp.row0 % t == 0
    n_pages = grp.past // PAGE_SIZE
    pps = _pages_per_step(n_pages)
    rb = grp.row0 // t
    return pl.pallas_call(
        functools.partial(_diff_sample_body, pps=pps, past=grp.past, post=post),
        grid_spec=pltpu.PrefetchScalarGridSpec(
            num_scalar_prefetch=1,
            grid=(grp.b, n_pages // pps),
            in_specs=[
                pl.BlockSpec(memory_space=pltpu.SMEM),
                pl.BlockSpec((t, DIFF_W), lambda b, c, pt: (rb + b, C_DQ // DIFF_W)),
            ] + _page_specs(layer, n_pages, pps) + [
                pl.BlockSpec((t, 2 * DIFF_W), lambda b, c, pt: (rb + b, C_DROWS // (2 * DIFF_W))),
                pl.BlockSpec((1, HEAD_DIM), lambda b, c, pt: (0, 0)),
            ],
            out_specs=pl.BlockSpec((t, DIFF_W), lambda b, c, pt: (b, 0)),
            scratch_shapes=_diff_state(t),
        ),
        out_shape=jax.ShapeDtypeStruct((grp.b * t, DIFF_W), F32),
        compiler_params=_cparams(("parallel", "arbitrary")),
        name="diff_attn_paged",
    )(page_table, lam, u, *([cache] * pps), u, gain)


def _mlstm_body(xq_ref, xk_ref, v_ref, og_ref, bq_ref, bk_ref, wq_ref, wk_ref, cq_ref, ck_ref, li_ref, lf_ref,
                c0_ref, n0_ref, m0_ref, g_ref, h_ref, c_ref, n_ref, m_ref, q_s, k_s, *, lc, nc):
    t = xq_ref.shape[0]
    row = lax.broadcasted_iota(jnp.int32, (lc, lc), 0)
    col = lax.broadcasted_iota(jnp.int32, (lc, lc), 1)
    tril = row >= col
    triu = row <= col
    eye = row == col
    gain = g_ref[...]

    def conv(x_ref, buf_ref, w_ref, b_ref):
        ext = jnp.concatenate([buf_ref[0], x_ref[...]], axis=0)
        y = b_ref[...]
        for i in range(CONV_WIDTH):
            shifted = ext if i == CONV_WIDTH - 1 else pltpu.roll(ext, CONV_WIDTH - 1 - i, 0)
            y = y + shifted[V7X_SUBLANES:V7X_SUBLANES + t] * w_ref[i:i + 1, :]
        return y * jax.nn.sigmoid(y)

    q_s[...] = conv(xq_ref, bq_ref, wq_ref, cq_ref) * MLSTM_DIM ** -0.5
    k_s[...] = conv(xk_ref, bk_ref, wk_ref, ck_ref)

    def to_col(r):
        return jnp.sum(jnp.where(eye, jnp.broadcast_to(r, (lc, lc)), 0.0), axis=1, keepdims=True)

    def step(ci, carry):
        c, n, m = carry
        rows = pl.ds(ci * lc if nc == 1 else pl.multiple_of(ci * lc, lc), lc)
        qc = q_s[rows, :]
        kc = k_s[rows, :]
        vc = v_ref[rows, :]
        li = li_ref[0, pl.ds(ci, 1), :]
        lf = lf_ref[0, pl.ds(ci, 1), :]
        lf_b = jnp.broadcast_to(lf, (lc, lc))
        bcum_c = jnp.sum(jnp.where(tril, lf_b, 0.0), axis=1, keepdims=True)
        bcum_r = jnp.sum(jnp.where(triu, jnp.broadcast_to(to_col(lf), (lc, lc)), 0.0), axis=0, keepdims=True)
        dmat = jnp.where(tril, bcum_c - bcum_r + li, NEG_INF)
        inter = bcum_c + m
        mj = jnp.maximum(inter, jnp.max(dmat, axis=1, keepdims=True))
        wts = jnp.exp(dmat - mj)
        a = jnp.exp(inter - mj)
        sqk = lax.dot_general(qc, kc, NT, precision=HIGHEST, preferred_element_type=F32) * wts
        num = a * jnp.dot(qc, c, precision=HIGHEST, preferred_element_type=F32) + \
            jnp.dot(sqk, vc, precision=HIGHEST, preferred_element_type=F32)
        den = a * jnp.sum(qc * n, axis=1, keepdims=True) + jnp.sum(sqk, axis=1, keepdims=True)
        h = num / jnp.maximum(jnp.abs(den), jnp.exp(-mj))
        h_ref[rows, :] = _rms_rows(h, gain) * jax.nn.sigmoid(og_ref[rows, :])
        b_last = jnp.sum(lf, axis=1, keepdims=True)
        gl = b_last - bcum_r + li
        m_new = jnp.maximum(b_last + m, jnp.max(gl, axis=1, keepdims=True))
        decay = jnp.exp(b_last + m - m_new)
        kw = kc * to_col(jnp.exp(gl - m_new))
        c_new = decay * c + lax.dot_general(kw, vc, (((0,), (0,)), ((), ())), precision=HIGHEST,
                                            preferred_element_type=F32)
        n_new = decay * n + jnp.sum(kw, axis=0, keepdims=True)
        return c_new, n_new, m_new

    init = (c0_ref[0], n0_ref[0], m0_ref[0][:, 0:1])
    c, n, m = step(0, init) if nc == 1 else lax.fori_loop(0, nc, step, init)
    c_ref[0] = c
    n_ref[0] = n
    m_ref[0] = jnp.broadcast_to(m, (1, MLSTM_DIM))


def _mlstm(u, grp, lp, c0, n0, m0, conv_buf):
    b, t = grp.b, grp.t
    assert grp.row0 % t == 0 and t % V7X_SUBLANES == 0
    bh = b * MLSTM_HEADS
    hd = MLSTM_DIM
    lc = MLSTM_CHUNK if t % MLSTM_CHUNK == 0 else t
    nc = t // lc
    rb = grp.row0 // t
    gates = u[grp.row0:grp.row0 + b * t, C_SMALL:C_SMALL + U_TILE].reshape(b, t, U_TILE)
    log_i = gates[:, :, G_MI:G_MI + MLSTM_HEADS] + lp['ml_gate_b'][0]
    log_f = jax.nn.log_sigmoid(gates[:, :, G_MF:G_MF + MLSTM_HEADS] + lp['ml_gate_b'][1])
    log_i = log_i.transpose(0, 2, 1).reshape(bh, nc, lc)
    log_f = log_f.transpose(0, 2, 1).reshape(bh, nc, lc)
    buf8 = jnp.pad(conv_buf, ((0, 0), (V7X_SUBLANES - (CONV_WIDTH - 1), 0), (0, 0)))
    w8 = jnp.pad(lp['ml_conv_w'], ((0, V7X_SUBLANES - CONV_WIDTH), (0, 0)))
    cb = lp['ml_conv_b'].reshape(1, 2 * MLSTM_W)
    tok = lambda col0: pl.BlockSpec((t, hd), lambda i, h: (rb + i, col0 // hd + h))
    buf = lambda off: pl.BlockSpec((1, V7X_SUBLANES, hd), lambda i, h: (i, 0, off + h))
    wsp = lambda off: pl.BlockSpec((V7X_SUBLANES, hd), lambda i, h: (0, off + h))
    bsp = lambda off: pl.BlockSpec((1, hd), lambda i, h: (0, off + h))
    gate = pl.BlockSpec((1, nc, lc), lambda i, h: (i * MLSTM_HEADS + h, 0, 0))
    vec = pl.BlockSpec((1, 1, hd), lambda i, h: (i * MLSTM_HEADS + h, 0, 0))
    mat = pl.BlockSpec((1, hd, hd), lambda i, h: (i * MLSTM_HEADS + h, 0, 0))
    h, c, n, m = pl.pallas_call(
        functools.partial(_mlstm_body, lc=lc, nc=nc),
        grid=(b, MLSTM_HEADS),
        in_specs=[tok(C_MQ), tok(C_MK), tok(C_MV), tok(C_MO), buf(0), buf(MLSTM_HEADS), wsp(0), wsp(MLSTM_HEADS),
                  bsp(0), bsp(MLSTM_HEADS), gate, gate, mat, vec, vec, pl.BlockSpec((1, hd), lambda i, h: (0, 0))],
        out_specs=[pl.BlockSpec((t, hd), lambda i, h: (i, h)), mat, vec, vec],
        out_shape=[
            jax.ShapeDtypeStruct((b * t, MLSTM_W), F32),
            jax.ShapeDtypeStruct((bh, hd, hd), F32),
            jax.ShapeDtypeStruct((bh, 1, hd), F32),
            jax.ShapeDtypeStruct((bh, 1, hd), F32),
        ],
        scratch_shapes=[pltpu.VMEM((t, hd), F32), pltpu.VMEM((t, hd), F32)],
        compiler_params=_cparams(("parallel", "parallel")),
        name="mlstm",
    )(u, u, u, u, buf8, buf8, w8, w8, cb, cb, log_i, log_f, c0.reshape(bh, hd, hd), n0.reshape(bh, 1, hd),
      jnp.broadcast_to(m0.reshape(bh, 1, 1), (bh, 1, hd)), lp['ml_out_norm'].reshape(1, hd))
    return (h, c.reshape(b, MLSTM_HEADS, hd, hd), n.reshape(b, MLSTM_HEADS, hd), m[:, 0, 0].reshape(b, MLSTM_HEADS))


def _pack_w_in(w_in):
    cuts = np.cumsum((0,) + IN_SPLITS)
    cols = [w_in[..., cuts[i]:cuts[i + 1]] for i in range(len(IN_SPLITS))]
    packed = jnp.concatenate([cols[i] for i in _MAIN_ORDER] + [cols[i] for i in _SMALL_ORDER], axis=-1)
    pad = [(0, 0)] * (packed.ndim - 1) + [(0, U_COLS - packed.shape[-1])]
    return jnp.pad(packed, pad).astype(BF16)


def _u_gain(lp):
    g = jnp.ones((U_COLS,), F32)
    g = g.at[C_NQ:C_NQ + NSA_W].set(jnp.tile(lp['nsa_q_norm'], NSA_HEADS) * (HEAD_DIM ** -0.5 * LOG2E))
    g = g.at[C_ROWS + 2 * NSA_KV_W:C_ROWS + 3 * NSA_KV_W].set(jnp.tile(lp['nsa_k_norm'][1], NSA_KV_HEADS))
    g = g.at[C_WIN:C_WIN + NSA_KV_W].set(jnp.tile(lp['nsa_k_norm'][2], NSA_KV_HEADS))
    g = g.at[C_DQ:C_DQ + DIFF_W].set(jnp.tile(lp['diff_q_norm'], 2 * DIFF_HEADS) * (DIFF_QK_DIM ** -0.5 * LOG2E))
    g = g.at[C_DROWS:C_DROWS + DIFF_W].set(jnp.tile(lp['diff_k_norm'], 2 * DIFF_HEADS))
    return g.reshape(1, U_COLS)


def _group_rows(u, grp, col0, width):
    return u[grp.row0:grp.row0 + grp.b * grp.t, col0:col0 + width].reshape(grp.b, grp.t, width)


def _tail_rows(buf, new, keep):
    n_new = min(keep, new.shape[1])
    parts = [buf[:, buf.shape[1] - (keep - n_new):], new[:, new.shape[1] - n_new:]] if keep > n_new else \
        [new[:, new.shape[1] - n_new:]]
    return jnp.concatenate(parts, axis=1)


def _trunk(x_prompt, x_sample, cache_nsa, cache_diff, state_win, state_ml_c, state_ml_n, state_ml_m, state_ml_conv,
           page_table, p):
    bp, tp, d = x_prompt.shape
    bs, ts, _ = x_sample.shape
    depth = p['w_in'].shape[0]
    past = page_table.shape[1] * PAGE_SIZE
    gp = _Group(bp, tp, 0, 0)
    gs = _Group(bs, ts, bp * tp, past)
    n_p = bp * tp
    w_in = _pack_w_in(p['w_in'])
    w_out = p['w_out'].astype(BF16)
    w13 = (p['ffn1_w13'].astype(BF16), p['ffn2_w13'].astype(BF16))
    w2 = (p['ffn1_w2'].astype(BF16), p['ffn2_w2'].astype(BF16))
    per_layer = ('mix_norm', 'nsa_q_norm', 'nsa_k_norm', 'nsa_cmp_pos', 'nsa_cmp_w1', 'nsa_cmp_w2', 'diff_q_norm',
                 'diff_k_norm', 'diff_lambda', 'diff_out_norm', 'ml_conv_w', 'ml_conv_b', 'ml_gate_b', 'ml_out_norm')
    cache_nsa = _page_view(cache_nsa)
    cache_diff = _page_view(cache_diff)
    win_state = state_win.reshape(state_win.shape[:3] + (2 * NSA_KV_W,))
    pt = page_table.reshape(-1)
    zeros = lambda *s: jnp.zeros(s, F32)
    p_state = (zeros(bp, MLSTM_HEADS, MLSTM_DIM, MLSTM_DIM), zeros(bp, MLSTM_HEADS, MLSTM_DIM),
               zeros(bp, MLSTM_HEADS), zeros(bp, CONV_WIDTH - 1, 2 * MLSTM_W))
    p_win0 = zeros(bp, 0, 2 * NSA_KV_W)

    x = jnp.concatenate([x_prompt.reshape(n_p, d), x_sample.reshape(bs * ts, d)], axis=0)
    outs = []
    for l in range(depth):
        lp = {name: p[name][l] for name in per_layer}
        lam_init = 0.8 - 0.6 * math.exp(-0.3 * l)
        lam_p = lp['diff_lambda']
        lam = (jnp.exp(jnp.sum(lam_p[0] * lam_p[1])) - jnp.exp(jnp.sum(lam_p[2] * lam_p[3])) + lam_init).reshape(1)
        dgain = lp['diff_out_norm'].reshape(1, HEAD_DIM)
        h = _ffn(x, p['ffn1_norm'][l], w13[0][l], w2[0][l])
        u = _inproj(h, lp['mix_norm'], w_in[l], _u_gain(lp))

        summ_p, n_cmp_p = _compress_prompt(u, gp, lp)
        oc_p, sel_p = _cmp_select(u, gp, summ_p, n_cmp_p)
        nsa_p = _nsa_prompt(u, gp, oc_p, sel_p)
        diff_p = _diff_prompt(u, gp, lam, dgain, 1.0 - lam_init)
        ml_p, c_p, nn_p, m_p = _mlstm(u, gp, lp, *p_state)

        summ_s, n_cmp_s = _compress_sample(u, gs, lp, cache_nsa, pt, l)
        oc_s, sel_s = _cmp_select(u, gs, summ_s, n_cmp_s)
        nsa_s = _nsa_sample(u, gs, oc_s, sel_s, cache_nsa, win_state, pt, l)
        diff_s = _diff_sample(u, gs, lam, dgain, 1.0 - lam_init, cache_diff, pt, l)
        ml_s, c_s, nn_s, m_s = _mlstm(u, gs, lp, state_ml_c[l], state_ml_n[l], state_ml_m[l], state_ml_conv[l])

        h = _outproj(h, jnp.concatenate([nsa_p, nsa_s]), jnp.concatenate([diff_p, diff_s]),
                     jnp.concatenate([ml_p, ml_s]), w_out[l])
        x = _ffn(h, p['ffn2_norm'][l], w13[1][l], w2[1][l])

        def rows(grp, col0, shape):
            return _group_rows(u, grp, col0, int(np.prod(shape))).reshape((grp.b, grp.t) + shape)

        nsa_shape, diff_shape = (4, NSA_KV_HEADS, HEAD_DIM), (2, DIFF_HEADS, HEAD_DIM)
        win_shape = (2, NSA_KV_HEADS, HEAD_DIM)
        win_rows_p = _tail_rows(p_win0, _group_rows(u, gp, C_WIN, 2 * NSA_KV_W), min(WINDOW, tp))
        win_rows_s = _tail_rows(win_state[l], _group_rows(u, gs, C_WIN, 2 * NSA_KV_W),
                                min(WINDOW, win_state.shape[2] + ts))
        conv_p = _tail_rows(p_state[3], _group_rows(u, gp, C_MQ, 2 * MLSTM_W), CONV_WIDTH - 1)
        conv_s = _tail_rows(state_ml_conv[l], _group_rows(u, gs, C_MQ, 2 * MLSTM_W), CONV_WIDTH - 1)
        outs.append((rows(gp, C_ROWS, nsa_shape), rows(gs, C_ROWS, nsa_shape),
                     rows(gp, C_DROWS, diff_shape), rows(gs, C_DROWS, diff_shape),
                     win_rows_p.reshape(win_rows_p.shape[:2] + win_shape),
                     win_rows_s.reshape(win_rows_s.shape[:2] + win_shape),
                     c_p, c_s, nn_p, nn_s, m_p, m_s, conv_p, conv_s))
    stacked = [jnp.stack(z) for z in zip(*outs)]
    return (x[:n_p].reshape(bp, tp, d), x[n_p:].reshape(bs, ts, d)) + tuple(stacked)


def kernel(x_prompt, x_sample, cache_nsa, cache_diff, state_win, state_ml_c, state_ml_n, state_ml_m, state_ml_conv,
           page_table, ffn1_norm, ffn1_w13, ffn1_w2, mix_norm, w_in, nsa_q_norm, nsa_k_norm, nsa_cmp_pos, nsa_cmp_w1,
           nsa_cmp_w2, diff_q_norm, diff_k_norm, diff_lambda, diff_out_norm, ml_conv_w, ml_conv_b, ml_gate_b,
           ml_out_norm, w_out, ffn2_norm, ffn2_w13, ffn2_w2):
    params = dict(ffn1_norm=ffn1_norm, ffn1_w13=ffn1_w13, ffn1_w2=ffn1_w2, mix_norm=mix_norm, w_in=w_in,
                  nsa_q_norm=nsa_q_norm, nsa_k_norm=nsa_k_norm, nsa_cmp_pos=nsa_cmp_pos, nsa_cmp_w1=nsa_cmp_w1,
                  nsa_cmp_w2=nsa_cmp_w2, diff_q_norm=diff_q_norm, diff_k_norm=diff_k_norm, diff_lambda=diff_lambda,
                  diff_out_norm=diff_out_norm, ml_conv_w=ml_conv_w, ml_conv_b=ml_conv_b, ml_gate_b=ml_gate_b,
                  ml_out_norm=ml_out_norm, w_out=w_out, ffn2_norm=ffn2_norm, ffn2_w13=ffn2_w13, ffn2_w2=ffn2_w2)
    return _trunk(x_prompt, x_sample, cache_nsa, cache_diff, state_win, state_ml_c, state_ml_n, state_ml_m,
                  state_ml_conv, page_table, params)
```

```python
import functools
import math
from typing import NamedTuple

import numpy as np
import jax
import jax.numpy as jnp
from jax import lax
from jax.experimental import pallas as pl
from jax.experimental.pallas import tpu as pltpu

F32 = jnp.float32
BF16 = jnp.bfloat16
HIGHEST = lax.Precision.HIGHEST
NT = (((1,), (1,)), ((), ()))

D_MODEL = 2048
D_FF = 5632
PAGE_SIZE = 128
HEAD_DIM = 128
ROW_GROUPS = 8
NSA_HEADS = 8
NSA_KV_HEADS = 2
NSA_GROUP = NSA_HEADS // NSA_KV_HEADS
CMP_BLOCK = 32
CMP_STRIDE = 16
SLC_BLOCK = 64
SLC_SHIFT = 6
SLC_TOPN = 16
WINDOW = 512
DIFF_HEADS = 4
DIFF_QK_DIM = HEAD_DIM // 2
MLSTM_HEADS = 4
MLSTM_DIM = 128
MLSTM_CHUNK = 64
CONV_WIDTH = 4
NORM_EPS = 1e-6
NEG_INF = -1e30
M_INIT = -1e29
LOG2E = 1.4426950408889634
FORCE_BONUS = 1e3

NSA_W = NSA_HEADS * HEAD_DIM
NSA_KV_W = NSA_KV_HEADS * HEAD_DIM
DIFF_W = DIFF_HEADS * HEAD_DIM
MLSTM_W = MLSTM_HEADS * MLSTM_DIM
IN_SPLITS = (NSA_W, 6 * NSA_KV_W, 3 * NSA_HEADS, DIFF_W, DIFF_W, DIFF_W,
             MLSTM_W, MLSTM_W, MLSTM_W, MLSTM_HEADS, MLSTM_HEADS, MLSTM_W)
IN_COLS = sum(IN_SPLITS)

V7X_LANES = 128
V7X_SUBLANES = 8
V7X_VMEM_LIMIT_BYTES = 56 * 1024 * 1024

U_TILE = 256
C_NQ = 0
C_ROWS = C_NQ + NSA_W
C_WIN = C_ROWS + 4 * NSA_KV_W
C_DQ = C_WIN + 2 * NSA_KV_W
C_DROWS = C_DQ + DIFF_W
C_MQ = C_DROWS + 2 * DIFF_W
C_MK = C_MQ + MLSTM_W
C_MV = C_MK + MLSTM_W
C_MO = C_MV + MLSTM_W
C_SMALL = C_MO + MLSTM_W
U_COLS = C_SMALL + U_TILE
U_STEP = 5 * U_TILE
G_NSA, G_MI, G_MF = 0, 3 * NSA_HEADS, 3 * NSA_HEADS + MLSTM_HEADS
U_KINDS = (1, 1, 1, 1, 0, 0, 1, 0, 1, 0, 2, 2, 2, 2, 0, 0, 0, 0, 0, 0, 0, 0, 0, 0, 0)
_MAIN_ORDER = (0, 1, 3, 4, 5, 6, 7, 8, 11)
_SMALL_ORDER = (2, 9, 10)


class _Group(NamedTuple):
    b: int
    t: int
    row0: int
    past: int


def _round_up(n, m):
    return -(-n // m) * m


def _cparams(semantics):
    return pltpu.CompilerParams(dimension_semantics=semantics, vmem_limit_bytes=V7X_VMEM_LIMIT_BYTES)


def _pick_tile(n, prefs):
    for p in prefs:
        if n % p == 0:
            return p
    return n


def _alibi(n):
    return [float(2.0 ** (-8.0 * i / n)) for i in range(1, n + 1)]


def _rms_rows(x, g):
    inv = lax.rsqrt(jnp.mean(x * x, axis=-1, keepdims=True) + NORM_EPS)
    return x * inv * g


def _ffn_body(x_ref, g_ref, wa_ref, wb_ref, w2_ref, o_ref, n_ref):
    @pl.when(pl.program_id(1) == 0)
    def _():
        x = x_ref[...]
        n_ref[...] = _rms_rows(x, g_ref[...]).astype(BF16)
        o_ref[...] = x

    n = n_ref[...]
    a = jnp.dot(n, wa_ref[...], preferred_element_type=F32)
    b = jnp.dot(n, wb_ref[...], preferred_element_type=F32)
    act = (0.5 * a * jax.nn.sigmoid(a) * b).astype(BF16)
    o_ref[...] += jnp.dot(act, w2_ref[...], preferred_element_type=F32)


def _ffn(x, g, w13, w2, layer):
    t, d = x.shape
    f = w2.shape[1]
    tm = _pick_tile(t, (768, 512, 256, 128, 64, 32, 16, 8))
    tf = _pick_tile(f, (512, 256, 128))
    nf = f // tf
    return pl.pallas_call(
        _ffn_body,
        grid=(t // tm, nf),
        in_specs=[
            pl.BlockSpec((tm, d), lambda i, j: (i, 0)),
            pl.BlockSpec((1, d), lambda i, j: (0, 0)),
            pl.BlockSpec((None, d, tf), lambda i, j: (layer, 0, j)),
            pl.BlockSpec((None, d, tf), lambda i, j: (layer, 0, j + nf)),
            pl.BlockSpec((None, tf, d), lambda i, j: (layer, j, 0)),
        ],
        out_specs=pl.BlockSpec((tm, d), lambda i, j: (i, 0)),
        out_shape=jax.ShapeDtypeStruct((t, d), F32),
        scratch_shapes=[pltpu.VMEM((tm, d), BF16)],
        compiler_params=_cparams(("parallel", "arbitrary")),
        name="ffn",
    )(x, g.reshape(1, d), w13, w13, w2)


def _inproj_body(kind_ref, x_ref, g_ref, w_ref, gain_ref, o_ref, n_ref):
    j = pl.program_id(1)

    @pl.when(j == 0)
    def _():
        n_ref[...] = _rms_rows(x_ref[...], g_ref[...]).astype(BF16)

    y = jnp.dot(n_ref[...], w_ref[...], preferred_element_type=F32)
    gain = gain_ref[...]
    for i in range(U_STEP // U_TILE):
        kind = kind_ref[j * (U_STEP // U_TILE) + i]
        halves = [slice(i * U_TILE + h * HEAD_DIM, i * U_TILE + (h + 1) * HEAD_DIM)
                  for h in range(U_TILE // HEAD_DIM)]

        @pl.when(kind == 0)
        def _(halves=halves):
            for cols in halves:
                o_ref[:, cols] = y[:, cols]

        @pl.when(kind == 1)
        def _(halves=halves):
            for cols in halves:
                o_ref[:, cols] = _rms_rows(y[:, cols], gain[:, cols])

        @pl.when(kind == 2)
        def _(halves=halves):
            low = lax.broadcasted_iota(jnp.int32, (y.shape[0], HEAD_DIM), 1) < DIFF_QK_DIM
            for cols in halves:
                yh = y[:, cols]
                sq = yh * yh
                s_lo = jnp.sum(jnp.where(low, sq, 0.0), axis=-1, keepdims=True)
                s_hi = jnp.sum(jnp.where(low, 0.0, sq), axis=-1, keepdims=True)
                inv = lax.rsqrt(jnp.where(low, s_lo, s_hi) * (1.0 / DIFF_QK_DIM) + NORM_EPS)
                o_ref[:, cols] = yh * inv * gain[:, cols]


def _inproj(x, g, w, gain, layer):
    t, d = x.shape
    tm = _pick_tile(t, (768, 512, 256, 128, 64, 32, 16, 8))
    return pl.pallas_call(
        _inproj_body,
        grid=(t // tm, U_COLS // U_STEP),
        in_specs=[
            pl.BlockSpec(memory_space=pltpu.SMEM),
            pl.BlockSpec((tm, d), lambda i, j: (i, 0)),
            pl.BlockSpec((1, d), lambda i, j: (0, 0)),
            pl.BlockSpec((None, d, U_STEP), lambda i, j: (layer, 0, j)),
            pl.BlockSpec((1, U_STEP), lambda i, j: (0, j)),
        ],
        out_specs=pl.BlockSpec((tm, U_STEP), lambda i, j: (i, j)),
        out_shape=jax.ShapeDtypeStruct((t, U_COLS), F32),
        scratch_shapes=[pltpu.VMEM((tm, d), BF16)],
        compiler_params=_cparams(("parallel", "arbitrary")),
        name="inproj",
    )(jnp.asarray(U_KINDS, jnp.int32), x, g.reshape(1, d), w, gain)


def _outproj_body(h_ref, a_ref, b_ref, c_ref, w_ref, y_ref):
    ka, kb = a_ref.shape[1], b_ref.shape[1]
    y = h_ref[...] + jnp.dot(a_ref[...].astype(BF16), w_ref[0:ka, :], preferred_element_type=F32)
    y = y + jnp.dot(b_ref[...].astype(BF16), w_ref[ka:ka + kb, :], preferred_element_type=F32)
    y_ref[...] = y + jnp.dot(c_ref[...].astype(BF16), w_ref[ka + kb:, :], preferred_element_type=F32)


def _outproj(h, o_nsa, o_diff, o_ml, w, layer):
    t, d = h.shape
    tm = _pick_tile(t, (768, 512, 256, 128, 64, 32, 16, 8))
    row = lambda a: pl.BlockSpec((tm, a.shape[1]), lambda i: (i, 0))
    return pl.pallas_call(
        _outproj_body,
        grid=(t // tm,),
        in_specs=[row(h), row(o_nsa), row(o_diff), row(o_ml),
                  pl.BlockSpec((None,) + w.shape[1:], lambda i: (layer, 0, 0))],
        out_specs=row(h),
        out_shape=jax.ShapeDtypeStruct((t, d), F32),
        compiler_params=_cparams(("parallel",)),
        name="outproj",
    )(h, o_nsa, o_diff, o_ml, w)


def _online_update(s, bias, krel, slopes, v, m_ref, l_ref, acc_ref, base, tq):
    def shift_of(slope):
        shift = (slope * LOG2E) * krel
        return jnp.broadcast_to(shift, (tq, shift.shape[1])) if bias is None else bias + shift

    if tq <= V7X_LANES:
        rows = slice(base, base + len(slopes) * tq)
        p, alpha = _online_softmax(s, jnp.concatenate([shift_of(sl) for sl in slopes], axis=0), m_ref, l_ref, rows)
        acc_ref[rows] = alpha * acc_ref[rows] + jnp.dot(p, v, preferred_element_type=F32)
        return
    for g, slope in enumerate(slopes):
        rows = slice(base + g * tq, base + (g + 1) * tq)
        p, alpha = _online_softmax(s[g * tq:(g + 1) * tq], shift_of(slope), m_ref, l_ref, rows)
        acc_ref[rows] = alpha * acc_ref[rows] + jnp.dot(p, v, preferred_element_type=F32)


def _online_softmax(s, shift, m_ref, l_ref, rows):
    sg = s + shift
    m_old = m_ref[rows]
    m_new = jnp.maximum(m_old, jnp.max(sg, axis=-1, keepdims=True))
    p = jnp.exp2(sg - m_new)
    alpha = jnp.exp2(m_old - m_new)
    l_ref[rows] = alpha * l_ref[rows] + jnp.sum(p, axis=-1, keepdims=True)
    m_ref[rows] = m_new
    return p.astype(BF16), alpha


def _init_state(m_ref, l_ref, acc_ref):
    m_ref[...] = jnp.full(m_ref.shape, M_INIT, F32)
    l_ref[...] = jnp.zeros(l_ref.shape, F32)
    acc_ref[...] = jnp.zeros(acc_ref.shape, F32)


def _mask_bias(keep):
    return jnp.where(keep, 0.0, NEG_INF)


def _key_offsets(n, first):
    return (first + lax.broadcasted_iota(jnp.int32, (1, n), 1)).astype(F32)


def _normalised(l_ref, acc_ref, rows):
    return acc_ref[rows] / jnp.maximum(l_ref[rows], 1e-30)


def _distance(tq, n, offset):
    return offset + lax.broadcasted_iota(jnp.int32, (tq, n), 0) - lax.broadcasted_iota(jnp.int32, (tq, n), 1)


def _block_expand(n_spad, n, first_key):
    blk = lax.broadcasted_iota(jnp.int32, (n_spad, n), 0)
    key = first_key + lax.broadcasted_iota(jnp.int32, (n_spad, n), 1)
    return jnp.where(lax.shift_right_logical(key, SLC_SHIFT) == blk, 1.0, 0.0).astype(BF16)


def _stack_heads(q_ref, k):
    return jnp.concatenate([q_ref[:, (k * NSA_GROUP + g) * HEAD_DIM:(k * NSA_GROUP + g + 1) * HEAD_DIM]
                            for g in range(NSA_GROUP)], axis=0).astype(BF16)


def _pad_keys(x, n):
    return jnp.concatenate([x, jnp.zeros((n - x.shape[0], x.shape[1]), x.dtype)], axis=0)


def _nsa_combine(gt_ref, oc_ref, o_ref, l_ref, acc_ref, tq):
    gt = jax.nn.sigmoid(gt_ref[:, 0:V7X_LANES])
    for h in range(NSA_HEADS):
        cols = slice(h * HEAD_DIM, (h + 1) * HEAD_DIM)
        o_slc = _normalised(l_ref, acc_ref, slice(h * tq, (h + 1) * tq))
        o_win = _normalised(l_ref, acc_ref, slice((NSA_HEADS + h) * tq, (NSA_HEADS + h + 1) * tq))
        c0 = G_NSA + 3 * h
        o_ref[:, cols] = gt[:, c0:c0 + 1] * oc_ref[:, cols] + gt[:, c0 + 1:c0 + 2] * o_slc + gt[:, c0 + 2:c0 + 3] * o_win


_NSA_SLOPES = _alibi(NSA_HEADS)
_DIFF_SLOPES = _alibi(DIFF_HEADS)


def _nsa_state(tq):
    rows = 2 * NSA_HEADS * tq
    return [pltpu.VMEM((rows, 1), F32), pltpu.VMEM((rows, 1), F32), pltpu.VMEM((rows, HEAD_DIM), F32)]


def _diff_state(tq):
    rows = 2 * DIFF_HEADS * tq
    return [pltpu.VMEM((rows, 1), F32), pltpu.VMEM((rows, 1), F32), pltpu.VMEM((rows, HEAD_DIM), F32)]


def _cmp_accumulate(rows_of, w_ref, o_ref):
    for c in range(4):
        for r in range(CMP_STRIDE):
            part = jnp.dot(rows_of(c, r).astype(BF16), w_ref[c // 2, r * HEAD_DIM:(r + 1) * HEAD_DIM, :],
                           preferred_element_type=F32)
            if r == 0:
                o_ref[0, c] = part
            else:
                o_ref[0, c] += part


def _cmp_partial_body(x0_ref, x1_ref, x2_ref, x3_ref, w_ref, o_ref):
    srcs = (x0_ref, x1_ref, x2_ref, x3_ref)
    per = x0_ref.shape[0] // CMP_STRIDE
    _cmp_accumulate(lambda c, r: srcs[c][pl.ds(r, per, stride=CMP_STRIDE), :], w_ref, o_ref)


def _cmp_partial_paged_body(pt_ref, *refs, pps):
    del pt_ref
    pages, w_ref, o_ref, kv_ref, slab_ref = refs[:pps], refs[pps], refs[pps + 1], refs[pps + 2], refs[pps + 3]
    for i, pg in enumerate(pages):
        for c in range(4):
            slab_ref[c, i * PAGE_SIZE:(i + 1) * PAGE_SIZE, :] = _page_rows(pg, c)
    per = pps * PAGE_SIZE // CMP_STRIDE
    _cmp_accumulate(lambda c, r: slab_ref[c, pl.ds(r, per, stride=CMP_STRIDE), :], w_ref, o_ref)
    for j in range(4):
        kv_ref[0, j] = _page_piece(pages, 4 + j)


def _cmp_finish_body(*refs, has_tail, n_out):
    if has_tail:
        p_ref, tail_ref, w1_ref, pos_ref, w2_ref, g_ref, o_ref = refs
    else:
        p_ref, w1_ref, pos_ref, w2_ref, g_ref, o_ref = refs
    c = pl.program_id(1)
    p = p_ref[0, 0]
    n = p.shape[0]
    w1 = w1_ref[0]
    pb = jnp.dot(pos_ref[0].astype(BF16), w1, preferred_element_type=F32)
    nxt = pltpu.roll(p[:, HEAD_DIM:], n - 1, 0)
    if has_tail:
        rid = lax.broadcasted_iota(jnp.int32, (V7X_SUBLANES, HEAD_DIM), 0)
        tail8 = jnp.zeros((V7X_SUBLANES, HEAD_DIM), F32)
        xt = tail_ref[...].astype(BF16)
        for r in range(tail_ref.shape[0]):
            d = jnp.dot(xt, w1[r * HEAD_DIM:(r + 1) * HEAD_DIM, HEAD_DIM:], preferred_element_type=F32)
            tail8 = tail8 + jnp.where(rid == r, d, 0.0)
        tail = jnp.sum(tail8, axis=0, keepdims=True)
    else:
        tail = jnp.zeros((1, HEAD_DIM), F32)
    last = lax.broadcasted_iota(jnp.int32, (n, HEAD_DIM), 0) == n - 1
    z = p[:, :HEAD_DIM] + pb[0:1, :HEAD_DIM] + jnp.where(last, tail, nxt) + pb[1:2, HEAD_DIM:]
    y = jnp.dot((z * jax.nn.sigmoid(z)).astype(BF16), w2_ref[0].astype(BF16), preferred_element_type=F32)
    y = jnp.where(c < NSA_KV_HEADS, _rms_rows(y, g_ref[...]), y)
    o_ref[0, 0, 0:n] = y
    if n_out > n:
        o_ref[0, 0, n:n_out] = jnp.zeros((n_out - n, HEAD_DIM), F32)


def _cmp_weights(lp):
    half = CMP_STRIDE * HEAD_DIM
    w1 = lp['nsa_cmp_w1']
    w1ab = jnp.concatenate([w1[:, :half], w1[:, half:]], axis=2).astype(BF16)
    pos = lp['nsa_cmp_pos']
    pos2 = jnp.concatenate([pos[:, :CMP_STRIDE].reshape(2, 1, half), pos[:, CMP_STRIDE:].reshape(2, 1, half),
                            jnp.zeros((2, V7X_SUBLANES - 2, half), F32)], axis=1)
    return w1ab, pos2


def _cmp_finish(p, tail_src, w1ab, pos2, lp, n_out, grp, u):
    b, _, n, _ = p.shape
    half = CMP_STRIDE * HEAD_DIM
    has_tail = tail_src is not None
    in_specs = [pl.BlockSpec((1, 1, n, 2 * HEAD_DIM), lambda i, c: (i, c, 0, 0))]
    args = [p]
    if has_tail:
        rb = grp.row0 // grp.t
        in_specs.append(pl.BlockSpec((grp.t, HEAD_DIM), lambda i, c: (rb + i, C_ROWS // HEAD_DIM + c)))
        args.append(u)
    in_specs += [
        pl.BlockSpec((1, half, 2 * HEAD_DIM), lambda i, c: (c // NSA_KV_HEADS, 0, 0)),
        pl.BlockSpec((1, V7X_SUBLANES, half), lambda i, c: (c // NSA_KV_HEADS, 0, 0)),
        pl.BlockSpec((1, HEAD_DIM, HEAD_DIM), lambda i, c: (c // NSA_KV_HEADS, 0, 0)),
        pl.BlockSpec((1, HEAD_DIM), lambda i, c: (0, 0)),
    ]
    args += [w1ab, pos2, lp['nsa_cmp_w2'], lp['nsa_k_norm'][0:1]]
    return pl.pallas_call(
        functools.partial(_cmp_finish_body, has_tail=has_tail, n_out=n_out),
        grid=(b, 4),
        in_specs=in_specs,
        out_specs=pl.BlockSpec((1, 1, n_out, HEAD_DIM), lambda i, c: (i, c, 0, 0)),
        out_shape=jax.ShapeDtypeStruct((b, 4, n_out, HEAD_DIM), F32),
        compiler_params=_cparams(("parallel", "parallel")),
        name="nsa_cmp_finish",
    )(*args)


def _compress_prompt(u, grp, lp):
    assert grp.past == 0 and grp.row0 == 0 and grp.t % (CMP_STRIDE * V7X_SUBLANES) == 0
    w1ab, pos2 = _cmp_weights(lp)
    n = grp.t // CMP_STRIDE
    group = lambda c: pl.BlockSpec((grp.t, HEAD_DIM), lambda i: (i, C_ROWS // HEAD_DIM + c))
    p = pl.pallas_call(
        _cmp_partial_body,
        grid=(grp.b,),
        in_specs=[group(0), group(1), group(2), group(3), pl.BlockSpec(w1ab.shape, lambda i: (0, 0, 0))],
        out_specs=pl.BlockSpec((1, 4, n, 2 * HEAD_DIM), lambda i: (i, 0, 0, 0)),
        out_shape=jax.ShapeDtypeStruct((grp.b, 4, n, 2 * HEAD_DIM), F32),
        compiler_params=_cparams(("parallel",)),
        name="nsa_cmp_partial",
    )(u, u, u, u, w1ab)
    return _cmp_finish(p, None, w1ab, pos2, lp, _round_up(n, V7X_LANES), grp, u), n - 1


def _pages_per_step(n_pages):
    return _pick_tile(n_pages, (16, 8, 4, 2, 1))


def _page_view(cache):
    assert cache.shape[3] * cache.shape[4] == ROW_GROUPS and cache.shape[5] == HEAD_DIM
    return cache.reshape(cache.shape[0], cache.shape[1], PAGE_SIZE * ROW_GROUPS, HEAD_DIM)


def _page_specs(layer, n_pages, pps):
    def spec(i):
        return pl.BlockSpec((1, 1, PAGE_SIZE * ROW_GROUPS, HEAD_DIM),
                            lambda b, c, pt: (layer, pt[b * n_pages + c * pps + i], 0, 0))
    return [spec(i) for i in range(pps)]


def _page_rows(page, j):
    return page[0, 0, pl.ds(j, PAGE_SIZE, stride=ROW_GROUPS), :]


def _page_piece(pages, j):
    return jnp.concatenate([_page_rows(pg, j).astype(BF16) for pg in pages], axis=0)


def _compress_sample(u, grp, lp, cache, page_table, layer):
    assert grp.past % PAGE_SIZE == 0 and grp.t < CMP_STRIDE
    w1ab, pos2 = _cmp_weights(lp)
    n_pages = grp.past // PAGE_SIZE
    pps = _pages_per_step(n_pages)
    per = pps * PAGE_SIZE // CMP_STRIDE
    n = grp.past // CMP_STRIDE
    p, slc_kv = pl.pallas_call(
        functools.partial(_cmp_partial_paged_body, pps=pps),
        grid_spec=pltpu.PrefetchScalarGridSpec(
            num_scalar_prefetch=1,
            grid=(grp.b, n_pages // pps),
            in_specs=_page_specs(layer, n_pages, pps) + [
                pl.BlockSpec(w1ab.shape, lambda b, c, pt: (0, 0, 0))],
            out_specs=[
                pl.BlockSpec((1, 4, per, 2 * HEAD_DIM), lambda b, c, pt: (b, 0, c, 0)),
                pl.BlockSpec((1, 4, pps * PAGE_SIZE, HEAD_DIM), lambda b, c, pt: (b, 0, c, 0)),
            ],
            scratch_shapes=[pltpu.VMEM((4, pps * PAGE_SIZE, HEAD_DIM), F32)],
        ),
        out_shape=[
            jax.ShapeDtypeStruct((grp.b, 4, n, 2 * HEAD_DIM), F32),
            jax.ShapeDtypeStruct((grp.b, 4, grp.past, HEAD_DIM), BF16),
        ],
        compiler_params=_cparams(("parallel", "arbitrary")),
        name="nsa_cmp_partial_paged",
    )(page_table, *([cache] * pps), w1ab)
    return _cmp_finish(p, u, w1ab, pos2, lp, _round_up(n + 1, V7X_LANES), grp, u), n, slc_kv


def _cmp_select_body(q_ref, kv_ref, o_ref, sel_ref, *, tq, q_off, n_cmp, n_slc, n_sel):
    qi = pl.program_id(1)
    n_cpad = kv_ref.shape[2]
    n_spad = sel_ref.shape[1] // NSA_KV_HEADS
    q_lo = q_off + qi * tq
    qpos = q_lo + lax.broadcasted_iota(jnp.int32, (tq, n_cpad), 0)
    ends = lax.broadcasted_iota(jnp.int32, (tq, n_cpad), 1) * CMP_STRIDE + (CMP_BLOCK - 1)
    d = qpos - ends
    mask = d >= 0
    df = d.astype(F32)
    c_lo = lax.broadcasted_iota(jnp.int32, (n_cpad, n_spad), 0) * CMP_STRIDE
    j_lo = lax.broadcasted_iota(jnp.int32, (n_cpad, n_spad), 1) * SLC_BLOCK
    cover = jnp.where(c_lo < j_lo + SLC_BLOCK, jnp.where(c_lo + CMP_BLOCK > j_lo, 1.0, 0.0), 0.0)
    cover = jnp.where(c_lo < n_cmp * CMP_STRIDE, cover, 0.0)
    qp = q_lo + lax.broadcasted_iota(jnp.int32, (tq, n_spad), 0)
    j = lax.broadcasted_iota(jnp.int32, (tq, n_spad), 1)
    cur = lax.shift_right_logical(qp, SLC_SHIFT)
    bonus = FORCE_BONUS * jnp.where(j == 0, 1.0, jnp.where(j == cur, 1.0, jnp.where(j == cur - 1, 1.0, 0.0)))
    valid = j * SLC_BLOCK <= qp
    for k in range(NSA_KV_HEADS):
        q = jnp.concatenate([q_ref[:, (k * NSA_GROUP + g) * HEAD_DIM:(k * NSA_GROUP + g + 1) * HEAD_DIM]
                             for g in range(NSA_GROUP)], axis=0)
        s = lax.dot_general(q, kv_ref[0, k], NT, precision=HIGHEST, preferred_element_type=F32)
        vc = kv_ref[0, NSA_KV_HEADS + k].astype(BF16)
        psum = jnp.zeros((tq, n_cpad), F32)
        for g in range(NSA_GROUP):
            h = k * NSA_GROUP + g
            sg = jnp.where(mask, s[g * tq:(g + 1) * tq] - (_NSA_SLOPES[h] * LOG2E) * df, NEG_INF)
            m = jnp.max(sg, axis=-1, keepdims=True)
            p = jnp.where(mask, jnp.exp2(sg - m), 0.0)
            p = p / jnp.maximum(jnp.sum(p, axis=-1, keepdims=True), 1e-30)
            o_ref[:, h * HEAD_DIM:(h + 1) * HEAD_DIM] = jnp.dot(p.astype(BF16), vc, preferred_element_type=F32)
            psum = psum + p
        imp = jnp.dot(psum, cover, precision=HIGHEST, preferred_element_type=F32)
        score = jnp.where(valid, imp + bonus, NEG_INF)
        cnt = jnp.zeros((tq, n_spad), F32)
        for i in range(n_slc):
            ci = score[:, i:i + 1]
            cnt = cnt + jnp.where(ci > score, 1.0, jnp.where(ci == score, jnp.where(j > i, 1.0, 0.0), 0.0))
        sel_ref[:, k * n_spad:(k + 1) * n_spad] = jnp.where(cnt < n_sel, jnp.where(j < n_slc, 0.0, NEG_INF), NEG_INF)


def _cmp_select(u, grp, summaries, n_cmp):
    n_slc = -(-(grp.past + grp.t) // SLC_BLOCK)
    n_spad = _round_up(n_slc, V7X_LANES)
    n_cpad = summaries.shape[2]
    tq = _pick_tile(grp.t, (256, 128, 64, 32, 16, 8))
    nq = grp.t // tq
    rb = grp.row0 // tq
    body = functools.partial(_cmp_select_body, tq=tq, q_off=grp.past, n_cmp=n_cmp, n_slc=n_slc,
                             n_sel=min(SLC_TOPN, n_slc))
    return pl.pallas_call(
        body,
        grid=(grp.b, nq),
        in_specs=[
            pl.BlockSpec((tq, NSA_W), lambda b, i: (rb + b * nq + i, 0)),
            pl.BlockSpec((1, 4, n_cpad, HEAD_DIM), lambda b, i: (b, 0, 0, 0)),
        ],
        out_specs=[
            pl.BlockSpec((tq, NSA_W), lambda b, i: (b * nq + i, 0)),
            pl.BlockSpec((tq, NSA_KV_HEADS * n_spad), lambda b, i: (b * nq + i, 0)),
        ],
        out_shape=[
            jax.ShapeDtypeStruct((grp.b * grp.t, NSA_W), F32),
            jax.ShapeDtypeStruct((grp.b * grp.t, NSA_KV_HEADS * n_spad), F32),
        ],
        compiler_params=_cparams(("parallel", "parallel")),
        name="nsa_cmp_select",
    )(u, summaries)


def _nsa_prompt_body(q_ref, ks_ref, kw_ref, sel_ref, oc_ref, gt_ref, o_ref, m_ref, l_ref, acc_ref, *, tq, tk):
    qi = pl.program_id(1)
    kj = pl.program_id(2)
    n_spad = sel_ref.shape[1] // NSA_KV_HEADS

    @pl.when(kj == 0)
    def _():
        _init_state(m_ref, l_ref, acc_ref)

    q_lo = qi * tq
    k_lo = kj * tk
    slc_on = k_lo <= q_lo + (tq - 1)
    win_on = jnp.logical_and(slc_on, k_lo + (tk - 1) >= q_lo - WINDOW)

    krel = _key_offsets(tk, k_lo - q_lo)

    @pl.when(slc_on)
    def _():
        causal = _mask_bias(_distance(tq, tk, q_lo - k_lo) >= 0)
        expand = _block_expand(n_spad, tk, k_lo)
        kv = ks_ref[...].astype(BF16)
        for k in range(NSA_KV_HEADS):
            s = lax.dot_general(_stack_heads(q_ref, k), kv[:, k * HEAD_DIM:(k + 1) * HEAD_DIM], NT,
                                preferred_element_type=F32)
            chosen = jnp.dot(sel_ref[:, k * n_spad:(k + 1) * n_spad].astype(BF16), expand,
                             preferred_element_type=F32)
            _online_update(s, causal + chosen, krel, _NSA_SLOPES[k * NSA_GROUP:(k + 1) * NSA_GROUP],
                           kv[:, (NSA_KV_HEADS + k) * HEAD_DIM:(NSA_KV_HEADS + k + 1) * HEAD_DIM],
                           m_ref, l_ref, acc_ref, k * NSA_GROUP * tq, tq)

    @pl.when(win_on)
    def _():
        d = _distance(tq, tk, q_lo - k_lo)
        bias = _mask_bias(jnp.logical_and(d >= 0, d <= WINDOW))
        kv = kw_ref[...].astype(BF16)
        for k in range(NSA_KV_HEADS):
            s = lax.dot_general(_stack_heads(q_ref, k), kv[:, k * HEAD_DIM:(k + 1) * HEAD_DIM], NT,
                                preferred_element_type=F32)
            _online_update(s, bias, krel, _NSA_SLOPES[k * NSA_GROUP:(k + 1) * NSA_GROUP],
                           kv[:, (NSA_KV_HEADS + k) * HEAD_DIM:(NSA_KV_HEADS + k + 1) * HEAD_DIM],
                           m_ref, l_ref, acc_ref, (NSA_KV_HEADS + k) * NSA_GROUP * tq, tq)

    @pl.when(kj == pl.num_programs(2) - 1)
    def _():
        _nsa_combine(gt_ref, oc_ref, o_ref, l_ref, acc_ref, tq)


def _nsa_prompt(u, grp, o_cmp, sel):
    assert grp.past == 0 and grp.row0 == 0
    t = grp.t
    tq = _pick_tile(t, (256, 128, 64, 32, 16, 8))
    tk = _pick_tile(t, (512, 256, 128))
    nq, nk = t // tq, t // tk
    kvw = 2 * NSA_KV_W

    def last_tile(i):
        return (i * tq + (tq - 1)) // tk

    def slc_rows(b, i, j):
        return (b * nk + jnp.minimum(j, last_tile(i)), (C_ROWS + kvw) // kvw)

    def win_rows(b, i, j):
        first = jnp.maximum(i * tq - WINDOW, 0) // tk
        return (b * nk + jnp.clip(j, first, last_tile(i)), C_WIN // kvw)

    return pl.pallas_call(
        functools.partial(_nsa_prompt_body, tq=tq, tk=tk),
        grid=(grp.b, nq, nk),
        in_specs=[
            pl.BlockSpec((tq, NSA_W), lambda b, i, j: (b * nq + i, 0)),
            pl.BlockSpec((tk, kvw), slc_rows),
            pl.BlockSpec((tk, kvw), win_rows),
            pl.BlockSpec((tq, sel.shape[1]), lambda b, i, j: (b * nq + i, 0)),
            pl.BlockSpec((tq, NSA_W), lambda b, i, j: (b * nq + i, 0)),
            pl.BlockSpec((tq, U_TILE), lambda b, i, j: (b * nq + i, C_SMALL // U_TILE)),
        ],
        out_specs=pl.BlockSpec((tq, NSA_W), lambda b, i, j: (b * nq + i, 0)),
        out_shape=jax.ShapeDtypeStruct((grp.b * t, NSA_W), F32),
        scratch_shapes=_nsa_state(tq),
        compiler_params=_cparams(("parallel", "parallel", "arbitrary")),
        name="nsa_attn_prompt",
    )(u, u, u, sel, o_cmp, u)


def _nsa_sample_body(q_ref, kv_ref, knew_ref, wst_ref, wnew_ref, sel_ref, oc_ref, gt_ref, o_ref, m_ref, l_ref, acc_ref,
                     *, past, n_win):
    c = pl.program_id(1)
    tq = q_ref.shape[0]
    n_spad = sel_ref.shape[1] // NSA_KV_HEADS
    n = kv_ref.shape[2]

    @pl.when(c == 0)
    def _():
        _init_state(m_ref, l_ref, acc_ref)

    def attend(keys_of, vals_of, bias_of, krel, branch):
        for k in range(NSA_KV_HEADS):
            s = lax.dot_general(_stack_heads(q_ref, k), keys_of(k), NT, preferred_element_type=F32)
            _online_update(s, bias_of(k), krel, _NSA_SLOPES[k * NSA_GROUP:(k + 1) * NSA_GROUP], vals_of(k),
                           m_ref, l_ref, acc_ref, (branch * NSA_KV_HEADS + k) * NSA_GROUP * tq, tq)

    def slab(ref, j):
        return ref[:, j * HEAD_DIM:(j + 1) * HEAD_DIM]

    first_key = c * n
    expand = _block_expand(n_spad, n, first_key)
    attend(lambda k: kv_ref[0, k], lambda k: kv_ref[0, NSA_KV_HEADS + k],
           lambda k: jnp.dot(sel_ref[:, k * n_spad:(k + 1) * n_spad].astype(BF16), expand,
                             preferred_element_type=F32),
           _key_offsets(n, first_key - past), 0)

    @pl.when(c == pl.num_programs(1) - 1)
    def _():
        npad = V7X_LANES
        real = lax.broadcasted_iota(jnp.int32, (tq, npad), 1) < tq
        causal = _mask_bias(jnp.logical_and(_distance(tq, npad, 0) >= 0, real))
        lane = lax.broadcasted_iota(jnp.int32, (tq, n_spad), 1)

        def chosen_new(k):
            col = jnp.where(lane == past // SLC_BLOCK, sel_ref[:, k * n_spad:(k + 1) * n_spad], 0.0)
            return causal + jnp.sum(col, axis=-1, keepdims=True)

        knew = _pad_keys(knew_ref[...], npad).astype(BF16)
        attend(lambda k: slab(knew, k), lambda k: slab(knew, NSA_KV_HEADS + k), chosen_new, _key_offsets(npad, 0), 0)
        wbuf = wst_ref[0, 0].astype(BF16)
        in_window = _mask_bias(_distance(tq, n_win, n_win) <= WINDOW)
        attend(lambda k: slab(wbuf, k), lambda k: slab(wbuf, NSA_KV_HEADS + k), lambda k: in_window,
               _key_offsets(n_win, -n_win), 1)
        wnew = _pad_keys(wnew_ref[...], npad).astype(BF16)
        attend(lambda k: slab(wnew, k), lambda k: slab(wnew, NSA_KV_HEADS + k), lambda k: causal,
               _key_offsets(npad, 0), 1)
        _nsa_combine(gt_ref, oc_ref, o_ref, l_ref, acc_ref, tq)


def _nsa_sample(u, grp, o_cmp, sel, slc_kv, win_state, layer):
    t = grp.t
    n_win = win_state.shape[2]
    assert n_win == WINDOW and grp.past % SLC_BLOCK == 0 and t <= SLC_BLOCK and grp.row0 % t == 0
    n = _pick_tile(grp.past, (2048, 1024, 512, 256, 128))
    kvw = 2 * NSA_KV_W
    rb = grp.row0 // t
    tok = lambda width, col: pl.BlockSpec((t, width), lambda b, c: (rb + b, col))
    own = lambda width: pl.BlockSpec((t, width), lambda b, c: (b, 0))
    return pl.pallas_call(
        functools.partial(_nsa_sample_body, past=grp.past, n_win=n_win),
        grid=(grp.b, grp.past // n),
        in_specs=[
            tok(NSA_W, 0),
            pl.BlockSpec((1, 4, n, HEAD_DIM), lambda b, c: (b, 0, c, 0)),
            tok(kvw, (C_ROWS + kvw) // kvw),
            pl.BlockSpec((1, 1, n_win, kvw), lambda b, c: (layer, b, 0, 0)),
            tok(kvw, C_WIN // kvw),
            own(sel.shape[1]),
            own(NSA_W),
            tok(U_TILE, C_SMALL // U_TILE),
        ],
        out_specs=own(NSA_W),
        out_shape=jax.ShapeDtypeStruct((grp.b * t, NSA_W), F32),
        scratch_shapes=_nsa_state(t),
        compiler_params=_cparams(("parallel", "arbitrary")),
        name="nsa_attn_cached",
    )(u, slc_kv, u, win_state, u, sel, o_cmp, u)


def _diff_update(q_ref, group_of, bias, krel, m_ref, l_ref, acc_ref):
    tq = q_ref.shape[0]
    low = lax.broadcasted_iota(jnp.int32, (tq, HEAD_DIM), 1) < DIFF_QK_DIM
    scores, shifts = [], []
    for h in range(DIFF_HEADS):
        qh = q_ref[:, h * HEAD_DIM:(h + 1) * HEAD_DIM]
        q2 = jnp.concatenate([jnp.where(low, qh, 0.0), jnp.where(low, 0.0, qh)], axis=0).astype(BF16)
        scores.append(lax.dot_general(q2, group_of(h), NT, preferred_element_type=F32))
        shift = (_DIFF_SLOPES[h] * LOG2E) * krel
        shift = jnp.broadcast_to(shift, (tq, shift.shape[1])) if bias is None else bias + shift
        shifts += [shift, shift]
    p, alpha = _online_softmax(jnp.concatenate(scores, axis=0), jnp.concatenate(shifts, axis=0), m_ref, l_ref,
                               slice(0, 2 * DIFF_HEADS * tq))
    for h in range(DIFF_HEADS):
        rows = slice(2 * h * tq, 2 * (h + 1) * tq)
        acc_ref[rows] = alpha[rows] * acc_ref[rows] + jnp.dot(p[rows], group_of(DIFF_HEADS + h),
                                                               preferred_element_type=F32)


def _diff_finish(lam_ref, g_ref, o_ref, l_ref, acc_ref, post):
    tq = o_ref.shape[0]
    lam = lam_ref[0]
    for h in range(DIFF_HEADS):
        o1 = _normalised(l_ref, acc_ref, slice(2 * h * tq, (2 * h + 1) * tq))
        o2 = _normalised(l_ref, acc_ref, slice((2 * h + 1) * tq, (2 * h + 2) * tq))
        o_ref[:, h * HEAD_DIM:(h + 1) * HEAD_DIM] = _rms_rows(o1 - lam * o2, g_ref[...]) * post


def _diff_prompt_body(lam_ref, q_ref, kv_ref, g_ref, o_ref, m_ref, l_ref, acc_ref, *, tq, tk, post):
    qi = pl.program_id(1)
    kj = pl.program_id(2)

    @pl.when(kj == 0)
    def _():
        _init_state(m_ref, l_ref, acc_ref)

    @pl.when(kj * tk <= qi * tq + (tq - 1))
    def _():
        kv = kv_ref[...].astype(BF16)
        _diff_update(q_ref, lambda j: kv[:, j * HEAD_DIM:(j + 1) * HEAD_DIM],
                     _mask_bias(_distance(tq, tk, qi * tq - kj * tk) >= 0), _key_offsets(tk, kj * tk - qi * tq),
                     m_ref, l_ref, acc_ref)

    @pl.when(kj == pl.num_programs(2) - 1)
    def _():
        _diff_finish(lam_ref, g_ref, o_ref, l_ref, acc_ref, post)


def _diff_prompt(u, grp, lam, gain, post):
    assert grp.past == 0 and grp.row0 == 0
    t = grp.t
    tq = _pick_tile(t, (256, 128, 64, 32, 16, 8))
    tk = _pick_tile(t, (512, 256, 128))
    nq, nk = t // tq, t // tk
    return pl.pallas_call(
        functools.partial(_diff_prompt_body, tq=tq, tk=tk, post=post),
        grid=(grp.b, nq, nk),
        in_specs=[
            pl.BlockSpec(memory_space=pltpu.SMEM),
            pl.BlockSpec((tq, DIFF_W), lambda b, i, j: (b * nq + i, C_DQ // DIFF_W)),
            pl.BlockSpec((tk, 2 * DIFF_W),
                         lambda b, i, j: (b * nk + jnp.minimum(j, (i * tq + (tq - 1)) // tk), C_DROWS // (2 * DIFF_W))),
            pl.BlockSpec((1, HEAD_DIM), lambda b, i, j: (0, 0)),
        ],
        out_specs=pl.BlockSpec((tq, DIFF_W), lambda b, i, j: (b * nq + i, 0)),
        out_shape=jax.ShapeDtypeStruct((grp.b * t, DIFF_W), F32),
        scratch_shapes=_diff_state(tq),
        compiler_params=_cparams(("parallel", "parallel", "arbitrary")),
        name="diff_attn_prompt",
    )(lam, u, u, gain)


def _diff_sample_body(pt_ref, lam_ref, q_ref, *rest, pps, past, post):
    pages = rest[:pps]
    new_ref, g_ref, o_ref, m_ref, l_ref, acc_ref = rest[pps:]
    del pt_ref
    c = pl.program_id(1)
    tq = q_ref.shape[0]
    n = pps * PAGE_SIZE

    @pl.when(c == 0)
    def _():
        _init_state(m_ref, l_ref, acc_ref)

    _diff_update(q_ref, lambda j: _page_piece(pages, j), None, _key_offsets(n, c * n - past), m_ref, l_ref, acc_ref)

    @pl.when(c == pl.num_programs(1) - 1)
    def _():
        npad = V7X_LANES
        keep = jnp.logical_and(_distance(tq, npad, 0) >= 0, lax.broadcasted_iota(jnp.int32, (tq, npad), 1) < tq)
        new = _pad_keys(new_ref[...], npad).astype(BF16)
        _diff_update(q_ref, lambda j: new[:, j * HEAD_DIM:(j + 1) * HEAD_DIM], _mask_bias(keep),
                     _key_offsets(npad, 0), m_ref, l_ref, acc_ref)
        _diff_finish(lam_ref, g_ref, o_ref, l_ref, acc_ref, post)


def _diff_sample(u, grp, lam, gain, post, cache, page_table, layer):
    t = grp.t
    assert grp.past % PAGE_SIZE == 0 and grp.row0 % t == 0
    n_pages = grp.past // PAGE_SIZE
    pps = _pages_per_step(n_pages)
    rb = grp.row0 // t
    return pl.pallas_call(
        functools.partial(_diff_sample_body, pps=pps, past=grp.past, post=post),
        grid_spec=pltpu.PrefetchScalarGridSpec(
            num_scalar_prefetch=1,
            grid=(grp.b, n_pages // pps),
            in_specs=[
                pl.BlockSpec(memory_space=pltpu.SMEM),
                pl.BlockSpec((t, DIFF_W), lambda b, c, pt: (rb + b, C_DQ // DIFF_W)),
            ] + _page_specs(layer, n_pages, pps) + [
                pl.BlockSpec((t, 2 * DIFF_W), lambda b, c, pt: (rb + b, C_DROWS // (2 * DIFF_W))),
                pl.BlockSpec((1, HEAD_DIM), lambda b, c, pt: (0, 0)),
            ],
            out_specs=pl.BlockSpec((t, DIFF_W), lambda b, c, pt: (b, 0)),
            scratch_shapes=_diff_state(t),
        ),
        out_shape=jax.ShapeDtypeStruct((grp.b * t, DIFF_W), F32),
        compiler_params=_cparams(("parallel", "arbitrary")),
        name="diff_attn_paged",
    )(page_table, lam, u, *([cache] * pps), u, gain)


def _mlstm_body(xq_ref, xk_ref, v_ref, og_ref, bq_ref, bk_ref, wq_ref, wk_ref, cq_ref, ck_ref, li_ref, lf_ref,
                c0_ref, n0_ref, m0_ref, g_ref, h_ref, c_ref, n_ref, m_ref, q_s, k_s, *, lc, nc):
    t = xq_ref.shape[0]
    row = lax.broadcasted_iota(jnp.int32, (lc, lc), 0)
    col = lax.broadcasted_iota(jnp.int32, (lc, lc), 1)
    tril = row >= col
    triu = row <= col
    eye = row == col
    gain = g_ref[...]

    def conv(x_ref, buf_ref, w_ref, b_ref):
        ext = jnp.concatenate([buf_ref[0], x_ref[...]], axis=0)
        y = b_ref[...]
        for i in range(CONV_WIDTH):
            shifted = ext if i == CONV_WIDTH - 1 else pltpu.roll(ext, CONV_WIDTH - 1 - i, 0)
            y = y + shifted[V7X_SUBLANES:V7X_SUBLANES + t] * w_ref[i:i + 1, :]
        return y * jax.nn.sigmoid(y)

    q_s[...] = conv(xq_ref, bq_ref, wq_ref, cq_ref) * MLSTM_DIM ** -0.5
    k_s[...] = conv(xk_ref, bk_ref, wk_ref, ck_ref)

    def to_col(r):
        return jnp.sum(jnp.where(eye, jnp.broadcast_to(r, (lc, lc)), 0.0), axis=1, keepdims=True)

    def step(ci, carry):
        c, n, m = carry
        rows = pl.ds(ci * lc if nc == 1 else pl.multiple_of(ci * lc, lc), lc)
        qc = q_s[rows, :]
        kc = k_s[rows, :]
        vc = v_ref[rows, :]
        li = li_ref[0, pl.ds(ci, 1), :]
        lf = lf_ref[0, pl.ds(ci, 1), :]
        lf_b = jnp.broadcast_to(lf, (lc, lc))
        bcum_c = jnp.sum(jnp.where(tril, lf_b, 0.0), axis=1, keepdims=True)
        bcum_r = jnp.sum(jnp.where(triu, jnp.broadcast_to(to_col(lf), (lc, lc)), 0.0), axis=0, keepdims=True)
        dmat = jnp.where(tril, bcum_c - bcum_r + li, NEG_INF)
        inter = bcum_c + m
        mj = jnp.maximum(inter, jnp.max(dmat, axis=1, keepdims=True))
        wts = jnp.exp(dmat - mj)
        a = jnp.exp(inter - mj)
        sqk = lax.dot_general(qc, kc, NT, precision=HIGHEST, preferred_element_type=F32) * wts
        num = a * jnp.dot(qc, c, precision=HIGHEST, preferred_element_type=F32) + \
            jnp.dot(sqk, vc, precision=HIGHEST, preferred_element_type=F32)
        den = a * jnp.sum(qc * n, axis=1, keepdims=True) + jnp.sum(sqk, axis=1, keepdims=True)
        h = num / jnp.maximum(jnp.abs(den), jnp.exp(-mj))
        h_ref[rows, :] = _rms_rows(h, gain) * jax.nn.sigmoid(og_ref[rows, :])
        b_last = jnp.sum(lf, axis=1, keepdims=True)
        gl = b_last - bcum_r + li
        m_new = jnp.maximum(b_last + m, jnp.max(gl, axis=1, keepdims=True))
        decay = jnp.exp(b_last + m - m_new)
        kw = kc * to_col(jnp.exp(gl - m_new))
        c_new = decay * c + lax.dot_general(kw, vc, (((0,), (0,)), ((), ())), precision=HIGHEST,
                                            preferred_element_type=F32)
        n_new = decay * n + jnp.sum(kw, axis=0, keepdims=True)
        return c_new, n_new, m_new

    init = (c0_ref[0], n0_ref[0], m0_ref[0][:, 0:1])
    c, n, m = step(0, init) if nc == 1 else lax.fori_loop(0, nc, step, init, unroll=2)
    c_ref[0] = c
    n_ref[0] = n
    m_ref[0] = jnp.broadcast_to(m, (1, MLSTM_DIM))


def _mlstm(u, grp, lp, c0, n0, m0, conv_buf):
    b, t = grp.b, grp.t
    assert grp.row0 % t == 0 and t % V7X_SUBLANES == 0
    bh = b * MLSTM_HEADS
    hd = MLSTM_DIM
    lc = MLSTM_CHUNK if t % MLSTM_CHUNK == 0 else t
    nc = t // lc
    rb = grp.row0 // t
    gates = u[grp.row0:grp.row0 + b * t, C_SMALL:C_SMALL + U_TILE].reshape(b, t, U_TILE)
    log_i = gates[:, :, G_MI:G_MI + MLSTM_HEADS] + lp['ml_gate_b'][0]
    log_f = jax.nn.log_sigmoid(gates[:, :, G_MF:G_MF + MLSTM_HEADS] + lp['ml_gate_b'][1])
    log_i = log_i.transpose(0, 2, 1).reshape(bh, nc, lc)
    log_f = log_f.transpose(0, 2, 1).reshape(bh, nc, lc)
    buf8 = jnp.pad(conv_buf, ((0, 0), (V7X_SUBLANES - (CONV_WIDTH - 1), 0), (0, 0)))
    w8 = jnp.pad(lp['ml_conv_w'], ((0, V7X_SUBLANES - CONV_WIDTH), (0, 0)))
    cb = lp['ml_conv_b'].reshape(1, 2 * MLSTM_W)
    tok = lambda col0: pl.BlockSpec((t, hd), lambda i, h: (rb + i, col0 // hd + h))
    buf = lambda off: pl.BlockSpec((1, V7X_SUBLANES, hd), lambda i, h: (i, 0, off + h))
    wsp = lambda off: pl.BlockSpec((V7X_SUBLANES, hd), lambda i, h: (0, off + h))
    bsp = lambda off: pl.BlockSpec((1, hd), lambda i, h: (0, off + h))
    gate = pl.BlockSpec((1, nc, lc), lambda i, h: (i * MLSTM_HEADS + h, 0, 0))
    vec = pl.BlockSpec((1, 1, hd), lambda i, h: (i * MLSTM_HEADS + h, 0, 0))
    mat = pl.BlockSpec((1, hd, hd), lambda i, h: (i * MLSTM_HEADS + h, 0, 0))
    h, c, n, m = pl.pallas_call(
        functools.partial(_mlstm_body, lc=lc, nc=nc),
        grid=(b, MLSTM_HEADS),
        in_specs=[tok(C_MQ), tok(C_MK), tok(C_MV), tok(C_MO), buf(0), buf(MLSTM_HEADS), wsp(0), wsp(MLSTM_HEADS),
                  bsp(0), bsp(MLSTM_HEADS), gate, gate, mat, vec, vec, pl.BlockSpec((1, hd), lambda i, h: (0, 0))],
        out_specs=[pl.BlockSpec((t, hd), lambda i, h: (i, h)), mat, vec, vec],
        out_shape=[
            jax.ShapeDtypeStruct((b * t, MLSTM_W), F32),
            jax.ShapeDtypeStruct((bh, hd, hd), F32),
            jax.ShapeDtypeStruct((bh, 1, hd), F32),
            jax.ShapeDtypeStruct((bh, 1, hd), F32),
        ],
        scratch_shapes=[pltpu.VMEM((t, hd), F32), pltpu.VMEM((t, hd), F32)],
        compiler_params=_cparams(("parallel", "parallel")),
        name="mlstm",
    )(u, u, u, u, buf8, buf8, w8, w8, cb, cb, log_i, log_f, c0.reshape(bh, hd, hd), n0.reshape(bh, 1, hd),
      jnp.broadcast_to(m0.reshape(bh, 1, 1), (bh, 1, hd)), lp['ml_out_norm'].reshape(1, hd))
    return (h, c.reshape(b, MLSTM_HEADS, hd, hd), n.reshape(b, MLSTM_HEADS, hd), m[:, 0, 0].reshape(b, MLSTM_HEADS))


def _pack_w_in(w_in):
    cuts = np.cumsum((0,) + IN_SPLITS)
    cols = [w_in[..., cuts[i]:cuts[i + 1]] for i in range(len(IN_SPLITS))]
    packed = jnp.concatenate([cols[i] for i in _MAIN_ORDER] + [cols[i] for i in _SMALL_ORDER], axis=-1)
    pad = [(0, 0)] * (packed.ndim - 1) + [(0, U_COLS - packed.shape[-1])]
    return jnp.pad(packed, pad).astype(BF16)


def _u_gain(lp):
    ones = lambda n: jnp.ones((n,), F32)
    g = jnp.concatenate([
        jnp.tile(lp['nsa_q_norm'], NSA_HEADS) * (HEAD_DIM ** -0.5 * LOG2E),
        ones(2 * NSA_KV_W), jnp.tile(lp['nsa_k_norm'][1], NSA_KV_HEADS), ones(NSA_KV_W),
        jnp.tile(lp['nsa_k_norm'][2], NSA_KV_HEADS), ones(NSA_KV_W),
        jnp.tile(lp['diff_q_norm'], 2 * DIFF_HEADS) * (DIFF_QK_DIM ** -0.5 * LOG2E),
        jnp.tile(lp['diff_k_norm'], 2 * DIFF_HEADS),
        ones(U_COLS - (C_DROWS + DIFF_W))])
    return g.reshape(1, U_COLS)


def _group_rows(u, grp, col0, width):
    return u[grp.row0:grp.row0 + grp.b * grp.t, col0:col0 + width].reshape(grp.b, grp.t, width)


def _tail_rows(buf, new, keep):
    n_new = min(keep, new.shape[1])
    parts = [buf[:, buf.shape[1] - (keep - n_new):], new[:, new.shape[1] - n_new:]] if keep > n_new else \
        [new[:, new.shape[1] - n_new:]]
    return jnp.concatenate(parts, axis=1)


def _trunk(x_prompt, x_sample, cache_nsa, cache_diff, state_win, state_ml_c, state_ml_n, state_ml_m, state_ml_conv,
           page_table, p):
    bp, tp, d = x_prompt.shape
    bs, ts, _ = x_sample.shape
    depth = p['w_in'].shape[0]
    past = page_table.shape[1] * PAGE_SIZE
    gp = _Group(bp, tp, 0, 0)
    gs = _Group(bs, ts, bp * tp, past)
    n_p = bp * tp
    w_in = _pack_w_in(p['w_in'])
    w_out = p['w_out'].astype(BF16)
    w13 = (p['ffn1_w13'].astype(BF16), p['ffn2_w13'].astype(BF16))
    w2 = (p['ffn1_w2'].astype(BF16), p['ffn2_w2'].astype(BF16))
    per_layer = ('mix_norm', 'nsa_q_norm', 'nsa_k_norm', 'nsa_cmp_pos', 'nsa_cmp_w1', 'nsa_cmp_w2', 'diff_q_norm',
                 'diff_k_norm', 'diff_lambda', 'diff_out_norm', 'ml_conv_w', 'ml_conv_b', 'ml_gate_b', 'ml_out_norm')
    cache_nsa = _page_view(cache_nsa)
    cache_diff = _page_view(cache_diff)
    win_state = state_win.reshape(state_win.shape[:3] + (2 * NSA_KV_W,))
    pt = page_table.reshape(-1)
    zeros = lambda *s: jnp.zeros(s, F32)
    p_state = (zeros(bp, MLSTM_HEADS, MLSTM_DIM, MLSTM_DIM), zeros(bp, MLSTM_HEADS, MLSTM_DIM),
               zeros(bp, MLSTM_HEADS), zeros(bp, CONV_WIDTH - 1, 2 * MLSTM_W))
    p_win0 = zeros(bp, 0, 2 * NSA_KV_W)

    x = jnp.concatenate([x_prompt.reshape(n_p, d), x_sample.reshape(bs * ts, d)], axis=0)
    outs = []
    for l in range(depth):
        lp = {name: p[name][l] for name in per_layer}
        lam_init = 0.8 - 0.6 * math.exp(-0.3 * l)
        lam_p = lp['diff_lambda']
        lam = (jnp.exp(jnp.sum(lam_p[0] * lam_p[1])) - jnp.exp(jnp.sum(lam_p[2] * lam_p[3])) + lam_init).reshape(1)
        dgain = lp['diff_out_norm'].reshape(1, HEAD_DIM)
        h = _ffn(x, p['ffn1_norm'][l], w13[0], w2[0], l)
        u = _inproj(h, lp['mix_norm'], w_in, _u_gain(lp), l)

        summ_p, n_cmp_p = _compress_prompt(u, gp, lp)
        oc_p, sel_p = _cmp_select(u, gp, summ_p, n_cmp_p)
        nsa_p = _nsa_prompt(u, gp, oc_p, sel_p)
        diff_p = _diff_prompt(u, gp, lam, dgain, 1.0 - lam_init)
        ml_p, c_p, nn_p, m_p = _mlstm(u, gp, lp, *p_state)

        summ_s, n_cmp_s, slc_kv = _compress_sample(u, gs, lp, cache_nsa, pt, l)
        oc_s, sel_s = _cmp_select(u, gs, summ_s, n_cmp_s)
        nsa_s = _nsa_sample(u, gs, oc_s, sel_s, slc_kv, win_state, l)
        diff_s = _diff_sample(u, gs, lam, dgain, 1.0 - lam_init, cache_diff, pt, l)
        ml_s, c_s, nn_s, m_s = _mlstm(u, gs, lp, state_ml_c[l], state_ml_n[l], state_ml_m[l], state_ml_conv[l])

        h = _outproj(h, jnp.concatenate([nsa_p, nsa_s]), jnp.concatenate([diff_p, diff_s]),
                     jnp.concatenate([ml_p, ml_s]), w_out, l)
        x = _ffn(h, p['ffn2_norm'][l], w13[1], w2[1], l)

        def rows(grp, col0, shape):
            return _group_rows(u, grp, col0, int(np.prod(shape))).reshape((grp.b, grp.t) + shape)

        nsa_shape, diff_shape = (4, NSA_KV_HEADS, HEAD_DIM), (2, DIFF_HEADS, HEAD_DIM)
        win_shape = (2, NSA_KV_HEADS, HEAD_DIM)
        win_rows_p = _tail_rows(p_win0, _group_rows(u, gp, C_WIN, 2 * NSA_KV_W), min(WINDOW, tp))
        win_rows_s = _tail_rows(win_state[l], _group_rows(u, gs, C_WIN, 2 * NSA_KV_W),
                                min(WINDOW, win_state.shape[2] + ts))
        conv_p = _tail_rows(p_state[3], _group_rows(u, gp, C_MQ, 2 * MLSTM_W), CONV_WIDTH - 1)
        conv_s = _tail_rows(state_ml_conv[l], _group_rows(u, gs, C_MQ, 2 * MLSTM_W), CONV_WIDTH - 1)
        outs.append((rows(gp, C_ROWS, nsa_shape), rows(gs, C_ROWS, nsa_shape),
                     rows(gp, C_DROWS, diff_shape), rows(gs, C_DROWS, diff_shape),
                     win_rows_p.reshape(win_rows_p.shape[:2] + win_shape),
                     win_rows_s.reshape(win_rows_s.shape[:2] + win_shape),
                     c_p, c_s, nn_p, nn_s, m_p, m_s, conv_p, conv_s))
    stacked = [jnp.stack(z) for z in zip(*outs)]
    return (x[:n_p].reshape(bp, tp, d), x[n_p:].reshape(bs, ts, d)) + tuple(stacked)


def kernel(x_prompt, x_sample, cache_nsa, cache_diff, state_win, state_ml_c, state_ml_n, state_ml_m, state_ml_conv,
           page_table, ffn1_norm, ffn1_w13, ffn1_w2, mix_norm, w_in, nsa_q_norm, nsa_k_norm, nsa_cmp_pos, nsa_cmp_w1,
           nsa_cmp_w2, diff_q_norm, diff_k_norm, diff_lambda, diff_out_norm, ml_conv_w, ml_conv_b, ml_gate_b,
           ml_out_norm, w_out, ffn2_norm, ffn2_w13, ffn2_w2):
    params = dict(ffn1_norm=ffn1_norm, ffn1_w13=ffn1_w13, ffn1_w2=ffn1_w2, mix_norm=mix_norm, w_in=w_in,
                  nsa_q_norm=nsa_q_norm, nsa_k_norm=nsa_k_norm, nsa_cmp_pos=nsa_cmp_pos, nsa_cmp_w1=nsa_cmp_w1,
                  nsa_cmp_w2=nsa_cmp_w2, diff_q_norm=diff_q_norm, diff_k_norm=diff_k_norm, diff_lambda=diff_lambda,
                  diff_out_norm=diff_out_norm, ml_conv_w=ml_conv_w, ml_conv_b=ml_conv_b, ml_gate_b=ml_gate_b,
                  ml_out_norm=ml_out_norm, w_out=w_out, ffn2_norm=ffn2_norm, ffn2_w13=ffn2_w13, ffn2_w2=ffn2_w2)
    return _trunk(x_prompt, x_sample, cache_nsa, cache_diff, state_win, state_ml_c, state_ml_n, state_ml_m,
                  state_ml_conv, page_table, params)
```

```python
import functools
import math
from typing import NamedTuple

import numpy as np
import jax
import jax.numpy as jnp
from jax import lax
from jax.experimental import pallas as pl
from jax.experimental.pallas import tpu as pltpu

F32 = jnp.float32
BF16 = jnp.bfloat16
HIGHEST = lax.Precision.HIGHEST
NT = (((1,), (1,)), ((), ()))

D_MODEL = 2048
D_FF = 5632
PAGE_SIZE = 128
HEAD_DIM = 128
ROW_GROUPS = 8
NSA_HEADS = 8
NSA_KV_HEADS = 2
NSA_GROUP = NSA_HEADS // NSA_KV_HEADS
CMP_BLOCK = 32
CMP_STRIDE = 16
SLC_BLOCK = 64
SLC_SHIFT = 6
SLC_TOPN = 16
WINDOW = 512
DIFF_HEADS = 4
DIFF_QK_DIM = HEAD_DIM // 2
MLSTM_HEADS = 4
MLSTM_DIM = 128
MLSTM_CHUNK = 64
CONV_WIDTH = 4
NORM_EPS = 1e-6
NEG_INF = -1e30
M_INIT = -1e29
LOG2E = 1.4426950408889634
FORCE_BONUS = 1e3

NSA_W = NSA_HEADS * HEAD_DIM
NSA_KV_W = NSA_KV_HEADS * HEAD_DIM
DIFF_W = DIFF_HEADS * HEAD_DIM
MLSTM_W = MLSTM_HEADS * MLSTM_DIM
IN_SPLITS = (NSA_W, 6 * NSA_KV_W, 3 * NSA_HEADS, DIFF_W, DIFF_W, DIFF_W,
             MLSTM_W, MLSTM_W, MLSTM_W, MLSTM_HEADS, MLSTM_HEADS, MLSTM_W)
IN_COLS = sum(IN_SPLITS)

V7X_LANES = 128
V7X_SUBLANES = 8
V7X_VMEM_LIMIT_BYTES = 56 * 1024 * 1024

U_TILE = 256
C_NQ = 0
C_ROWS = C_NQ + NSA_W
C_WIN = C_ROWS + 4 * NSA_KV_W
C_DQ = C_WIN + 2 * NSA_KV_W
C_DROWS = C_DQ + DIFF_W
C_MQ = C_DROWS + 2 * DIFF_W
C_MK = C_MQ + MLSTM_W
C_MV = C_MK + MLSTM_W
C_MO = C_MV + MLSTM_W
C_SMALL = C_MO + MLSTM_W
U_COLS = C_SMALL + U_TILE
U_STEP = 5 * U_TILE
G_NSA, G_MI, G_MF = 0, 3 * NSA_HEADS, 3 * NSA_HEADS + MLSTM_HEADS
U_KINDS = (1, 1, 1, 1, 0, 0, 1, 0, 1, 0, 2, 2, 2, 2, 0, 0, 0, 0, 0, 0, 0, 0, 0, 0, 0)
_MAIN_ORDER = (0, 1, 3, 4, 5, 6, 7, 8, 11)
_SMALL_ORDER = (2, 9, 10)


class _Group(NamedTuple):
    b: int
    t: int
    row0: int
    past: int


def _round_up(n, m):
    return -(-n // m) * m


def _cparams(semantics):
    return pltpu.CompilerParams(dimension_semantics=semantics, vmem_limit_bytes=V7X_VMEM_LIMIT_BYTES)


def _pick_tile(n, prefs):
    for p in prefs:
        if n % p == 0:
            return p
    return n


def _alibi(n):
    return [float(2.0 ** (-8.0 * i / n)) for i in range(1, n + 1)]


def _rms_rows(x, g):
    inv = lax.rsqrt(jnp.mean(x * x, axis=-1, keepdims=True) + NORM_EPS)
    return x * inv * g


def _ffn_body(x_ref, g_ref, wa_ref, wb_ref, w2_ref, o_ref, n_ref):
    @pl.when(pl.program_id(1) == 0)
    def _():
        x = x_ref[...]
        n_ref[...] = _rms_rows(x, g_ref[...]).astype(BF16)
        o_ref[...] = x

    n = n_ref[...]
    a = jnp.dot(n, wa_ref[...], preferred_element_type=F32)
    b = jnp.dot(n, wb_ref[...], preferred_element_type=F32)
    act = (0.5 * a * jax.nn.sigmoid(a) * b).astype(BF16)
    o_ref[...] += jnp.dot(act, w2_ref[...], preferred_element_type=F32)


def _ffn(x, g, w13, w2, layer):
    t, d = x.shape
    f = w2.shape[1]
    tm = _pick_tile(t, (768, 512, 256, 128, 64, 32, 16, 8))
    tf = _pick_tile(f, (512, 256, 128))
    nf = f // tf
    return pl.pallas_call(
        _ffn_body,
        grid=(t // tm, nf),
        in_specs=[
            pl.BlockSpec((tm, d), lambda i, j: (i, 0)),
            pl.BlockSpec((1, d), lambda i, j: (0, 0)),
            pl.BlockSpec((None, d, tf), lambda i, j: (layer, 0, j)),
            pl.BlockSpec((None, d, tf), lambda i, j: (layer, 0, j + nf)),
            pl.BlockSpec((None, tf, d), lambda i, j: (layer, j, 0)),
        ],
        out_specs=pl.BlockSpec((tm, d), lambda i, j: (i, 0)),
        out_shape=jax.ShapeDtypeStruct((t, d), F32),
        scratch_shapes=[pltpu.VMEM((tm, d), BF16)],
        compiler_params=_cparams(("parallel", "arbitrary")),
        name="ffn",
    )(x, g.reshape(1, d), w13, w13, w2)


def _inproj_body(kind_ref, x_ref, g_ref, w_ref, gain_ref, o_ref, n_ref):
    j = pl.program_id(1)

    @pl.when(j == 0)
    def _():
        n_ref[...] = _rms_rows(x_ref[...], g_ref[...]).astype(BF16)

    y = jnp.dot(n_ref[...], w_ref[...], preferred_element_type=F32)
    gain = gain_ref[...]
    for i in range(U_STEP // U_TILE):
        kind = kind_ref[j * (U_STEP // U_TILE) + i]
        halves = [slice(i * U_TILE + h * HEAD_DIM, i * U_TILE + (h + 1) * HEAD_DIM)
                  for h in range(U_TILE // HEAD_DIM)]

        @pl.when(kind == 0)
        def _(halves=halves):
            for cols in halves:
                o_ref[:, cols] = y[:, cols]

        @pl.when(kind == 1)
        def _(halves=halves):
            for cols in halves:
                o_ref[:, cols] = _rms_rows(y[:, cols], gain[:, cols])

        @pl.when(kind == 2)
        def _(halves=halves):
            low = lax.broadcasted_iota(jnp.int32, (y.shape[0], HEAD_DIM), 1) < DIFF_QK_DIM
            for cols in halves:
                yh = y[:, cols]
                sq = yh * yh
                s_lo = jnp.sum(jnp.where(low, sq, 0.0), axis=-1, keepdims=True)
                s_hi = jnp.sum(jnp.where(low, 0.0, sq), axis=-1, keepdims=True)
                inv = lax.rsqrt(jnp.where(low, s_lo, s_hi) * (1.0 / DIFF_QK_DIM) + NORM_EPS)
                o_ref[:, cols] = yh * inv * gain[:, cols]


def _inproj(x, g, w, gain, layer):
    t, d = x.shape
    tm = _pick_tile(t, (768, 512, 256, 128, 64, 32, 16, 8))
    return pl.pallas_call(
        _inproj_body,
        grid=(t // tm, U_COLS // U_STEP),
        in_specs=[
            pl.BlockSpec(memory_space=pltpu.SMEM),
            pl.BlockSpec((tm, d), lambda i, j: (i, 0)),
            pl.BlockSpec((1, d), lambda i, j: (0, 0)),
            pl.BlockSpec((None, d, U_STEP), lambda i, j: (layer, 0, j)),
            pl.BlockSpec((1, U_STEP), lambda i, j: (0, j)),
        ],
        out_specs=pl.BlockSpec((tm, U_STEP), lambda i, j: (i, j)),
        out_shape=jax.ShapeDtypeStruct((t, U_COLS), F32),
        scratch_shapes=[pltpu.VMEM((tm, d), BF16)],
        compiler_params=_cparams(("parallel", "arbitrary")),
        name="inproj",
    )(jnp.asarray(U_KINDS, jnp.int32), x, g.reshape(1, d), w, gain)


def _outproj_body(h_ref, a_ref, b_ref, c_ref, w_ref, y_ref):
    ka, kb = a_ref.shape[1], b_ref.shape[1]
    y = h_ref[...] + jnp.dot(a_ref[...].astype(BF16), w_ref[0:ka, :], preferred_element_type=F32)
    y = y + jnp.dot(b_ref[...].astype(BF16), w_ref[ka:ka + kb, :], preferred_element_type=F32)
    y_ref[...] = y + jnp.dot(c_ref[...].astype(BF16), w_ref[ka + kb:, :], preferred_element_type=F32)


def _outproj(h, o_nsa, o_diff, o_ml, w, layer):
    t, d = h.shape
    tm = _pick_tile(t, (768, 512, 256, 128, 64, 32, 16, 8))
    row = lambda a: pl.BlockSpec((tm, a.shape[1]), lambda i: (i, 0))
    return pl.pallas_call(
        _outproj_body,
        grid=(t // tm,),
        in_specs=[row(h), row(o_nsa), row(o_diff), row(o_ml),
                  pl.BlockSpec((None,) + w.shape[1:], lambda i: (layer, 0, 0))],
        out_specs=row(h),
        out_shape=jax.ShapeDtypeStruct((t, d), F32),
        compiler_params=_cparams(("parallel",)),
        name="outproj",
    )(h, o_nsa, o_diff, o_ml, w)


def _online_update(s, bias, krel, slopes, v, m_ref, l_ref, acc_ref, base, tq):
    def shift_of(slope):
        shift = (slope * LOG2E) * krel
        return jnp.broadcast_to(shift, (tq, shift.shape[1])) if bias is None else bias + shift

    if tq <= V7X_LANES:
        rows = slice(base, base + len(slopes) * tq)
        p, alpha = _online_softmax(s, jnp.concatenate([shift_of(sl) for sl in slopes], axis=0), m_ref, l_ref, rows)
        acc_ref[rows] = alpha * acc_ref[rows] + jnp.dot(p, v, preferred_element_type=F32)
        return
    for g, slope in enumerate(slopes):
        rows = slice(base + g * tq, base + (g + 1) * tq)
        p, alpha = _online_softmax(s[g * tq:(g + 1) * tq], shift_of(slope), m_ref, l_ref, rows)
        acc_ref[rows] = alpha * acc_ref[rows] + jnp.dot(p, v, preferred_element_type=F32)


def _online_softmax(s, shift, m_ref, l_ref, rows):
    sg = s + shift
    m_old = m_ref[rows]
    m_new = jnp.maximum(m_old, jnp.max(sg, axis=-1, keepdims=True))
    p = jnp.exp2(sg - m_new)
    alpha = jnp.exp2(m_old - m_new)
    l_ref[rows] = alpha * l_ref[rows] + jnp.sum(p, axis=-1, keepdims=True)
    m_ref[rows] = m_new
    return p.astype(BF16), alpha


def _init_state(m_ref, l_ref, acc_ref):
    m_ref[...] = jnp.full(m_ref.shape, M_INIT, F32)
    l_ref[...] = jnp.zeros(l_ref.shape, F32)
    acc_ref[...] = jnp.zeros(acc_ref.shape, F32)


def _mask_bias(keep):
    return jnp.where(keep, 0.0, NEG_INF)


def _key_offsets(n, first):
    return (first + lax.broadcasted_iota(jnp.int32, (1, n), 1)).astype(F32)


def _normalised(l_ref, acc_ref, rows):
    return acc_ref[rows] / jnp.maximum(l_ref[rows], 1e-30)


def _distance(tq, n, offset):
    return offset + lax.broadcasted_iota(jnp.int32, (tq, n), 0) - lax.broadcasted_iota(jnp.int32, (tq, n), 1)


def _block_expand(n_spad, n, first_key):
    blk = lax.broadcasted_iota(jnp.int32, (n_spad, n), 0)
    key = first_key + lax.broadcasted_iota(jnp.int32, (n_spad, n), 1)
    return jnp.where(lax.shift_right_logical(key, SLC_SHIFT) == blk, 1.0, 0.0).astype(BF16)


def _stack_heads(q_ref, k):
    return jnp.concatenate([q_ref[:, (k * NSA_GROUP + g) * HEAD_DIM:(k * NSA_GROUP + g + 1) * HEAD_DIM]
                            for g in range(NSA_GROUP)], axis=0).astype(BF16)


def _pad_keys(x, n):
    return jnp.concatenate([x, jnp.zeros((n - x.shape[0], x.shape[1]), x.dtype)], axis=0)


def _nsa_combine(gt_ref, oc_ref, o_ref, l_ref, acc_ref, tq):
    gt = jax.nn.sigmoid(gt_ref[:, 0:V7X_LANES])
    for h in range(NSA_HEADS):
        cols = slice(h * HEAD_DIM, (h + 1) * HEAD_DIM)
        o_slc = _normalised(l_ref, acc_ref, slice(h * tq, (h + 1) * tq))
        o_win = _normalised(l_ref, acc_ref, slice((NSA_HEADS + h) * tq, (NSA_HEADS + h + 1) * tq))
        c0 = G_NSA + 3 * h
        o_ref[:, cols] = gt[:, c0:c0 + 1] * oc_ref[:, cols] + gt[:, c0 + 1:c0 + 2] * o_slc + gt[:, c0 + 2:c0 + 3] * o_win


_NSA_SLOPES = _alibi(NSA_HEADS)
_DIFF_SLOPES = _alibi(DIFF_HEADS)


def _nsa_state(tq):
    rows = 2 * NSA_HEADS * tq
    return [pltpu.VMEM((rows, 1), F32), pltpu.VMEM((rows, 1), F32), pltpu.VMEM((rows, HEAD_DIM), F32)]


def _diff_state(tq):
    rows = 2 * DIFF_HEADS * tq
    return [pltpu.VMEM((rows, 1), F32), pltpu.VMEM((rows, 1), F32), pltpu.VMEM((rows, HEAD_DIM), F32)]


def _cmp_accumulate(rows_of, w_ref, o_ref):
    for c in range(4):
        for r in range(CMP_STRIDE):
            part = jnp.dot(rows_of(c, r).astype(BF16), w_ref[c // 2, r * HEAD_DIM:(r + 1) * HEAD_DIM, :],
                           preferred_element_type=F32)
            if r == 0:
                o_ref[0, c] = part
            else:
                o_ref[0, c] += part


def _cmp_partial_body(x0_ref, x1_ref, x2_ref, x3_ref, w_ref, o_ref):
    srcs = (x0_ref, x1_ref, x2_ref, x3_ref)
    per = x0_ref.shape[0] // CMP_STRIDE
    _cmp_accumulate(lambda c, r: srcs[c][pl.ds(r, per, stride=CMP_STRIDE), :], w_ref, o_ref)


def _cmp_partial_paged_body(pt_ref, *refs, pps):
    del pt_ref
    pages, w_ref, o_ref, kv_ref, slab_ref = refs[:pps], refs[pps], refs[pps + 1], refs[pps + 2], refs[pps + 3]
    for i, pg in enumerate(pages):
        for c in range(4):
            slab_ref[c, i * PAGE_SIZE:(i + 1) * PAGE_SIZE, :] = _page_rows(pg, c)
    per = pps * PAGE_SIZE // CMP_STRIDE
    _cmp_accumulate(lambda c, r: slab_ref[c, pl.ds(r, per, stride=CMP_STRIDE), :], w_ref, o_ref)
    for j in range(4):
        kv_ref[0, j] = _page_piece(pages, 4 + j)


def _cmp_finish_body(*refs, has_tail, n_out):
    if has_tail:
        p_ref, tail_ref, w1_ref, pos_ref, w2_ref, g_ref, o_ref = refs
    else:
        p_ref, w1_ref, pos_ref, w2_ref, g_ref, o_ref = refs
    c = pl.program_id(0)
    p = p_ref[0, 0]
    n = p.shape[0]
    w1 = w1_ref[0]
    pb = jnp.dot(pos_ref[0].astype(BF16), w1, preferred_element_type=F32)
    nxt = pltpu.roll(p[:, HEAD_DIM:], n - 1, 0)
    if has_tail:
        rid = lax.broadcasted_iota(jnp.int32, (V7X_SUBLANES, HEAD_DIM), 0)
        tail8 = jnp.zeros((V7X_SUBLANES, HEAD_DIM), F32)
        xt = tail_ref[...].astype(BF16)
        for r in range(tail_ref.shape[0]):
            d = jnp.dot(xt, w1[r * HEAD_DIM:(r + 1) * HEAD_DIM, HEAD_DIM:], preferred_element_type=F32)
            tail8 = tail8 + jnp.where(rid == r, d, 0.0)
        tail = jnp.sum(tail8, axis=0, keepdims=True)
    else:
        tail = jnp.zeros((1, HEAD_DIM), F32)
    last = lax.broadcasted_iota(jnp.int32, (n, HEAD_DIM), 0) == n - 1
    z = p[:, :HEAD_DIM] + pb[0:1, :HEAD_DIM] + jnp.where(last, tail, nxt) + pb[1:2, HEAD_DIM:]
    y = jnp.dot((z * jax.nn.sigmoid(z)).astype(BF16), w2_ref[0].astype(BF16), preferred_element_type=F32)
    y = jnp.where(c < NSA_KV_HEADS, _rms_rows(y, g_ref[...]), y)
    o_ref[0, 0, 0:n] = y
    if n_out > n:
        o_ref[0, 0, n:n_out] = jnp.zeros((n_out - n, HEAD_DIM), F32)


def _cmp_weights(lp):
    half = CMP_STRIDE * HEAD_DIM
    w1 = lp['nsa_cmp_w1']
    w1ab = jnp.concatenate([w1[:, :half], w1[:, half:]], axis=2).astype(BF16)
    pos = lp['nsa_cmp_pos']
    pos2 = jnp.concatenate([pos[:, :CMP_STRIDE].reshape(2, 1, half), pos[:, CMP_STRIDE:].reshape(2, 1, half),
                            jnp.zeros((2, V7X_SUBLANES - 2, half), F32)], axis=1)
    return w1ab, pos2


def _cmp_finish(p, tail_src, w1ab, pos2, lp, n_out, grp, u):
    b, _, n, _ = p.shape
    half = CMP_STRIDE * HEAD_DIM
    has_tail = tail_src is not None
    in_specs = [pl.BlockSpec((1, 1, n, 2 * HEAD_DIM), lambda c, i: (i, c, 0, 0))]
    args = [p]
    if has_tail:
        rb = grp.row0 // grp.t
        in_specs.append(pl.BlockSpec((grp.t, HEAD_DIM), lambda c, i: (rb + i, C_ROWS // HEAD_DIM + c)))
        args.append(u)
    in_specs += [
        pl.BlockSpec((1, half, 2 * HEAD_DIM), lambda c, i: (c // NSA_KV_HEADS, 0, 0)),
        pl.BlockSpec((1, V7X_SUBLANES, half), lambda c, i: (c // NSA_KV_HEADS, 0, 0)),
        pl.BlockSpec((1, HEAD_DIM, HEAD_DIM), lambda c, i: (c // NSA_KV_HEADS, 0, 0)),
        pl.BlockSpec((1, HEAD_DIM), lambda c, i: (0, 0)),
    ]
    args += [w1ab, pos2, lp['nsa_cmp_w2'], lp['nsa_k_norm'][0:1]]
    return pl.pallas_call(
        functools.partial(_cmp_finish_body, has_tail=has_tail, n_out=n_out),
        grid=(4, b),
        in_specs=in_specs,
        out_specs=pl.BlockSpec((1, 1, n_out, HEAD_DIM), lambda c, i: (i, c, 0, 0)),
        out_shape=jax.ShapeDtypeStruct((b, 4, n_out, HEAD_DIM), F32),
        compiler_params=_cparams(("parallel", "parallel")),
        name="nsa_cmp_finish",
    )(*args)


def _compress_prompt(u, grp, lp):
    assert grp.past == 0 and grp.row0 == 0 and grp.t % (CMP_STRIDE * V7X_SUBLANES) == 0
    w1ab, pos2 = _cmp_weights(lp)
    n = grp.t // CMP_STRIDE
    group = lambda c: pl.BlockSpec((grp.t, HEAD_DIM), lambda i: (i, C_ROWS // HEAD_DIM + c))
    p = pl.pallas_call(
        _cmp_partial_body,
        grid=(grp.b,),
        in_specs=[group(0), group(1), group(2), group(3), pl.BlockSpec(w1ab.shape, lambda i: (0, 0, 0))],
        out_specs=pl.BlockSpec((1, 4, n, 2 * HEAD_DIM), lambda i: (i, 0, 0, 0)),
        out_shape=jax.ShapeDtypeStruct((grp.b, 4, n, 2 * HEAD_DIM), F32),
        compiler_params=_cparams(("parallel",)),
        name="nsa_cmp_partial",
    )(u, u, u, u, w1ab)
    return _cmp_finish(p, None, w1ab, pos2, lp, _round_up(n, V7X_LANES), grp, u), n - 1


def _pages_per_step(n_pages):
    return _pick_tile(n_pages, (16, 8, 4, 2, 1))


def _page_view(cache):
    assert cache.shape[3] * cache.shape[4] == ROW_GROUPS and cache.shape[5] == HEAD_DIM
    return cache.reshape(cache.shape[0], cache.shape[1], PAGE_SIZE * ROW_GROUPS, HEAD_DIM)


def _page_specs(layer, n_pages, pps):
    def spec(i):
        return pl.BlockSpec((1, 1, PAGE_SIZE * ROW_GROUPS, HEAD_DIM),
                            lambda b, c, pt: (layer, pt[b * n_pages + c * pps + i], 0, 0))
    return [spec(i) for i in range(pps)]


def _page_rows(page, j):
    return page[0, 0, pl.ds(j, PAGE_SIZE, stride=ROW_GROUPS), :]


def _page_piece(pages, j):
    return jnp.concatenate([_page_rows(pg, j).astype(BF16) for pg in pages], axis=0)


def _compress_sample(u, grp, lp, cache, page_table, layer):
    assert grp.past % PAGE_SIZE == 0 and grp.t < CMP_STRIDE
    w1ab, pos2 = _cmp_weights(lp)
    n_pages = grp.past // PAGE_SIZE
    pps = _pages_per_step(n_pages)
    per = pps * PAGE_SIZE // CMP_STRIDE
    n = grp.past // CMP_STRIDE
    p, slc_kv = pl.pallas_call(
        functools.partial(_cmp_partial_paged_body, pps=pps),
        grid_spec=pltpu.PrefetchScalarGridSpec(
            num_scalar_prefetch=1,
            grid=(grp.b, n_pages // pps),
            in_specs=_page_specs(layer, n_pages, pps) + [
                pl.BlockSpec(w1ab.shape, lambda b, c, pt: (0, 0, 0))],
            out_specs=[
                pl.BlockSpec((1, 4, per, 2 * HEAD_DIM), lambda b, c, pt: (b, 0, c, 0)),
                pl.BlockSpec((1, 4, pps * PAGE_SIZE, HEAD_DIM), lambda b, c, pt: (b, 0, c, 0)),
            ],
            scratch_shapes=[pltpu.VMEM((4, pps * PAGE_SIZE, HEAD_DIM), F32)],
        ),
        out_shape=[
            jax.ShapeDtypeStruct((grp.b, 4, n, 2 * HEAD_DIM), F32),
            jax.ShapeDtypeStruct((grp.b, 4, grp.past, HEAD_DIM), BF16),
        ],
        compiler_params=_cparams(("parallel", "arbitrary")),
        name="nsa_cmp_partial_paged",
    )(page_table, *([cache] * pps), w1ab)
    return _cmp_finish(p, u, w1ab, pos2, lp, _round_up(n + 1, V7X_LANES), grp, u), n, slc_kv


def _cmp_select_body(q_ref, kv_ref, o_ref, sel_ref, *, tq, q_off, n_cmp, n_slc, n_sel):
    qi = pl.program_id(1)
    n_cpad = kv_ref.shape[2]
    n_spad = sel_ref.shape[1] // NSA_KV_HEADS
    q_lo = q_off + qi * tq
    qpos = q_lo + lax.broadcasted_iota(jnp.int32, (tq, n_cpad), 0)
    ends = lax.broadcasted_iota(jnp.int32, (tq, n_cpad), 1) * CMP_STRIDE + (CMP_BLOCK - 1)
    d = qpos - ends
    mask = d >= 0
    df = d.astype(F32)
    c_lo = lax.broadcasted_iota(jnp.int32, (n_cpad, n_spad), 0) * CMP_STRIDE
    j_lo = lax.broadcasted_iota(jnp.int32, (n_cpad, n_spad), 1) * SLC_BLOCK
    cover = jnp.where(c_lo < j_lo + SLC_BLOCK, jnp.where(c_lo + CMP_BLOCK > j_lo, 1.0, 0.0), 0.0)
    cover = jnp.where(c_lo < n_cmp * CMP_STRIDE, cover, 0.0)
    qp = q_lo + lax.broadcasted_iota(jnp.int32, (tq, n_spad), 0)
    j = lax.broadcasted_iota(jnp.int32, (tq, n_spad), 1)
    cur = lax.shift_right_logical(qp, SLC_SHIFT)
    bonus = FORCE_BONUS * jnp.where(j == 0, 1.0, jnp.where(j == cur, 1.0, jnp.where(j == cur - 1, 1.0, 0.0)))
    valid = j * SLC_BLOCK <= qp
    for k in range(NSA_KV_HEADS):
        q = jnp.concatenate([q_ref[:, (k * NSA_GROUP + g) * HEAD_DIM:(k * NSA_GROUP + g + 1) * HEAD_DIM]
                             for g in range(NSA_GROUP)], axis=0)
        s = lax.dot_general(q, kv_ref[0, k], NT, precision=HIGHEST, preferred_element_type=F32)
        vc = kv_ref[0, NSA_KV_HEADS + k].astype(BF16)
        psum = jnp.zeros((tq, n_cpad), F32)
        for g in range(NSA_GROUP):
            h = k * NSA_GROUP + g
            sg = jnp.where(mask, s[g * tq:(g + 1) * tq] - (_NSA_SLOPES[h] * LOG2E) * df, NEG_INF)
            m = jnp.max(sg, axis=-1, keepdims=True)
            p = jnp.where(mask, jnp.exp2(sg - m), 0.0)
            p = p / jnp.maximum(jnp.sum(p, axis=-1, keepdims=True), 1e-30)
            o_ref[:, h * HEAD_DIM:(h + 1) * HEAD_DIM] = jnp.dot(p.astype(BF16), vc, preferred_element_type=F32)
            psum = psum + p
        imp = jnp.dot(psum, cover, precision=HIGHEST, preferred_element_type=F32)
        score = jnp.where(valid, imp + bonus, NEG_INF)
        cnt = jnp.zeros((tq, n_spad), F32)
        for i in range(n_slc):
            ci = score[:, i:i + 1]
            cnt = cnt + jnp.where(ci > score, 1.0, jnp.where(ci == score, jnp.where(j > i, 1.0, 0.0), 0.0))
        sel_ref[:, k * n_spad:(k + 1) * n_spad] = jnp.where(cnt < n_sel, jnp.where(j < n_slc, 0.0, NEG_INF), NEG_INF)


def _cmp_select(u, grp, summaries, n_cmp):
    n_slc = -(-(grp.past + grp.t) // SLC_BLOCK)
    n_spad = _round_up(n_slc, V7X_LANES)
    n_cpad = summaries.shape[2]
    tq = _pick_tile(grp.t, (256, 128, 64, 32, 16, 8))
    nq = grp.t // tq
    rb = grp.row0 // tq
    body = functools.partial(_cmp_select_body, tq=tq, q_off=grp.past, n_cmp=n_cmp, n_slc=n_slc,
                             n_sel=min(SLC_TOPN, n_slc))
    return pl.pallas_call(
        body,
        grid=(grp.b, nq),
        in_specs=[
            pl.BlockSpec((tq, NSA_W), lambda b, i: (rb + b * nq + i, 0)),
            pl.BlockSpec((1, 4, n_cpad, HEAD_DIM), lambda b, i: (b, 0, 0, 0)),
        ],
        out_specs=[
            pl.BlockSpec((tq, NSA_W), lambda b, i: (b * nq + i, 0)),
            pl.BlockSpec((tq, NSA_KV_HEADS * n_spad), lambda b, i: (b * nq + i, 0)),
        ],
        out_shape=[
            jax.ShapeDtypeStruct((grp.b * grp.t, NSA_W), F32),
            jax.ShapeDtypeStruct((grp.b * grp.t, NSA_KV_HEADS * n_spad), F32),
        ],
        compiler_params=_cparams(("parallel", "parallel")),
        name="nsa_cmp_select",
    )(u, summaries)


def _nsa_prompt_body(q_ref, ks_ref, kw_ref, sel_ref, oc_ref, gt_ref, o_ref, m_ref, l_ref, acc_ref, *, tq, tk):
    qi = pl.program_id(1)
    kj = pl.program_id(2)
    n_spad = sel_ref.shape[1] // NSA_KV_HEADS

    @pl.when(kj == 0)
    def _():
        _init_state(m_ref, l_ref, acc_ref)

    q_lo = qi * tq
    k_lo = kj * tk
    slc_on = k_lo <= q_lo + (tq - 1)
    win_on = jnp.logical_and(slc_on, k_lo + (tk - 1) >= q_lo - WINDOW)

    krel = _key_offsets(tk, k_lo - q_lo)

    @pl.when(slc_on)
    def _():
        causal = _mask_bias(_distance(tq, tk, q_lo - k_lo) >= 0)
        expand = _block_expand(n_spad, tk, k_lo)
        kv = ks_ref[...].astype(BF16)
        for k in range(NSA_KV_HEADS):
            s = lax.dot_general(_stack_heads(q_ref, k), kv[:, k * HEAD_DIM:(k + 1) * HEAD_DIM], NT,
                                preferred_element_type=F32)
            chosen = jnp.dot(sel_ref[:, k * n_spad:(k + 1) * n_spad].astype(BF16), expand,
                             preferred_element_type=F32)
            _online_update(s, causal + chosen, krel, _NSA_SLOPES[k * NSA_GROUP:(k + 1) * NSA_GROUP],
                           kv[:, (NSA_KV_HEADS + k) * HEAD_DIM:(NSA_KV_HEADS + k + 1) * HEAD_DIM],
                           m_ref, l_ref, acc_ref, k * NSA_GROUP * tq, tq)

    @pl.when(win_on)
    def _():
        d = _distance(tq, tk, q_lo - k_lo)
        bias = _mask_bias(jnp.logical_and(d >= 0, d <= WINDOW))
        kv = kw_ref[...].astype(BF16)
        for k in range(NSA_KV_HEADS):
            s = lax.dot_general(_stack_heads(q_ref, k), kv[:, k * HEAD_DIM:(k + 1) * HEAD_DIM], NT,
                                preferred_element_type=F32)
            _online_update(s, bias, krel, _NSA_SLOPES[k * NSA_GROUP:(k + 1) * NSA_GROUP],
                           kv[:, (NSA_KV_HEADS + k) * HEAD_DIM:(NSA_KV_HEADS + k + 1) * HEAD_DIM],
                           m_ref, l_ref, acc_ref, (NSA_KV_HEADS + k) * NSA_GROUP * tq, tq)

    @pl.when(kj == pl.num_programs(2) - 1)
    def _():
        _nsa_combine(gt_ref, oc_ref, o_ref, l_ref, acc_ref, tq)


def _nsa_prompt(u, grp, o_cmp, sel):
    assert grp.past == 0 and grp.row0 == 0
    t = grp.t
    tq = _pick_tile(t, (256, 128, 64, 32, 16, 8))
    tk = _pick_tile(t, (512, 256, 128))
    nq, nk = t // tq, t // tk
    kvw = 2 * NSA_KV_W

    def last_tile(i):
        return (i * tq + (tq - 1)) // tk

    def slc_rows(b, i, j):
        return (b * nk + jnp.minimum(j, last_tile(i)), (C_ROWS + kvw) // kvw)

    def win_rows(b, i, j):
        first = jnp.maximum(i * tq - WINDOW, 0) // tk
        return (b * nk + jnp.clip(j, first, last_tile(i)), C_WIN // kvw)

    return pl.pallas_call(
        functools.partial(_nsa_prompt_body, tq=tq, tk=tk),
        grid=(grp.b, nq, nk),
        in_specs=[
            pl.BlockSpec((tq, NSA_W), lambda b, i, j: (b * nq + i, 0)),
            pl.BlockSpec((tk, kvw), slc_rows),
            pl.BlockSpec((tk, kvw), win_rows),
            pl.BlockSpec((tq, sel.shape[1]), lambda b, i, j: (b * nq + i, 0)),
            pl.BlockSpec((tq, NSA_W), lambda b, i, j: (b * nq + i, 0)),
            pl.BlockSpec((tq, U_TILE), lambda b, i, j: (b * nq + i, C_SMALL // U_TILE)),
        ],
        out_specs=pl.BlockSpec((tq, NSA_W), lambda b, i, j: (b * nq + i, 0)),
        out_shape=jax.ShapeDtypeStruct((grp.b * t, NSA_W), F32),
        scratch_shapes=_nsa_state(tq),
        compiler_params=_cparams(("parallel", "parallel", "arbitrary")),
        name="nsa_attn_prompt",
    )(u, u, u, sel, o_cmp, u)


def _nsa_sample_body(q_ref, kv_ref, knew_ref, wst_ref, wnew_ref, sel_ref, oc_ref, gt_ref, o_ref, m_ref, l_ref, acc_ref,
                     *, past, n_win):
    c = pl.program_id(1)
    tq = q_ref.shape[0]
    n_spad = sel_ref.shape[1] // NSA_KV_HEADS
    n = kv_ref.shape[2]

    @pl.when(c == 0)
    def _():
        _init_state(m_ref, l_ref, acc_ref)

    def attend(keys_of, vals_of, bias_of, krel, branch):
        for k in range(NSA_KV_HEADS):
            s = lax.dot_general(_stack_heads(q_ref, k), keys_of(k), NT, preferred_element_type=F32)
            _online_update(s, bias_of(k), krel, _NSA_SLOPES[k * NSA_GROUP:(k + 1) * NSA_GROUP], vals_of(k),
                           m_ref, l_ref, acc_ref, (branch * NSA_KV_HEADS + k) * NSA_GROUP * tq, tq)

    def slab(ref, j):
        return ref[:, j * HEAD_DIM:(j + 1) * HEAD_DIM]

    first_key = c * n
    expand = _block_expand(n_spad, n, first_key)
    attend(lambda k: kv_ref[0, k], lambda k: kv_ref[0, NSA_KV_HEADS + k],
           lambda k: jnp.dot(sel_ref[:, k * n_spad:(k + 1) * n_spad].astype(BF16), expand,
                             preferred_element_type=F32),
           _key_offsets(n, first_key - past), 0)

    @pl.when(c == pl.num_programs(1) - 1)
    def _():
        npad = V7X_LANES
        real = lax.broadcasted_iota(jnp.int32, (tq, npad), 1) < tq
        causal = _mask_bias(jnp.logical_and(_distance(tq, npad, 0) >= 0, real))
        lane = lax.broadcasted_iota(jnp.int32, (tq, n_spad), 1)

        def chosen_new(k):
            col = jnp.where(lane == past // SLC_BLOCK, sel_ref[:, k * n_spad:(k + 1) * n_spad], 0.0)
            return causal + jnp.sum(col, axis=-1, keepdims=True)

        knew = _pad_keys(knew_ref[...], npad).astype(BF16)
        attend(lambda k: slab(knew, k), lambda k: slab(knew, NSA_KV_HEADS + k), chosen_new, _key_offsets(npad, 0), 0)
        wbuf = wst_ref[0, 0].astype(BF16)
        in_window = _mask_bias(_distance(tq, n_win, n_win) <= WINDOW)
        attend(lambda k: slab(wbuf, k), lambda k: slab(wbuf, NSA_KV_HEADS + k), lambda k: in_window,
               _key_offsets(n_win, -n_win), 1)
        wnew = _pad_keys(wnew_ref[...], npad).astype(BF16)
        attend(lambda k: slab(wnew, k), lambda k: slab(wnew, NSA_KV_HEADS + k), lambda k: causal,
               _key_offsets(npad, 0), 1)
        _nsa_combine(gt_ref, oc_ref, o_ref, l_ref, acc_ref, tq)


def _nsa_sample(u, grp, o_cmp, sel, slc_kv, win_state, layer):
    t = grp.t
    n_win = win_state.shape[2]
    assert n_win == WINDOW and grp.past % SLC_BLOCK == 0 and t <= SLC_BLOCK and grp.row0 % t == 0
    n = _pick_tile(grp.past, (2048, 1024, 512, 256, 128))
    kvw = 2 * NSA_KV_W
    rb = grp.row0 // t
    tok = lambda width, col: pl.BlockSpec((t, width), lambda b, c: (rb + b, col))
    own = lambda width: pl.BlockSpec((t, width), lambda b, c: (b, 0))
    return pl.pallas_call(
        functools.partial(_nsa_sample_body, past=grp.past, n_win=n_win),
        grid=(grp.b, grp.past // n),
        in_specs=[
            tok(NSA_W, 0),
            pl.BlockSpec((1, 4, n, HEAD_DIM), lambda b, c: (b, 0, c, 0)),
            tok(kvw, (C_ROWS + kvw) // kvw),
            pl.BlockSpec((1, 1, n_win, kvw), lambda b, c: (layer, b, 0, 0)),
            tok(kvw, C_WIN // kvw),
            own(sel.shape[1]),
            own(NSA_W),
            tok(U_TILE, C_SMALL // U_TILE),
        ],
        out_specs=own(NSA_W),
        out_shape=jax.ShapeDtypeStruct((grp.b * t, NSA_W), F32),
        scratch_shapes=_nsa_state(t),
        compiler_params=_cparams(("parallel", "arbitrary")),
        name="nsa_attn_cached",
    )(u, slc_kv, u, win_state, u, sel, o_cmp, u)


def _diff_update(q_ref, group_of, bias, krel, m_ref, l_ref, acc_ref):
    tq = q_ref.shape[0]
    low = lax.broadcasted_iota(jnp.int32, (tq, HEAD_DIM), 1) < DIFF_QK_DIM
    scores, shifts = [], []
    for h in range(DIFF_HEADS):
        qh = q_ref[:, h * HEAD_DIM:(h + 1) * HEAD_DIM]
        q2 = jnp.concatenate([jnp.where(low, qh, 0.0), jnp.where(low, 0.0, qh)], axis=0).astype(BF16)
        scores.append(lax.dot_general(q2, group_of(h), NT, preferred_element_type=F32))
        shift = (_DIFF_SLOPES[h] * LOG2E) * krel
        shift = jnp.broadcast_to(shift, (tq, shift.shape[1])) if bias is None else bias + shift
        shifts += [shift, shift]
    p, alpha = _online_softmax(jnp.concatenate(scores, axis=0), jnp.concatenate(shifts, axis=0), m_ref, l_ref,
                               slice(0, 2 * DIFF_HEADS * tq))
    for h in range(DIFF_HEADS):
        rows = slice(2 * h * tq, 2 * (h + 1) * tq)
        acc_ref[rows] = alpha[rows] * acc_ref[rows] + jnp.dot(p[rows], group_of(DIFF_HEADS + h),
                                                               preferred_element_type=F32)


def _diff_finish(lam_ref, g_ref, o_ref, l_ref, acc_ref, post):
    tq = o_ref.shape[0]
    lam = lam_ref[0]
    for h in range(DIFF_HEADS):
        o1 = _normalised(l_ref, acc_ref, slice(2 * h * tq, (2 * h + 1) * tq))
        o2 = _normalised(l_ref, acc_ref, slice((2 * h + 1) * tq, (2 * h + 2) * tq))
        o_ref[:, h * HEAD_DIM:(h + 1) * HEAD_DIM] = _rms_rows(o1 - lam * o2, g_ref[...]) * post


def _diff_prompt_body(lam_ref, q_ref, kv_ref, g_ref, o_ref, m_ref, l_ref, acc_ref, *, tq, tk, post):
    qi = pl.program_id(1)
    kj = pl.program_id(2)

    @pl.when(kj == 0)
    def _():
        _init_state(m_ref, l_ref, acc_ref)

    @pl.when(kj * tk <= qi * tq + (tq - 1))
    def _():
        kv = kv_ref[...].astype(BF16)
        _diff_update(q_ref, lambda j: kv[:, j * HEAD_DIM:(j + 1) * HEAD_DIM],
                     _mask_bias(_distance(tq, tk, qi * tq - kj * tk) >= 0), _key_offsets(tk, kj * tk - qi * tq),
                     m_ref, l_ref, acc_ref)

    @pl.when(kj == pl.num_programs(2) - 1)
    def _():
        _diff_finish(lam_ref, g_ref, o_ref, l_ref, acc_ref, post)


def _diff_prompt(u, grp, lam, gain, post):
    assert grp.past == 0 and grp.row0 == 0
    t = grp.t
    tq = _pick_tile(t, (256, 128, 64, 32, 16, 8))
    tk = _pick_tile(t, (512, 256, 128))
    nq, nk = t // tq, t // tk
    return pl.pallas_call(
        functools.partial(_diff_prompt_body, tq=tq, tk=tk, post=post),
        grid=(grp.b, nq, nk),
        in_specs=[
            pl.BlockSpec(memory_space=pltpu.SMEM),
            pl.BlockSpec((tq, DIFF_W), lambda b, i, j: (b * nq + i, C_DQ // DIFF_W)),
            pl.BlockSpec((tk, 2 * DIFF_W),
                         lambda b, i, j: (b * nk + jnp.minimum(j, (i * tq + (tq - 1)) // tk), C_DROWS // (2 * DIFF_W))),
            pl.BlockSpec((1, HEAD_DIM), lambda b, i, j: (0, 0)),
        ],
        out_specs=pl.BlockSpec((tq, DIFF_W), lambda b, i, j: (b * nq + i, 0)),
        out_shape=jax.ShapeDtypeStruct((grp.b * t, DIFF_W), F32),
        scratch_shapes=_diff_state(tq),
        compiler_params=_cparams(("parallel", "parallel", "arbitrary")),
        name="diff_attn_prompt",
    )(lam, u, u, gain)


def _diff_sample_body(pt_ref, lam_ref, q_ref, *rest, pps, past, post):
    pages = rest[:pps]
    new_ref, g_ref, o_ref, m_ref, l_ref, acc_ref = rest[pps:]
    del pt_ref
    c = pl.program_id(1)
    tq = q_ref.shape[0]
    n = pps * PAGE_SIZE

    @pl.when(c == 0)
    def _():
        _init_state(m_ref, l_ref, acc_ref)

    _diff_update(q_ref, lambda j: _page_piece(pages, j), None, _key_offsets(n, c * n - past), m_ref, l_ref, acc_ref)

    @pl.when(c == pl.num_programs(1) - 1)
    def _():
        npad = V7X_LANES
        keep = jnp.logical_and(_distance(tq, npad, 0) >= 0, lax.broadcasted_iota(jnp.int32, (tq, npad), 1) < tq)
        new = _pad_keys(new_ref[...], npad).astype(BF16)
        _diff_update(q_ref, lambda j: new[:, j * HEAD_DIM:(j + 1) * HEAD_DIM], _mask_bias(keep),
                     _key_offsets(npad, 0), m_ref, l_ref, acc_ref)
        _diff_finish(lam_ref, g_ref, o_ref, l_ref, acc_ref, post)


def _diff_sample(u, grp, lam, gain, post, cache, page_table, layer):
    t = grp.t
    assert grp.past % PAGE_SIZE == 0 and grp.row0 % t == 0
    n_pages = grp.past // PAGE_SIZE
    pps = _pages_per_step(n_pages)
    rb = grp.row0 // t
    return pl.pallas_call(
        functools.partial(_diff_sample_body, pps=pps, past=grp.past, post=post),
        grid_spec=pltpu.PrefetchScalarGridSpec(
            num_scalar_prefetch=1,
            grid=(grp.b, n_pages // pps),
            in_specs=[
                pl.BlockSpec(memory_space=pltpu.SMEM),
                pl.BlockSpec((t, DIFF_W), lambda b, c, pt: (rb + b, C_DQ // DIFF_W)),
            ] + _page_specs(layer, n_pages, pps) + [
                pl.BlockSpec((t, 2 * DIFF_W), lambda b, c, pt: (rb + b, C_DROWS // (2 * DIFF_W))),
                pl.BlockSpec((1, HEAD_DIM), lambda b, c, pt: (0, 0)),
            ],
            out_specs=pl.BlockSpec((t, DIFF_W), lambda b, c, pt: (b, 0)),
            scratch_shapes=_diff_state(t),
        ),
        out_shape=jax.ShapeDtypeStruct((grp.b * t, DIFF_W), F32),
        compiler_params=_cparams(("parallel", "arbitrary")),
        name="diff_attn_paged",
    )(page_table, lam, u, *([cache] * pps), u, gain)


def _mlstm_body(xq_ref, xk_ref, v_ref, og_ref, bq_ref, bk_ref, wq_ref, wk_ref, cq_ref, ck_ref, li_ref, lf_ref,
                c0_ref, n0_ref, m0_ref, g_ref, h_ref, c_ref, n_ref, m_ref, q_s, k_s, *, lc, nc):
    t = xq_ref.shape[0]
    row = lax.broadcasted_iota(jnp.int32, (lc, lc), 0)
    col = lax.broadcasted_iota(jnp.int32, (lc, lc), 1)
    tril = row >= col
    triu = row <= col
    eye = row == col
    gain = g_ref[...]

    def conv(x_ref, buf_ref, w_ref, b_ref):
        ext = jnp.concatenate([buf_ref[0], x_ref[...]], axis=0)
        y = b_ref[...]
        for i in range(CONV_WIDTH):
            shifted = ext if i == CONV_WIDTH - 1 else pltpu.roll(ext, CONV_WIDTH - 1 - i, 0)
            y = y + shifted[V7X_SUBLANES:V7X_SUBLANES + t] * w_ref[i:i + 1, :]
        return y * jax.nn.sigmoid(y)

    q_s[...] = conv(xq_ref, bq_ref, wq_ref, cq_ref) * MLSTM_DIM ** -0.5
    k_s[...] = conv(xk_ref, bk_ref, wk_ref, ck_ref)

    def to_col(r):
        return jnp.sum(jnp.where(eye, jnp.broadcast_to(r, (lc, lc)), 0.0), axis=1, keepdims=True)

    def step(ci, carry):
        c, n, m = carry
        rows = pl.ds(ci * lc if nc == 1 else pl.multiple_of(ci * lc, lc), lc)
        qc = q_s[rows, :]
        kc = k_s[rows, :]
        vc = v_ref[rows, :]
        li = li_ref[0, pl.ds(ci, 1), :]
        lf = lf_ref[0, pl.ds(ci, 1), :]
        lf_b = jnp.broadcast_to(lf, (lc, lc))
        bcum_c = jnp.sum(jnp.where(tril, lf_b, 0.0), axis=1, keepdims=True)
        bcum_r = jnp.sum(jnp.where(triu, jnp.broadcast_to(to_col(lf), (lc, lc)), 0.0), axis=0, keepdims=True)
        dmat = jnp.where(tril, bcum_c - bcum_r + li, NEG_INF)
        inter = bcum_c + m
        mj = jnp.maximum(inter, jnp.max(dmat, axis=1, keepdims=True))
        wts = jnp.exp(dmat - mj)
        a = jnp.exp(inter - mj)
        sqk = lax.dot_general(qc, kc, NT, precision=HIGHEST, preferred_element_type=F32) * wts
        num = a * jnp.dot(qc, c, precision=HIGHEST, preferred_element_type=F32) + \
            jnp.dot(sqk, vc, precision=HIGHEST, preferred_element_type=F32)
        den = a * jnp.sum(qc * n, axis=1, keepdims=True) + jnp.sum(sqk, axis=1, keepdims=True)
        h = num / jnp.maximum(jnp.abs(den), jnp.exp(-mj))
        h_ref[rows, :] = _rms_rows(h, gain) * jax.nn.sigmoid(og_ref[rows, :])
        b_last = jnp.sum(lf, axis=1, keepdims=True)
        gl = b_last - bcum_r + li
        m_new = jnp.maximum(b_last + m, jnp.max(gl, axis=1, keepdims=True))
        decay = jnp.exp(b_last + m - m_new)
        kw = kc * to_col(jnp.exp(gl - m_new))
        c_new = decay * c + lax.dot_general(kw, vc, (((0,), (0,)), ((), ())), precision=HIGHEST,
                                            preferred_element_type=F32)
        n_new = decay * n + jnp.sum(kw, axis=0, keepdims=True)
        return c_new, n_new, m_new

    init = (c0_ref[0], n0_ref[0], m0_ref[0][:, 0:1])
    c, n, m = step(0, init) if nc == 1 else lax.fori_loop(0, nc, step, init, unroll=2)
    c_ref[0] = c
    n_ref[0] = n
    m_ref[0] = jnp.broadcast_to(m, (1, MLSTM_DIM))


def _mlstm(u, grp, lp, c0, n0, m0, conv_buf):
    b, t = grp.b, grp.t
    assert grp.row0 % t == 0 and t % V7X_SUBLANES == 0
    bh = b * MLSTM_HEADS
    hd = MLSTM_DIM
    lc = MLSTM_CHUNK if t % MLSTM_CHUNK == 0 else t
    nc = t // lc
    rb = grp.row0 // t
    gates = u[grp.row0:grp.row0 + b * t, C_SMALL:C_SMALL + U_TILE].reshape(b, t, U_TILE)
    log_i = gates[:, :, G_MI:G_MI + MLSTM_HEADS] + lp['ml_gate_b'][0]
    log_f = jax.nn.log_sigmoid(gates[:, :, G_MF:G_MF + MLSTM_HEADS] + lp['ml_gate_b'][1])
    log_i = log_i.transpose(0, 2, 1).reshape(bh, nc, lc)
    log_f = log_f.transpose(0, 2, 1).reshape(bh, nc, lc)
    buf8 = jnp.pad(conv_buf, ((0, 0), (V7X_SUBLANES - (CONV_WIDTH - 1), 0), (0, 0)))
    w8 = jnp.pad(lp['ml_conv_w'], ((0, V7X_SUBLANES - CONV_WIDTH), (0, 0)))
    cb = lp['ml_conv_b'].reshape(1, 2 * MLSTM_W)
    tok = lambda col0: pl.BlockSpec((t, hd), lambda i, h: (rb + i, col0 // hd + h))
    buf = lambda off: pl.BlockSpec((1, V7X_SUBLANES, hd), lambda i, h: (i, 0, off + h))
    wsp = lambda off: pl.BlockSpec((V7X_SUBLANES, hd), lambda i, h: (0, off + h))
    bsp = lambda off: pl.BlockSpec((1, hd), lambda i, h: (0, off + h))
    gate = pl.BlockSpec((1, nc, lc), lambda i, h: (i * MLSTM_HEADS + h, 0, 0))
    vec = pl.BlockSpec((1, 1, hd), lambda i, h: (i * MLSTM_HEADS + h, 0, 0))
    mat = pl.BlockSpec((1, hd, hd), lambda i, h: (i * MLSTM_HEADS + h, 0, 0))
    h, c, n, m = pl.pallas_call(
        functools.partial(_mlstm_body, lc=lc, nc=nc),
        grid=(b, MLSTM_HEADS),
        in_specs=[tok(C_MQ), tok(C_MK), tok(C_MV), tok(C_MO), buf(0), buf(MLSTM_HEADS), wsp(0), wsp(MLSTM_HEADS),
                  bsp(0), bsp(MLSTM_HEADS), gate, gate, mat, vec, vec, pl.BlockSpec((1, hd), lambda i, h: (0, 0))],
        out_specs=[pl.BlockSpec((t, hd), lambda i, h: (i, h)), mat, vec, vec],
        out_shape=[
            jax.ShapeDtypeStruct((b * t, MLSTM_W), F32),
            jax.ShapeDtypeStruct((bh, hd, hd), F32),
            jax.ShapeDtypeStruct((bh, 1, hd), F32),
            jax.ShapeDtypeStruct((bh, 1, hd), F32),
        ],
        scratch_shapes=[pltpu.VMEM((t, hd), F32), pltpu.VMEM((t, hd), F32)],
        compiler_params=_cparams(("parallel", "parallel")),
        name="mlstm",
    )(u, u, u, u, buf8, buf8, w8, w8, cb, cb, log_i, log_f, c0.reshape(bh, hd, hd), n0.reshape(bh, 1, hd),
      jnp.broadcast_to(m0.reshape(bh, 1, 1), (bh, 1, hd)), lp['ml_out_norm'].reshape(1, hd))
    return (h, c.reshape(b, MLSTM_HEADS, hd, hd), n.reshape(b, MLSTM_HEADS, hd), m[:, 0, 0].reshape(b, MLSTM_HEADS))


def _cache_rows_body(nsa_ref, diff_ref, *rest):
    nsa_out, diff_out = rest[-2:]
    tm = nsa_ref.shape[0]
    for j in range(ROW_GROUPS):
        cols = slice(j * HEAD_DIM, (j + 1) * HEAD_DIM)
        nsa_out[pl.ds(j, tm, stride=ROW_GROUPS), :] = nsa_ref[:, cols]
        diff_out[pl.ds(j, tm, stride=ROW_GROUPS), :] = diff_ref[:, cols]


def _cache_rows(u, layer, depth, prev):
    t = u.shape[0]
    tm = _pick_tile(t, (256, 128, 64, 32, 16, 8))
    width = ROW_GROUPS * HEAD_DIM
    in_specs = [pl.BlockSpec((tm, width), lambda i: (i, C_ROWS // width)),
                pl.BlockSpec((tm, width), lambda i: (i, C_DROWS // width))]
    args = [u, u]
    aliases = {}
    if prev is not None:
        in_specs += [pl.BlockSpec(memory_space=pl.ANY)] * 2
        args += list(prev)
        aliases = {2: 0, 3: 1}
    out = pl.BlockSpec((None, tm * ROW_GROUPS, HEAD_DIM), lambda i: (layer, i, 0))
    shape = jax.ShapeDtypeStruct((depth, t * ROW_GROUPS, HEAD_DIM), F32)
    return pl.pallas_call(
        _cache_rows_body,
        grid=(t // tm,),
        in_specs=in_specs,
        out_specs=[out, out],
        out_shape=[shape, shape],
        input_output_aliases=aliases,
        compiler_params=_cparams(("parallel",)),
        name="cache_rows",
    )(*args)


def _pack_w_in(w_in):
    cuts = np.cumsum((0,) + IN_SPLITS)
    cols = [w_in[..., cuts[i]:cuts[i + 1]] for i in range(len(IN_SPLITS))]
    packed = jnp.concatenate([cols[i] for i in _MAIN_ORDER] + [cols[i] for i in _SMALL_ORDER], axis=-1)
    pad = [(0, 0)] * (packed.ndim - 1) + [(0, U_COLS - packed.shape[-1])]
    return jnp.pad(packed, pad).astype(BF16)


def _u_gain(lp):
    ones = lambda n: jnp.ones((n,), F32)
    g = jnp.concatenate([
        jnp.tile(lp['nsa_q_norm'], NSA_HEADS) * (HEAD_DIM ** -0.5 * LOG2E),
        ones(2 * NSA_KV_W), jnp.tile(lp['nsa_k_norm'][1], NSA_KV_HEADS), ones(NSA_KV_W),
        jnp.tile(lp['nsa_k_norm'][2], NSA_KV_HEADS), ones(NSA_KV_W),
        jnp.tile(lp['diff_q_norm'], 2 * DIFF_HEADS) * (DIFF_QK_DIM ** -0.5 * LOG2E),
        jnp.tile(lp['diff_k_norm'], 2 * DIFF_HEADS),
        ones(U_COLS - (C_DROWS + DIFF_W))])
    return g.reshape(1, U_COLS)


def _group_rows(u, grp, col0, width):
    return u[grp.row0:grp.row0 + grp.b * grp.t, col0:col0 + width].reshape(grp.b, grp.t, width)


def _tail_rows(buf, new, keep):
    n_new = min(keep, new.shape[1])
    parts = [buf[:, buf.shape[1] - (keep - n_new):], new[:, new.shape[1] - n_new:]] if keep > n_new else \
        [new[:, new.shape[1] - n_new:]]
    return jnp.concatenate(parts, axis=1)


def _trunk(x_prompt, x_sample, cache_nsa, cache_diff, state_win, state_ml_c, state_ml_n, state_ml_m, state_ml_conv,
           page_table, p):
    bp, tp, d = x_prompt.shape
    bs, ts, _ = x_sample.shape
    depth = p['w_in'].shape[0]
    past = page_table.shape[1] * PAGE_SIZE
    gp = _Group(bp, tp, 0, 0)
    gs = _Group(bs, ts, bp * tp, past)
    n_p = bp * tp
    w_in = _pack_w_in(p['w_in'])
    w_out = p['w_out'].astype(BF16)
    w13 = (p['ffn1_w13'].astype(BF16), p['ffn2_w13'].astype(BF16))
    w2 = (p['ffn1_w2'].astype(BF16), p['ffn2_w2'].astype(BF16))
    per_layer = ('mix_norm', 'nsa_q_norm', 'nsa_k_norm', 'nsa_cmp_pos', 'nsa_cmp_w1', 'nsa_cmp_w2', 'diff_q_norm',
                 'diff_k_norm', 'diff_lambda', 'diff_out_norm', 'ml_conv_w', 'ml_conv_b', 'ml_gate_b', 'ml_out_norm')
    cache_nsa = _page_view(cache_nsa)
    cache_diff = _page_view(cache_diff)
    win_state = state_win.reshape(state_win.shape[:3] + (2 * NSA_KV_W,))
    pt = page_table.reshape(-1)
    zeros = lambda *s: jnp.zeros(s, F32)
    p_state = (zeros(bp, MLSTM_HEADS, MLSTM_DIM, MLSTM_DIM), zeros(bp, MLSTM_HEADS, MLSTM_DIM),
               zeros(bp, MLSTM_HEADS), zeros(bp, CONV_WIDTH - 1, 2 * MLSTM_W))
    p_win0 = zeros(bp, 0, 2, NSA_KV_HEADS, HEAD_DIM)
    row_arrays = None

    x = jnp.concatenate([x_prompt.reshape(n_p, d), x_sample.reshape(bs * ts, d)], axis=0)
    outs = []
    for l in range(depth):
        lp = {name: p[name][l] for name in per_layer}
        lam_init = 0.8 - 0.6 * math.exp(-0.3 * l)
        lam_p = lp['diff_lambda']
        lam = (jnp.exp(jnp.sum(lam_p[0] * lam_p[1])) - jnp.exp(jnp.sum(lam_p[2] * lam_p[3])) + lam_init).reshape(1)
        dgain = lp['diff_out_norm'].reshape(1, HEAD_DIM)
        h = _ffn(x, p['ffn1_norm'][l], w13[0], w2[0], l)
        u = _inproj(h, lp['mix_norm'], w_in, _u_gain(lp), l)

        summ_p, n_cmp_p = _compress_prompt(u, gp, lp)
        oc_p, sel_p = _cmp_select(u, gp, summ_p, n_cmp_p)
        nsa_p = _nsa_prompt(u, gp, oc_p, sel_p)
        diff_p = _diff_prompt(u, gp, lam, dgain, 1.0 - lam_init)
        ml_p, c_p, nn_p, m_p = _mlstm(u, gp, lp, *p_state)

        summ_s, n_cmp_s, slc_kv = _compress_sample(u, gs, lp, cache_nsa, pt, l)
        oc_s, sel_s = _cmp_select(u, gs, summ_s, n_cmp_s)
        nsa_s = _nsa_sample(u, gs, oc_s, sel_s, slc_kv, win_state, l)
        diff_s = _diff_sample(u, gs, lam, dgain, 1.0 - lam_init, cache_diff, pt, l)
        ml_s, c_s, nn_s, m_s = _mlstm(u, gs, lp, state_ml_c[l], state_ml_n[l], state_ml_m[l], state_ml_conv[l])

        h = _outproj(h, jnp.concatenate([nsa_p, nsa_s]), jnp.concatenate([diff_p, diff_s]),
                     jnp.concatenate([ml_p, ml_s]), w_out, l)
        x = _ffn(h, p['ffn2_norm'][l], w13[1], w2[1], l)

        row_arrays = _cache_rows(u, l, depth, row_arrays)
        win_shape = (2, NSA_KV_HEADS, HEAD_DIM)
        keep_p, keep_s = min(WINDOW, tp), min(WINDOW, state_win.shape[2] + ts)
        win_new_p = _group_rows(u, gp, C_WIN, 2 * NSA_KV_W)[:, tp - min(keep_p, tp):]
        win_new_s = _group_rows(u, gs, C_WIN, 2 * NSA_KV_W)[:, ts - min(keep_s, ts):]
        win_rows_p = _tail_rows(p_win0, win_new_p.reshape(win_new_p.shape[:2] + win_shape), keep_p)
        win_rows_s = _tail_rows(state_win[l], win_new_s.reshape(win_new_s.shape[:2] + win_shape), keep_s)
        conv_p = _tail_rows(p_state[3], _group_rows(u, gp, C_MQ, 2 * MLSTM_W), CONV_WIDTH - 1)
        conv_s = _tail_rows(state_ml_conv[l], _group_rows(u, gs, C_MQ, 2 * MLSTM_W), CONV_WIDTH - 1)
        outs.append((win_rows_p, win_rows_s, c_p, c_s, nn_p, nn_s, m_p, m_s, conv_p, conv_s))
    stacked = [jnp.stack(z) for z in zip(*outs)]
    nsa_rows, diff_rows = row_arrays
    n_rows = n_p * ROW_GROUPS

    def split(a, shape):
        return (a[:, :n_rows].reshape((depth, bp, tp) + shape), a[:, n_rows:].reshape((depth, bs, ts) + shape))

    return ((x[:n_p].reshape(bp, tp, d), x[n_p:].reshape(bs, ts, d)) + split(nsa_rows, (4, NSA_KV_HEADS, HEAD_DIM))
            + split(diff_rows, (2, DIFF_HEADS, HEAD_DIM)) + tuple(stacked))


def kernel(x_prompt, x_sample, cache_nsa, cache_diff, state_win, state_ml_c, state_ml_n, state_ml_m, state_ml_conv,
           page_table, ffn1_norm, ffn1_w13, ffn1_w2, mix_norm, w_in, nsa_q_norm, nsa_k_norm, nsa_cmp_pos, nsa_cmp_w1,
           nsa_cmp_w2, diff_q_norm, diff_k_norm, diff_lambda, diff_out_norm, ml_conv_w, ml_conv_b, ml_gate_b,
           ml_out_norm, w_out, ffn2_norm, ffn2_w13, ffn2_w2):
    params = dict(ffn1_norm=ffn1_norm, ffn1_w13=ffn1_w13, ffn1_w2=ffn1_w2, mix_norm=mix_norm, w_in=w_in,
                  nsa_q_norm=nsa_q_norm, nsa_k_norm=nsa_k_norm, nsa_cmp_pos=nsa_cmp_pos, nsa_cmp_w1=nsa_cmp_w1,
                  nsa_cmp_w2=nsa_cmp_w2, diff_q_norm=diff_q_norm, diff_k_norm=diff_k_norm, diff_lambda=diff_lambda,
                  diff_out_norm=diff_out_norm, ml_conv_w=ml_conv_w, ml_conv_b=ml_conv_b, ml_gate_b=ml_gate_b,
                  ml_out_norm=ml_out_norm, w_out=w_out, ffn2_norm=ffn2_norm, ffn2_w13=ffn2_w13, ffn2_w2=ffn2_w2)
    return _trunk(x_prompt, x_sample, cache_nsa, cache_diff, state_win, state_ml_c, state_ml_n, state_ml_m,
                  state_ml_conv, page_table, params)
```

```python
import functools
import math
from typing import NamedTuple

import numpy as np
import jax
import jax.numpy as jnp
from jax import lax
from jax.experimental import pallas as pl
from jax.experimental.pallas import tpu as pltpu

F32 = jnp.float32
BF16 = jnp.bfloat16
HIGHEST = lax.Precision.HIGHEST
NT = (((1,), (1,)), ((), ()))

D_MODEL = 2048
D_FF = 5632
PAGE_SIZE = 128
HEAD_DIM = 128
ROW_GROUPS = 8
NSA_HEADS = 8
NSA_KV_HEADS = 2
NSA_GROUP = NSA_HEADS // NSA_KV_HEADS
CMP_BLOCK = 32
CMP_STRIDE = 16
SLC_BLOCK = 64
SLC_SHIFT = 6
SLC_TOPN = 16
WINDOW = 512
DIFF_HEADS = 4
DIFF_QK_DIM = HEAD_DIM // 2
MLSTM_HEADS = 4
MLSTM_DIM = 128
MLSTM_CHUNK = 64
CONV_WIDTH = 4
NORM_EPS = 1e-6
NEG_INF = -1e30
M_INIT = -1e29
LOG2E = 1.4426950408889634
FORCE_BONUS = 1e3

NSA_W = NSA_HEADS * HEAD_DIM
NSA_KV_W = NSA_KV_HEADS * HEAD_DIM
DIFF_W = DIFF_HEADS * HEAD_DIM
MLSTM_W = MLSTM_HEADS * MLSTM_DIM
IN_SPLITS = (NSA_W, 6 * NSA_KV_W, 3 * NSA_HEADS, DIFF_W, DIFF_W, DIFF_W,
             MLSTM_W, MLSTM_W, MLSTM_W, MLSTM_HEADS, MLSTM_HEADS, MLSTM_W)
IN_COLS = sum(IN_SPLITS)

V7X_LANES = 128
V7X_SUBLANES = 8
V7X_VMEM_LIMIT_BYTES = 56 * 1024 * 1024

U_TILE = 256
C_NQ = 0
C_ROWS = C_NQ + NSA_W
C_WIN = C_ROWS + 4 * NSA_KV_W
C_DQ = C_WIN + 2 * NSA_KV_W
C_DROWS = C_DQ + DIFF_W
C_MQ = C_DROWS + 2 * DIFF_W
C_MK = C_MQ + MLSTM_W
C_MV = C_MK + MLSTM_W
C_MO = C_MV + MLSTM_W
C_SMALL = C_MO + MLSTM_W
U_COLS = C_SMALL + U_TILE
U_STEP = 5 * U_TILE
G_NSA, G_MI, G_MF = 0, 3 * NSA_HEADS, 3 * NSA_HEADS + MLSTM_HEADS
U_KINDS = (1, 1, 1, 1, 0, 0, 1, 0, 1, 0, 2, 2, 2, 2, 0, 0, 0, 0, 0, 0, 0, 0, 0, 0, 0)
_MAIN_ORDER = (0, 1, 3, 4, 5, 6, 7, 8, 11)
_SMALL_ORDER = (2, 9, 10)


class _Group(NamedTuple):
    b: int
    t: int
    row0: int
    past: int


def _round_up(n, m):
    return -(-n // m) * m


def _cparams(semantics):
    return pltpu.CompilerParams(dimension_semantics=semantics, vmem_limit_bytes=V7X_VMEM_LIMIT_BYTES)


def _pick_tile(n, prefs):
    for p in prefs:
        if n % p == 0:
            return p
    return n


def _alibi(n):
    return [float(2.0 ** (-8.0 * i / n)) for i in range(1, n + 1)]


def _rms_rows(x, g):
    inv = lax.rsqrt(jnp.mean(x * x, axis=-1, keepdims=True) + NORM_EPS)
    return x * inv * g


def _ffn_body(x_ref, g_ref, wa_ref, wb_ref, w2_ref, o_ref, n_ref):
    @pl.when(pl.program_id(1) == 0)
    def _():
        x = x_ref[...]
        n_ref[...] = _rms_rows(x, g_ref[...]).astype(BF16)
        o_ref[...] = x

    n = n_ref[...]
    a = jnp.dot(n, wa_ref[...], preferred_element_type=F32)
    b = jnp.dot(n, wb_ref[...], preferred_element_type=F32)
    act = (0.5 * a * jax.nn.sigmoid(a) * b).astype(BF16)
    o_ref[...] += jnp.dot(act, w2_ref[...], preferred_element_type=F32)


def _ffn(x, g, w13, w2, layer):
    t, d = x.shape
    f = w2.shape[1]
    tm = _pick_tile(t, (768, 512, 256, 128, 64, 32, 16, 8))
    tf = _pick_tile(f, (512, 256, 128))
    nf = f // tf
    return pl.pallas_call(
        _ffn_body,
        grid=(t // tm, nf),
        in_specs=[
            pl.BlockSpec((tm, d), lambda i, j: (i, 0)),
            pl.BlockSpec((1, d), lambda i, j: (0, 0)),
            pl.BlockSpec((None, d, tf), lambda i, j: (layer, 0, j)),
            pl.BlockSpec((None, d, tf), lambda i, j: (layer, 0, j + nf)),
            pl.BlockSpec((None, tf, d), lambda i, j: (layer, j, 0)),
        ],
        out_specs=pl.BlockSpec((tm, d), lambda i, j: (i, 0)),
        out_shape=jax.ShapeDtypeStruct((t, d), F32),
        scratch_shapes=[pltpu.VMEM((tm, d), BF16)],
        compiler_params=_cparams(("parallel", "arbitrary")),
        name="ffn",
    )(x, g.reshape(1, d), w13, w13, w2)


def _inproj_body(kind_ref, x_ref, g_ref, w_ref, gain_ref, o_ref, n_ref):
    j = pl.program_id(1)

    @pl.when(j == 0)
    def _():
        n_ref[...] = _rms_rows(x_ref[...], g_ref[...]).astype(BF16)

    y = jnp.dot(n_ref[...], w_ref[...], preferred_element_type=F32)
    gain = gain_ref[...]
    for i in range(U_STEP // U_TILE):
        kind = kind_ref[j * (U_STEP // U_TILE) + i]
        halves = [slice(i * U_TILE + h * HEAD_DIM, i * U_TILE + (h + 1) * HEAD_DIM)
                  for h in range(U_TILE // HEAD_DIM)]

        @pl.when(kind == 0)
        def _(halves=halves):
            for cols in halves:
                o_ref[:, cols] = y[:, cols]

        @pl.when(kind == 1)
        def _(halves=halves):
            for cols in halves:
                o_ref[:, cols] = _rms_rows(y[:, cols], gain[:, cols])

        @pl.when(kind == 2)
        def _(halves=halves):
            low = lax.broadcasted_iota(jnp.int32, (y.shape[0], HEAD_DIM), 1) < DIFF_QK_DIM
            for cols in halves:
                yh = y[:, cols]
                sq = yh * yh
                s_lo = jnp.sum(jnp.where(low, sq, 0.0), axis=-1, keepdims=True)
                s_hi = jnp.sum(jnp.where(low, 0.0, sq), axis=-1, keepdims=True)
                inv = lax.rsqrt(jnp.where(low, s_lo, s_hi) * (1.0 / DIFF_QK_DIM) + NORM_EPS)
                o_ref[:, cols] = yh * inv * gain[:, cols]


def _inproj(x, g, w, gain, layer):
    t, d = x.shape
    tm = _pick_tile(t, (768, 512, 256, 128, 64, 32, 16, 8))
    return pl.pallas_call(
        _inproj_body,
        grid=(t // tm, U_COLS // U_STEP),
        in_specs=[
            pl.BlockSpec(memory_space=pltpu.SMEM),
            pl.BlockSpec((tm, d), lambda i, j: (i, 0)),
            pl.BlockSpec((1, d), lambda i, j: (0, 0)),
            pl.BlockSpec((None, d, U_STEP), lambda i, j: (layer, 0, j)),
            pl.BlockSpec((1, U_STEP), lambda i, j: (0, j)),
        ],
        out_specs=pl.BlockSpec((tm, U_STEP), lambda i, j: (i, j)),
        out_shape=jax.ShapeDtypeStruct((t, U_COLS), F32),
        scratch_shapes=[pltpu.VMEM((tm, d), BF16)],
        compiler_params=_cparams(("parallel", "arbitrary")),
        name="inproj",
    )(jnp.asarray(U_KINDS, jnp.int32), x, g.reshape(1, d), w, gain)


def _outproj_body(h_ref, a_ref, b_ref, c_ref, w_ref, y_ref):
    ka, kb = a_ref.shape[1], b_ref.shape[1]
    y = h_ref[...] + jnp.dot(a_ref[...].astype(BF16), w_ref[0:ka, :], preferred_element_type=F32)
    y = y + jnp.dot(b_ref[...].astype(BF16), w_ref[ka:ka + kb, :], preferred_element_type=F32)
    y_ref[...] = y + jnp.dot(c_ref[...].astype(BF16), w_ref[ka + kb:, :], preferred_element_type=F32)


def _outproj(h, o_nsa, o_diff, o_ml, w, layer):
    t, d = h.shape
    tm = _pick_tile(t, (768, 512, 256, 128, 64, 32, 16, 8))
    row = lambda a: pl.BlockSpec((tm, a.shape[1]), lambda i: (i, 0))
    return pl.pallas_call(
        _outproj_body,
        grid=(t // tm,),
        in_specs=[row(h), row(o_nsa), row(o_diff), row(o_ml),
                  pl.BlockSpec((None,) + w.shape[1:], lambda i: (layer, 0, 0))],
        out_specs=row(h),
        out_shape=jax.ShapeDtypeStruct((t, d), F32),
        compiler_params=_cparams(("parallel",)),
        name="outproj",
    )(h, o_nsa, o_diff, o_ml, w)


def _online_update(s, bias, krel, slopes, v, m_ref, l_ref, acc_ref, base, tq):
    def shift_of(slope):
        shift = (slope * LOG2E) * krel
        return jnp.broadcast_to(shift, (tq, shift.shape[1])) if bias is None else bias + shift

    rows = slice(base, base + len(slopes) * tq)
    p, alpha = _online_softmax(s, jnp.concatenate([shift_of(sl) for sl in slopes], axis=0), m_ref, l_ref, rows)
    acc_ref[rows] = alpha * acc_ref[rows] + jnp.dot(p, v, preferred_element_type=F32)


def _online_softmax(s, shift, m_ref, l_ref, rows):
    sg = s + shift
    m_old = m_ref[rows]
    m_new = jnp.maximum(m_old, jnp.max(sg, axis=-1, keepdims=True))
    p = jnp.exp2(sg - pltpu.repeat(m_new, sg.shape[1] // V7X_LANES, axis=1))
    alpha = jnp.exp2(m_old - m_new)
    l_ref[rows] = alpha * l_ref[rows] + jnp.sum(p, axis=-1, keepdims=True)
    m_ref[rows] = m_new
    return p.astype(BF16), alpha


def _init_state(m_ref, l_ref, acc_ref):
    m_ref[...] = jnp.full(m_ref.shape, M_INIT, F32)
    l_ref[...] = jnp.zeros(l_ref.shape, F32)
    acc_ref[...] = jnp.zeros(acc_ref.shape, F32)


def _mask_bias(keep):
    return jnp.where(keep, 0.0, NEG_INF)


def _key_offsets(n, first):
    return (first + lax.broadcasted_iota(jnp.int32, (1, n), 1)).astype(F32)


def _normalised(l_ref, acc_ref, rows):
    return acc_ref[rows] / jnp.maximum(l_ref[rows], 1e-30)


def _distance(tq, n, offset):
    return offset + lax.broadcasted_iota(jnp.int32, (tq, n), 0) - lax.broadcasted_iota(jnp.int32, (tq, n), 1)


def _block_expand(n_spad, n, first_key):
    blk = lax.broadcasted_iota(jnp.int32, (n_spad, n), 0)
    key = first_key + lax.broadcasted_iota(jnp.int32, (n_spad, n), 1)
    return jnp.where(lax.shift_right_logical(key, SLC_SHIFT) == blk, 1.0, 0.0).astype(BF16)


def _stack_heads(q_ref, k):
    return jnp.concatenate([q_ref[:, (k * NSA_GROUP + g) * HEAD_DIM:(k * NSA_GROUP + g + 1) * HEAD_DIM]
                            for g in range(NSA_GROUP)], axis=0).astype(BF16)


def _pad_keys(x, n):
    return jnp.concatenate([x, jnp.zeros((n - x.shape[0], x.shape[1]), x.dtype)], axis=0)


def _nsa_combine(gt_ref, oc_ref, o_ref, l_ref, acc_ref, tq):
    gt = jax.nn.sigmoid(gt_ref[:, 0:V7X_LANES])
    for h in range(NSA_HEADS):
        cols = slice(h * HEAD_DIM, (h + 1) * HEAD_DIM)
        o_slc = _normalised(l_ref, acc_ref, slice(h * tq, (h + 1) * tq))
        o_win = _normalised(l_ref, acc_ref, slice((NSA_HEADS + h) * tq, (NSA_HEADS + h + 1) * tq))
        c0 = G_NSA + 3 * h
        o_ref[:, cols] = gt[:, c0:c0 + 1] * oc_ref[:, cols] + gt[:, c0 + 1:c0 + 2] * o_slc + gt[:, c0 + 2:c0 + 3] * o_win


_NSA_SLOPES = _alibi(NSA_HEADS)
_DIFF_SLOPES = _alibi(DIFF_HEADS)


def _softmax_state(rows):
    return [pltpu.VMEM((rows, V7X_LANES), F32), pltpu.VMEM((rows, V7X_LANES), F32), pltpu.VMEM((rows, HEAD_DIM), F32)]


def _nsa_state(tq):
    return _softmax_state(2 * NSA_HEADS * tq)


def _diff_state(tq):
    return _softmax_state(2 * DIFF_HEADS * tq)


def _cmp_accumulate(rows_of, w_ref, o_ref):
    for c in range(4):
        for r in range(CMP_STRIDE):
            part = jnp.dot(rows_of(c, r).astype(BF16), w_ref[c // 2, r * HEAD_DIM:(r + 1) * HEAD_DIM, :],
                           preferred_element_type=F32)
            if r == 0:
                o_ref[0, c] = part
            else:
                o_ref[0, c] += part


def _cmp_partial_body(x0_ref, x1_ref, x2_ref, x3_ref, w_ref, o_ref):
    srcs = (x0_ref, x1_ref, x2_ref, x3_ref)
    per = x0_ref.shape[0] // CMP_STRIDE
    _cmp_accumulate(lambda c, r: srcs[c][pl.ds(r, per, stride=CMP_STRIDE), :], w_ref, o_ref)


def _cmp_partial_paged_body(pt_ref, *refs, pps):
    del pt_ref
    pages, w_ref, o_ref, kv_ref, slab_ref = refs[:pps], refs[pps], refs[pps + 1], refs[pps + 2], refs[pps + 3]
    for i, pg in enumerate(pages):
        for c in range(4):
            slab_ref[c, i * PAGE_SIZE:(i + 1) * PAGE_SIZE, :] = _page_rows(pg, c)
    per = pps * PAGE_SIZE // CMP_STRIDE
    _cmp_accumulate(lambda c, r: slab_ref[c, pl.ds(r, per, stride=CMP_STRIDE), :], w_ref, o_ref)
    for j in range(4):
        kv_ref[0, j] = _page_piece(pages, 4 + j)


def _cmp_finish_body(*refs, has_tail, n_out):
    if has_tail:
        p_ref, tail_ref, w1_ref, pos_ref, w2_ref, g_ref, o_ref = refs
    else:
        p_ref, w1_ref, pos_ref, w2_ref, g_ref, o_ref = refs
    c = pl.program_id(0)
    p = p_ref[0, 0]
    n = p.shape[0]
    w1 = w1_ref[0]
    pb = jnp.dot(pos_ref[0].astype(BF16), w1, preferred_element_type=F32)
    nxt = pltpu.roll(p[:, HEAD_DIM:], n - 1, 0)
    if has_tail:
        rid = lax.broadcasted_iota(jnp.int32, (V7X_SUBLANES, HEAD_DIM), 0)
        tail8 = jnp.zeros((V7X_SUBLANES, HEAD_DIM), F32)
        xt = tail_ref[...].astype(BF16)
        for r in range(tail_ref.shape[0]):
            d = jnp.dot(xt, w1[r * HEAD_DIM:(r + 1) * HEAD_DIM, HEAD_DIM:], preferred_element_type=F32)
            tail8 = tail8 + jnp.where(rid == r, d, 0.0)
        tail = jnp.sum(tail8, axis=0, keepdims=True)
    else:
        tail = jnp.zeros((1, HEAD_DIM), F32)
    last = lax.broadcasted_iota(jnp.int32, (n, HEAD_DIM), 0) == n - 1
    z = p[:, :HEAD_DIM] + pb[0:1, :HEAD_DIM] + jnp.where(last, tail, nxt) + pb[1:2, HEAD_DIM:]
    y = jnp.dot((z * jax.nn.sigmoid(z)).astype(BF16), w2_ref[0].astype(BF16), preferred_element_type=F32)
    y = jnp.where(c < NSA_KV_HEADS, _rms_rows(y, g_ref[...]), y)
    o_ref[0, 0, 0:n] = y
    if n_out > n:
        o_ref[0, 0, n:n_out] = jnp.zeros((n_out - n, HEAD_DIM), F32)


def _cmp_weights(lp):
    half = CMP_STRIDE * HEAD_DIM
    w1 = lp['nsa_cmp_w1']
    w1ab = jnp.concatenate([w1[:, :half], w1[:, half:]], axis=2).astype(BF16)
    pos = lp['nsa_cmp_pos']
    pos2 = jnp.concatenate([pos[:, :CMP_STRIDE].reshape(2, 1, half), pos[:, CMP_STRIDE:].reshape(2, 1, half),
                            jnp.zeros((2, V7X_SUBLANES - 2, half), F32)], axis=1)
    return w1ab, pos2


def _cmp_finish(p, tail_src, w1ab, pos2, lp, n_out, grp, u):
    b, _, n, _ = p.shape
    half = CMP_STRIDE * HEAD_DIM
    has_tail = tail_src is not None
    in_specs = [pl.BlockSpec((1, 1, n, 2 * HEAD_DIM), lambda c, i: (i, c, 0, 0))]
    args = [p]
    if has_tail:
        rb = grp.row0 // grp.t
        in_specs.append(pl.BlockSpec((grp.t, HEAD_DIM), lambda c, i: (rb + i, C_ROWS // HEAD_DIM + c)))
        args.append(u)
    in_specs += [
        pl.BlockSpec((1, half, 2 * HEAD_DIM), lambda c, i: (c // NSA_KV_HEADS, 0, 0)),
        pl.BlockSpec((1, V7X_SUBLANES, half), lambda c, i: (c // NSA_KV_HEADS, 0, 0)),
        pl.BlockSpec((1, HEAD_DIM, HEAD_DIM), lambda c, i: (c // NSA_KV_HEADS, 0, 0)),
        pl.BlockSpec((1, HEAD_DIM), lambda c, i: (0, 0)),
    ]
    args += [w1ab, pos2, lp['nsa_cmp_w2'], lp['nsa_k_norm'][0:1]]
    return pl.pallas_call(
        functools.partial(_cmp_finish_body, has_tail=has_tail, n_out=n_out),
        grid=(4, b),
        in_specs=in_specs,
        out_specs=pl.BlockSpec((1, 1, n_out, HEAD_DIM), lambda c, i: (i, c, 0, 0)),
        out_shape=jax.ShapeDtypeStruct((b, 4, n_out, HEAD_DIM), F32),
        compiler_params=_cparams(("parallel", "parallel")),
        name="nsa_cmp_finish",
    )(*args)


def _compress_prompt(u, grp, lp):
    assert grp.past == 0 and grp.row0 == 0 and grp.t % (CMP_STRIDE * V7X_SUBLANES) == 0
    w1ab, pos2 = _cmp_weights(lp)
    n = grp.t // CMP_STRIDE
    group = lambda c: pl.BlockSpec((grp.t, HEAD_DIM), lambda i: (i, C_ROWS // HEAD_DIM + c))
    p = pl.pallas_call(
        _cmp_partial_body,
        grid=(grp.b,),
        in_specs=[group(0), group(1), group(2), group(3), pl.BlockSpec(w1ab.shape, lambda i: (0, 0, 0))],
        out_specs=pl.BlockSpec((1, 4, n, 2 * HEAD_DIM), lambda i: (i, 0, 0, 0)),
        out_shape=jax.ShapeDtypeStruct((grp.b, 4, n, 2 * HEAD_DIM), F32),
        compiler_params=_cparams(("parallel",)),
        name="nsa_cmp_partial",
    )(u, u, u, u, w1ab)
    return _cmp_finish(p, None, w1ab, pos2, lp, _round_up(n, V7X_LANES), grp, u), n - 1


def _pages_per_step(n_pages):
    return _pick_tile(n_pages, (16, 8, 4, 2, 1))


def _page_view(cache):
    assert cache.shape[3] * cache.shape[4] == ROW_GROUPS and cache.shape[5] == HEAD_DIM
    return cache.reshape(cache.shape[0], cache.shape[1], PAGE_SIZE * ROW_GROUPS, HEAD_DIM)


def _page_specs(layer, n_pages, pps):
    def spec(i):
        return pl.BlockSpec((1, 1, PAGE_SIZE * ROW_GROUPS, HEAD_DIM),
                            lambda b, c, pt: (layer, pt[b * n_pages + c * pps + i], 0, 0))
    return [spec(i) for i in range(pps)]


def _page_rows(page, j):
    return page[0, 0, pl.ds(j, PAGE_SIZE, stride=ROW_GROUPS), :]


def _page_piece(pages, j):
    return jnp.concatenate([_page_rows(pg, j).astype(BF16) for pg in pages], axis=0)


def _compress_sample(u, grp, lp, cache, page_table, layer):
    assert grp.past % PAGE_SIZE == 0 and grp.t < CMP_STRIDE
    w1ab, pos2 = _cmp_weights(lp)
    n_pages = grp.past // PAGE_SIZE
    pps = _pages_per_step(n_pages)
    per = pps * PAGE_SIZE // CMP_STRIDE
    n = grp.past // CMP_STRIDE
    p, slc_kv = pl.pallas_call(
        functools.partial(_cmp_partial_paged_body, pps=pps),
        grid_spec=pltpu.PrefetchScalarGridSpec(
            num_scalar_prefetch=1,
            grid=(grp.b, n_pages // pps),
            in_specs=_page_specs(layer, n_pages, pps) + [
                pl.BlockSpec(w1ab.shape, lambda b, c, pt: (0, 0, 0))],
            out_specs=[
                pl.BlockSpec((1, 4, per, 2 * HEAD_DIM), lambda b, c, pt: (b, 0, c, 0)),
                pl.BlockSpec((1, 4, pps * PAGE_SIZE, HEAD_DIM), lambda b, c, pt: (b, 0, c, 0)),
            ],
            scratch_shapes=[pltpu.VMEM((4, pps * PAGE_SIZE, HEAD_DIM), F32)],
        ),
        out_shape=[
            jax.ShapeDtypeStruct((grp.b, 4, n, 2 * HEAD_DIM), F32),
            jax.ShapeDtypeStruct((grp.b, 4, grp.past, HEAD_DIM), BF16),
        ],
        compiler_params=_cparams(("parallel", "arbitrary")),
        name="nsa_cmp_partial_paged",
    )(page_table, *([cache] * pps), w1ab)
    return _cmp_finish(p, u, w1ab, pos2, lp, _round_up(n + 1, V7X_LANES), grp, u), n, slc_kv


def _cmp_select_body(q_ref, kv_ref, o_ref, sel_ref, *, tq, q_off, n_cmp, n_slc, n_sel):
    qi = pl.program_id(1)
    n_cpad = kv_ref.shape[2]
    n_spad = sel_ref.shape[1] // NSA_KV_HEADS
    q_lo = q_off + qi * tq
    qpos = q_lo + lax.broadcasted_iota(jnp.int32, (tq, n_cpad), 0)
    ends = lax.broadcasted_iota(jnp.int32, (tq, n_cpad), 1) * CMP_STRIDE + (CMP_BLOCK - 1)
    d = qpos - ends
    mask = d >= 0
    df = d.astype(F32)
    c_lo = lax.broadcasted_iota(jnp.int32, (n_cpad, n_spad), 0) * CMP_STRIDE
    j_lo = lax.broadcasted_iota(jnp.int32, (n_cpad, n_spad), 1) * SLC_BLOCK
    cover = jnp.where(c_lo < j_lo + SLC_BLOCK, jnp.where(c_lo + CMP_BLOCK > j_lo, 1.0, 0.0), 0.0)
    cover = jnp.where(c_lo < n_cmp * CMP_STRIDE, cover, 0.0)
    qp = q_lo + lax.broadcasted_iota(jnp.int32, (tq, n_spad), 0)
    j = lax.broadcasted_iota(jnp.int32, (tq, n_spad), 1)
    cur = lax.shift_right_logical(qp, SLC_SHIFT)
    bonus = FORCE_BONUS * jnp.where(j == 0, 1.0, jnp.where(j == cur, 1.0, jnp.where(j == cur - 1, 1.0, 0.0)))
    valid = j * SLC_BLOCK <= qp
    for k in range(NSA_KV_HEADS):
        q = jnp.concatenate([q_ref[:, (k * NSA_GROUP + g) * HEAD_DIM:(k * NSA_GROUP + g + 1) * HEAD_DIM]
                             for g in range(NSA_GROUP)], axis=0)
        s = lax.dot_general(q, kv_ref[0, k], NT, precision=HIGHEST, preferred_element_type=F32)
        vc = kv_ref[0, NSA_KV_HEADS + k].astype(BF16)
        psum = jnp.zeros((tq, n_cpad), F32)
        for g in range(NSA_GROUP):
            h = k * NSA_GROUP + g
            sg = jnp.where(mask, s[g * tq:(g + 1) * tq] - (_NSA_SLOPES[h] * LOG2E) * df, NEG_INF)
            m = jnp.max(sg, axis=-1, keepdims=True)
            p = jnp.where(mask, jnp.exp2(sg - m), 0.0)
            p = p / jnp.maximum(jnp.sum(p, axis=-1, keepdims=True), 1e-30)
            o_ref[:, h * HEAD_DIM:(h + 1) * HEAD_DIM] = jnp.dot(p.astype(BF16), vc, preferred_element_type=F32)
            psum = psum + p
        imp = jnp.dot(psum, cover, precision=HIGHEST, preferred_element_type=F32)
        score = jnp.where(valid, imp + bonus, NEG_INF)
        cnt = jnp.zeros((tq, n_spad), F32)
        for i in range(n_slc):
            ci = score[:, i:i + 1]
            cnt = cnt + jnp.where(ci > score, 1.0, jnp.where(ci == score, jnp.where(j > i, 1.0, 0.0), 0.0))
        sel_ref[:, k * n_spad:(k + 1) * n_spad] = jnp.where(cnt < n_sel, jnp.where(j < n_slc, 0.0, NEG_INF), NEG_INF)


def _cmp_select(u, grp, summaries, n_cmp):
    n_slc = -(-(grp.past + grp.t) // SLC_BLOCK)
    n_spad = _round_up(n_slc, V7X_LANES)
    n_cpad = summaries.shape[2]
    tq = _pick_tile(grp.t, (256, 128, 64, 32, 16, 8))
    nq = grp.t // tq
    rb = grp.row0 // tq
    body = functools.partial(_cmp_select_body, tq=tq, q_off=grp.past, n_cmp=n_cmp, n_slc=n_slc,
                             n_sel=min(SLC_TOPN, n_slc))
    return pl.pallas_call(
        body,
        grid=(grp.b, nq),
        in_specs=[
            pl.BlockSpec((tq, NSA_W), lambda b, i: (rb + b * nq + i, 0)),
            pl.BlockSpec((1, 4, n_cpad, HEAD_DIM), lambda b, i: (b, 0, 0, 0)),
        ],
        out_specs=[
            pl.BlockSpec((tq, NSA_W), lambda b, i: (b * nq + i, 0)),
            pl.BlockSpec((tq, NSA_KV_HEADS * n_spad), lambda b, i: (b * nq + i, 0)),
        ],
        out_shape=[
            jax.ShapeDtypeStruct((grp.b * grp.t, NSA_W), F32),
            jax.ShapeDtypeStruct((grp.b * grp.t, NSA_KV_HEADS * n_spad), F32),
        ],
        compiler_params=_cparams(("parallel", "parallel")),
        name="nsa_cmp_select",
    )(u, summaries)


def _nsa_prompt_body(q_ref, ks_ref, kw_ref, sel_ref, oc_ref, gt_ref, o_ref, m_ref, l_ref, acc_ref, *, tq, tk):
    qi = pl.program_id(1)
    kj = pl.program_id(2)
    n_spad = sel_ref.shape[1] // NSA_KV_HEADS

    @pl.when(kj == 0)
    def _():
        _init_state(m_ref, l_ref, acc_ref)

    q_lo = qi * tq
    k_lo = kj * tk
    slc_on = k_lo <= q_lo + (tq - 1)
    win_on = jnp.logical_and(slc_on, k_lo + (tk - 1) >= q_lo - WINDOW)

    krel = _key_offsets(tk, k_lo - q_lo)

    @pl.when(slc_on)
    def _():
        causal = _mask_bias(_distance(tq, tk, q_lo - k_lo) >= 0)
        expand = _block_expand(n_spad, tk, k_lo)
        kv = ks_ref[...].astype(BF16)
        for k in range(NSA_KV_HEADS):
            s = lax.dot_general(_stack_heads(q_ref, k), kv[:, k * HEAD_DIM:(k + 1) * HEAD_DIM], NT,
                                preferred_element_type=F32)
            chosen = jnp.dot(sel_ref[:, k * n_spad:(k + 1) * n_spad].astype(BF16), expand,
                             preferred_element_type=F32)
            _online_update(s, causal + chosen, krel, _NSA_SLOPES[k * NSA_GROUP:(k + 1) * NSA_GROUP],
                           kv[:, (NSA_KV_HEADS + k) * HEAD_DIM:(NSA_KV_HEADS + k + 1) * HEAD_DIM],
                           m_ref, l_ref, acc_ref, k * NSA_GROUP * tq, tq)

    @pl.when(win_on)
    def _():
        d = _distance(tq, tk, q_lo - k_lo)
        bias = _mask_bias(jnp.logical_and(d >= 0, d <= WINDOW))
        kv = kw_ref[...].astype(BF16)
        for k in range(NSA_KV_HEADS):
            s = lax.dot_general(_stack_heads(q_ref, k), kv[:, k * HEAD_DIM:(k + 1) * HEAD_DIM], NT,
                                preferred_element_type=F32)
            _online_update(s, bias, krel, _NSA_SLOPES[k * NSA_GROUP:(k + 1) * NSA_GROUP],
                           kv[:, (NSA_KV_HEADS + k) * HEAD_DIM:(NSA_KV_HEADS + k + 1) * HEAD_DIM],
                           m_ref, l_ref, acc_ref, (NSA_KV_HEADS + k) * NSA_GROUP * tq, tq)

    @pl.when(kj == pl.num_programs(2) - 1)
    def _():
        _nsa_combine(gt_ref, oc_ref, o_ref, l_ref, acc_ref, tq)


def _nsa_prompt(u, grp, o_cmp, sel):
    assert grp.past == 0 and grp.row0 == 0
    t = grp.t
    tq = _pick_tile(t, (256, 128, 64, 32, 16, 8))
    tk = _pick_tile(t, (512, 256, 128))
    nq, nk = t // tq, t // tk
    kvw = 2 * NSA_KV_W

    def last_tile(i):
        return (i * tq + (tq - 1)) // tk

    def slc_rows(b, i, j):
        return (b * nk + jnp.minimum(j, last_tile(i)), (C_ROWS + kvw) // kvw)

    def win_rows(b, i, j):
        first = jnp.maximum(i * tq - WINDOW, 0) // tk
        return (b * nk + jnp.clip(j, first, last_tile(i)), C_WIN // kvw)

    return pl.pallas_call(
        functools.partial(_nsa_prompt_body, tq=tq, tk=tk),
        grid=(grp.b, nq, nk),
        in_specs=[
            pl.BlockSpec((tq, NSA_W), lambda b, i, j: (b * nq + i, 0)),
            pl.BlockSpec((tk, kvw), slc_rows),
            pl.BlockSpec((tk, kvw), win_rows),
            pl.BlockSpec((tq, sel.shape[1]), lambda b, i, j: (b * nq + i, 0)),
            pl.BlockSpec((tq, NSA_W), lambda b, i, j: (b * nq + i, 0)),
            pl.BlockSpec((tq, U_TILE), lambda b, i, j: (b * nq + i, C_SMALL // U_TILE)),
        ],
        out_specs=pl.BlockSpec((tq, NSA_W), lambda b, i, j: (b * nq + i, 0)),
        out_shape=jax.ShapeDtypeStruct((grp.b * t, NSA_W), F32),
        scratch_shapes=_nsa_state(tq),
        compiler_params=_cparams(("parallel", "parallel", "arbitrary")),
        name="nsa_attn_prompt",
    )(u, u, u, sel, o_cmp, u)


def _nsa_sample_body(q_ref, kv_ref, knew_ref, wst_ref, wnew_ref, sel_ref, oc_ref, gt_ref, o_ref, m_ref, l_ref, acc_ref,
                     *, past, n_win):
    c = pl.program_id(1)
    tq = q_ref.shape[0]
    n_spad = sel_ref.shape[1] // NSA_KV_HEADS
    n = kv_ref.shape[2]

    @pl.when(c == 0)
    def _():
        _init_state(m_ref, l_ref, acc_ref)

    def attend(keys_of, vals_of, bias_of, krel, branch):
        for k in range(NSA_KV_HEADS):
            s = lax.dot_general(_stack_heads(q_ref, k), keys_of(k), NT, preferred_element_type=F32)
            _online_update(s, bias_of(k), krel, _NSA_SLOPES[k * NSA_GROUP:(k + 1) * NSA_GROUP], vals_of(k),
                           m_ref, l_ref, acc_ref, (branch * NSA_KV_HEADS + k) * NSA_GROUP * tq, tq)

    def slab(ref, j):
        return ref[:, j * HEAD_DIM:(j + 1) * HEAD_DIM]

    first_key = c * n
    expand = _block_expand(n_spad, n, first_key)
    attend(lambda k: kv_ref[0, k], lambda k: kv_ref[0, NSA_KV_HEADS + k],
           lambda k: jnp.dot(sel_ref[:, k * n_spad:(k + 1) * n_spad].astype(BF16), expand,
                             preferred_element_type=F32),
           _key_offsets(n, first_key - past), 0)

    @pl.when(c == pl.num_programs(1) - 1)
    def _():
        npad = V7X_LANES
        real = lax.broadcasted_iota(jnp.int32, (tq, npad), 1) < tq
        causal = _mask_bias(jnp.logical_and(_distance(tq, npad, 0) >= 0, real))
        lane = lax.broadcasted_iota(jnp.int32, (tq, n_spad), 1)

        def chosen_new(k):
            col = jnp.where(lane == past // SLC_BLOCK, sel_ref[:, k * n_spad:(k + 1) * n_spad], 0.0)
            return causal + jnp.sum(col, axis=-1, keepdims=True)

        knew = _pad_keys(knew_ref[...], npad).astype(BF16)
        attend(lambda k: slab(knew, k), lambda k: slab(knew, NSA_KV_HEADS + k), chosen_new, _key_offsets(npad, 0), 0)
        wbuf = wst_ref[0, 0].astype(BF16)
        in_window = _mask_bias(_distance(tq, n_win, n_win) <= WINDOW)
        attend(lambda k: slab(wbuf, k), lambda k: slab(wbuf, NSA_KV_HEADS + k), lambda k: in_window,
               _key_offsets(n_win, -n_win), 1)
        wnew = _pad_keys(wnew_ref[...], npad).astype(BF16)
        attend(lambda k: slab(wnew, k), lambda k: slab(wnew, NSA_KV_HEADS + k), lambda k: causal,
               _key_offsets(npad, 0), 1)
        _nsa_combine(gt_ref, oc_ref, o_ref, l_ref, acc_ref, tq)


def _nsa_sample(u, grp, o_cmp, sel, slc_kv, win_state, layer):
    t = grp.t
    n_win = win_state.shape[2]
    assert n_win == WINDOW and grp.past % SLC_BLOCK == 0 and t <= SLC_BLOCK and grp.row0 % t == 0
    n = _pick_tile(grp.past, (2048, 1024, 512, 256, 128))
    kvw = 2 * NSA_KV_W
    rb = grp.row0 // t
    tok = lambda width, col: pl.BlockSpec((t, width), lambda b, c: (rb + b, col))
    own = lambda width: pl.BlockSpec((t, width), lambda b, c: (b, 0))
    return pl.pallas_call(
        functools.partial(_nsa_sample_body, past=grp.past, n_win=n_win),
        grid=(grp.b, grp.past // n),
        in_specs=[
            tok(NSA_W, 0),
            pl.BlockSpec((1, 4, n, HEAD_DIM), lambda b, c: (b, 0, c, 0)),
            tok(kvw, (C_ROWS + kvw) // kvw),
            pl.BlockSpec((1, 1, n_win, kvw), lambda b, c: (layer, b, 0, 0)),
            tok(kvw, C_WIN // kvw),
            own(sel.shape[1]),
            own(NSA_W),
            tok(U_TILE, C_SMALL // U_TILE),
        ],
        out_specs=own(NSA_W),
        out_shape=jax.ShapeDtypeStruct((grp.b * t, NSA_W), F32),
        scratch_shapes=_nsa_state(t),
        compiler_params=_cparams(("parallel", "arbitrary")),
        name="nsa_attn_cached",
    )(u, slc_kv, u, win_state, u, sel, o_cmp, u)


def _diff_update(q_ref, group_of, bias, krel, m_ref, l_ref, acc_ref):
    tq = q_ref.shape[0]
    low = lax.broadcasted_iota(jnp.int32, (tq, HEAD_DIM), 1) < DIFF_QK_DIM
    scores, shifts = [], []
    for h in range(DIFF_HEADS):
        qh = q_ref[:, h * HEAD_DIM:(h + 1) * HEAD_DIM]
        q2 = jnp.concatenate([jnp.where(low, qh, 0.0), jnp.where(low, 0.0, qh)], axis=0).astype(BF16)
        scores.append(lax.dot_general(q2, group_of(h), NT, preferred_element_type=F32))
        shift = (_DIFF_SLOPES[h] * LOG2E) * krel
        shift = jnp.broadcast_to(shift, (tq, shift.shape[1])) if bias is None else bias + shift
        shifts += [shift, shift]
    p, alpha = _online_softmax(jnp.concatenate(scores, axis=0), jnp.concatenate(shifts, axis=0), m_ref, l_ref,
                               slice(0, 2 * DIFF_HEADS * tq))
    for h in range(DIFF_HEADS):
        rows = slice(2 * h * tq, 2 * (h + 1) * tq)
        acc_ref[rows] = alpha[rows] * acc_ref[rows] + jnp.dot(p[rows], group_of(DIFF_HEADS + h),
                                                               preferred_element_type=F32)


def _diff_finish(lam_ref, g_ref, o_ref, l_ref, acc_ref, post):
    tq = o_ref.shape[0]
    lam = lam_ref[0]
    for h in range(DIFF_HEADS):
        o1 = _normalised(l_ref, acc_ref, slice(2 * h * tq, (2 * h + 1) * tq))
        o2 = _normalised(l_ref, acc_ref, slice((2 * h + 1) * tq, (2 * h + 2) * tq))
        o_ref[:, h * HEAD_DIM:(h + 1) * HEAD_DIM] = _rms_rows(o1 - lam * o2, g_ref[...]) * post


def _diff_prompt_body(lam_ref, q_ref, kv_ref, g_ref, o_ref, m_ref, l_ref, acc_ref, *, tq, tk, post):
    qi = pl.program_id(1)
    kj = pl.program_id(2)

    @pl.when(kj == 0)
    def _():
        _init_state(m_ref, l_ref, acc_ref)

    @pl.when(kj * tk <= qi * tq + (tq - 1))
    def _():
        kv = kv_ref[...].astype(BF16)
        _diff_update(q_ref, lambda j: kv[:, j * HEAD_DIM:(j + 1) * HEAD_DIM],
                     _mask_bias(_distance(tq, tk, qi * tq - kj * tk) >= 0), _key_offsets(tk, kj * tk - qi * tq),
                     m_ref, l_ref, acc_ref)

    @pl.when(kj == pl.num_programs(2) - 1)
    def _():
        _diff_finish(lam_ref, g_ref, o_ref, l_ref, acc_ref, post)


def _diff_prompt(u, grp, lam, gain, post):
    assert grp.past == 0 and grp.row0 == 0
    t = grp.t
    tq = _pick_tile(t, (256, 128, 64, 32, 16, 8))
    tk = _pick_tile(t, (512, 256, 128))
    nq, nk = t // tq, t // tk
    return pl.pallas_call(
        functools.partial(_diff_prompt_body, tq=tq, tk=tk, post=post),
        grid=(grp.b, nq, nk),
        in_specs=[
            pl.BlockSpec(memory_space=pltpu.SMEM),
            pl.BlockSpec((tq, DIFF_W), lambda b, i, j: (b * nq + i, C_DQ // DIFF_W)),
            pl.BlockSpec((tk, 2 * DIFF_W),
                         lambda b, i, j: (b * nk + jnp.minimum(j, (i * tq + (tq - 1)) // tk), C_DROWS // (2 * DIFF_W))),
            pl.BlockSpec((1, HEAD_DIM), lambda b, i, j: (0, 0)),
        ],
        out_specs=pl.BlockSpec((tq, DIFF_W), lambda b, i, j: (b * nq + i, 0)),
        out_shape=jax.ShapeDtypeStruct((grp.b * t, DIFF_W), F32),
        scratch_shapes=_diff_state(tq),
        compiler_params=_cparams(("parallel", "parallel", "arbitrary")),
        name="diff_attn_prompt",
    )(lam, u, u, gain)


def _diff_sample_body(pt_ref, lam_ref, q_ref, *rest, pps, past, post):
    pages = rest[:pps]
    new_ref, g_ref, o_ref, m_ref, l_ref, acc_ref = rest[pps:]
    del pt_ref
    c = pl.program_id(1)
    tq = q_ref.shape[0]
    n = pps * PAGE_SIZE

    @pl.when(c == 0)
    def _():
        _init_state(m_ref, l_ref, acc_ref)

    _diff_update(q_ref, lambda j: _page_piece(pages, j), None, _key_offsets(n, c * n - past), m_ref, l_ref, acc_ref)

    @pl.when(c == pl.num_programs(1) - 1)
    def _():
        npad = V7X_LANES
        keep = jnp.logical_and(_distance(tq, npad, 0) >= 0, lax.broadcasted_iota(jnp.int32, (tq, npad), 1) < tq)
        new = _pad_keys(new_ref[...], npad).astype(BF16)
        _diff_update(q_ref, lambda j: new[:, j * HEAD_DIM:(j + 1) * HEAD_DIM], _mask_bias(keep),
                     _key_offsets(npad, 0), m_ref, l_ref, acc_ref)
        _diff_finish(lam_ref, g_ref, o_ref, l_ref, acc_ref, post)


def _diff_sample(u, grp, lam, gain, post, cache, page_table, layer):
    t = grp.t
    assert grp.past % PAGE_SIZE == 0 and grp.row0 % t == 0
    n_pages = grp.past // PAGE_SIZE
    pps = _pages_per_step(n_pages)
    rb = grp.row0 // t
    return pl.pallas_call(
        functools.partial(_diff_sample_body, pps=pps, past=grp.past, post=post),
        grid_spec=pltpu.PrefetchScalarGridSpec(
            num_scalar_prefetch=1,
            grid=(grp.b, n_pages // pps),
            in_specs=[
                pl.BlockSpec(memory_space=pltpu.SMEM),
                pl.BlockSpec((t, DIFF_W), lambda b, c, pt: (rb + b, C_DQ // DIFF_W)),
            ] + _page_specs(layer, n_pages, pps) + [
                pl.BlockSpec((t, 2 * DIFF_W), lambda b, c, pt: (rb + b, C_DROWS // (2 * DIFF_W))),
                pl.BlockSpec((1, HEAD_DIM), lambda b, c, pt: (0, 0)),
            ],
            out_specs=pl.BlockSpec((t, DIFF_W), lambda b, c, pt: (b, 0)),
            scratch_shapes=_diff_state(t),
        ),
        out_shape=jax.ShapeDtypeStruct((grp.b * t, DIFF_W), F32),
        compiler_params=_cparams(("parallel", "arbitrary")),
        name="diff_attn_paged",
    )(page_table, lam, u, *([cache] * pps), u, gain)


def _mlstm_body(xq_ref, xk_ref, v_ref, og_ref, bq_ref, bk_ref, wq_ref, wk_ref, cq_ref, ck_ref, li_ref, lf_ref,
                c0_ref, n0_ref, m0_ref, g_ref, h_ref, c_ref, n_ref, m_ref, q_s, k_s, *, lc, nc):
    t = xq_ref.shape[0]
    row = lax.broadcasted_iota(jnp.int32, (lc, lc), 0)
    col = lax.broadcasted_iota(jnp.int32, (lc, lc), 1)
    tril = row >= col
    triu = row <= col
    eye = row == col
    gain = g_ref[...]

    def conv(x_ref, buf_ref, w_ref, b_ref):
        ext = jnp.concatenate([buf_ref[0], x_ref[...]], axis=0)
        y = b_ref[...]
        for i in range(CONV_WIDTH):
            shifted = ext if i == CONV_WIDTH - 1 else pltpu.roll(ext, CONV_WIDTH - 1 - i, 0)
            y = y + shifted[V7X_SUBLANES:V7X_SUBLANES + t] * w_ref[i:i + 1, :]
        return y * jax.nn.sigmoid(y)

    q_s[...] = conv(xq_ref, bq_ref, wq_ref, cq_ref) * MLSTM_DIM ** -0.5
    k_s[...] = conv(xk_ref, bk_ref, wk_ref, ck_ref)

    def to_col(r):
        return jnp.sum(jnp.where(eye, jnp.broadcast_to(r, (lc, lc)), 0.0), axis=1, keepdims=True)

    def step(ci, carry):
        c, n, m = carry
        rows = pl.ds(ci * lc if nc == 1 else pl.multiple_of(ci * lc, lc), lc)
        qc = q_s[rows, :]
        kc = k_s[rows, :]
        vc = v_ref[rows, :]
        li = li_ref[0, pl.ds(ci, 1), :]
        lf = lf_ref[0, pl.ds(ci, 1), :]
        lf_b = jnp.broadcast_to(lf, (lc, lc))
        bcum_c = jnp.sum(jnp.where(tril, lf_b, 0.0), axis=1, keepdims=True)
        bcum_r = jnp.sum(jnp.where(triu, jnp.broadcast_to(to_col(lf), (lc, lc)), 0.0), axis=0, keepdims=True)
        dmat = jnp.where(tril, bcum_c - bcum_r + li, NEG_INF)
        inter = bcum_c + m
        mj = jnp.maximum(inter, jnp.max(dmat, axis=1, keepdims=True))
        wts = jnp.exp(dmat - mj)
        a = jnp.exp(inter - mj)
        sqk = lax.dot_general(qc, kc, NT, precision=HIGHEST, preferred_element_type=F32) * wts
        num = a * jnp.dot(qc, c, precision=HIGHEST, preferred_element_type=F32) + \
            jnp.dot(sqk, vc, precision=HIGHEST, preferred_element_type=F32)
        den = a * jnp.sum(qc * n, axis=1, keepdims=True) + jnp.sum(sqk, axis=1, keepdims=True)
        h = num / jnp.maximum(jnp.abs(den), jnp.exp(-mj))
        h_ref[rows, :] = _rms_rows(h, gain) * jax.nn.sigmoid(og_ref[rows, :])
        b_last = jnp.sum(lf, axis=1, keepdims=True)
        gl = b_last - bcum_r + li
        m_new = jnp.maximum(b_last + m, jnp.max(gl, axis=1, keepdims=True))
        decay = jnp.exp(b_last + m - m_new)
        kw = kc * to_col(jnp.exp(gl - m_new))
        c_new = decay * c + lax.dot_general(kw, vc, (((0,), (0,)), ((), ())), precision=HIGHEST,
                                            preferred_element_type=F32)
        n_new = decay * n + jnp.sum(kw, axis=0, keepdims=True)
        return c_new, n_new, m_new

    init = (c0_ref[0], n0_ref[0], m0_ref[0][:, 0:1])
    c, n, m = step(0, init) if nc == 1 else lax.fori_loop(0, nc, step, init, unroll=2)
    c_ref[0] = c
    n_ref[0] = n
    m_ref[0] = jnp.broadcast_to(m, (1, MLSTM_DIM))


def _mlstm(u, grp, lp, c0, n0, m0, conv_buf):
    b, t = grp.b, grp.t
    assert grp.row0 % t == 0 and t % V7X_SUBLANES == 0
    bh = b * MLSTM_HEADS
    hd = MLSTM_DIM
    lc = MLSTM_CHUNK if t % MLSTM_CHUNK == 0 else t
    nc = t // lc
    rb = grp.row0 // t
    gates = u[grp.row0:grp.row0 + b * t, C_SMALL:C_SMALL + U_TILE].reshape(b, t, U_TILE)
    log_i = gates[:, :, G_MI:G_MI + MLSTM_HEADS] + lp['ml_gate_b'][0]
    log_f = jax.nn.log_sigmoid(gates[:, :, G_MF:G_MF + MLSTM_HEADS] + lp['ml_gate_b'][1])
    log_i = log_i.transpose(0, 2, 1).reshape(bh, nc, lc)
    log_f = log_f.transpose(0, 2, 1).reshape(bh, nc, lc)
    buf8 = jnp.pad(conv_buf, ((0, 0), (V7X_SUBLANES - (CONV_WIDTH - 1), 0), (0, 0)))
    w8 = jnp.pad(lp['ml_conv_w'], ((0, V7X_SUBLANES - CONV_WIDTH), (0, 0)))
    cb = lp['ml_conv_b'].reshape(1, 2 * MLSTM_W)
    tok = lambda col0: pl.BlockSpec((t, hd), lambda i, h: (rb + i, col0 // hd + h))
    buf = lambda off: pl.BlockSpec((1, V7X_SUBLANES, hd), lambda i, h: (i, 0, off + h))
    wsp = lambda off: pl.BlockSpec((V7X_SUBLANES, hd), lambda i, h: (0, off + h))
    bsp = lambda off: pl.BlockSpec((1, hd), lambda i, h: (0, off + h))
    gate = pl.BlockSpec((1, nc, lc), lambda i, h: (i * MLSTM_HEADS + h, 0, 0))
    vec = pl.BlockSpec((1, 1, hd), lambda i, h: (i * MLSTM_HEADS + h, 0, 0))
    mat = pl.BlockSpec((1, hd, hd), lambda i, h: (i * MLSTM_HEADS + h, 0, 0))
    h, c, n, m = pl.pallas_call(
        functools.partial(_mlstm_body, lc=lc, nc=nc),
        grid=(b, MLSTM_HEADS),
        in_specs=[tok(C_MQ), tok(C_MK), tok(C_MV), tok(C_MO), buf(0), buf(MLSTM_HEADS), wsp(0), wsp(MLSTM_HEADS),
                  bsp(0), bsp(MLSTM_HEADS), gate, gate, mat, vec, vec, pl.BlockSpec((1, hd), lambda i, h: (0, 0))],
        out_specs=[pl.BlockSpec((t, hd), lambda i, h: (i, h)), mat, vec, vec],
        out_shape=[
            jax.ShapeDtypeStruct((b * t, MLSTM_W), F32),
            jax.ShapeDtypeStruct((bh, hd, hd), F32),
            jax.ShapeDtypeStruct((bh, 1, hd), F32),
            jax.ShapeDtypeStruct((bh, 1, hd), F32),
        ],
        scratch_shapes=[pltpu.VMEM((t, hd), F32), pltpu.VMEM((t, hd), F32)],
        compiler_params=_cparams(("parallel", "parallel")),
        name="mlstm",
    )(u, u, u, u, buf8, buf8, w8, w8, cb, cb, log_i, log_f, c0.reshape(bh, hd, hd), n0.reshape(bh, 1, hd),
      jnp.broadcast_to(m0.reshape(bh, 1, 1), (bh, 1, hd)), lp['ml_out_norm'].reshape(1, hd))
    return (h, c.reshape(b, MLSTM_HEADS, hd, hd), n.reshape(b, MLSTM_HEADS, hd), m[:, 0, 0].reshape(b, MLSTM_HEADS))


def _cache_rows_body(nsa_ref, diff_ref, *rest):
    nsa_out, diff_out = rest[-2:]
    tm = nsa_ref.shape[0]
    for j in range(ROW_GROUPS):
        cols = slice(j * HEAD_DIM, (j + 1) * HEAD_DIM)
        nsa_out[pl.ds(j, tm, stride=ROW_GROUPS), :] = nsa_ref[:, cols]
        diff_out[pl.ds(j, tm, stride=ROW_GROUPS), :] = diff_ref[:, cols]


def _cache_rows(u, layer, depth, prev):
    t = u.shape[0]
    tm = _pick_tile(t, (256, 128, 64, 32, 16, 8))
    width = ROW_GROUPS * HEAD_DIM
    in_specs = [pl.BlockSpec((tm, width), lambda i: (i, C_ROWS // width)),
                pl.BlockSpec((tm, width), lambda i: (i, C_DROWS // width))]
    args = [u, u]
    aliases = {}
    if prev is not None:
        in_specs += [pl.BlockSpec(memory_space=pl.ANY)] * 2
        args += list(prev)
        aliases = {2: 0, 3: 1}
    out = pl.BlockSpec((None, tm * ROW_GROUPS, HEAD_DIM), lambda i: (layer, i, 0))
    shape = jax.ShapeDtypeStruct((depth, t * ROW_GROUPS, HEAD_DIM), F32)
    return pl.pallas_call(
        _cache_rows_body,
        grid=(t // tm,),
        in_specs=in_specs,
        out_specs=[out, out],
        out_shape=[shape, shape],
        input_output_aliases=aliases,
        compiler_params=_cparams(("parallel",)),
        name="cache_rows",
    )(*args)


def _pack_w_in(w_in):
    cuts = np.cumsum((0,) + IN_SPLITS)
    cols = [w_in[..., cuts[i]:cuts[i + 1]] for i in range(len(IN_SPLITS))]
    packed = jnp.concatenate([cols[i] for i in _MAIN_ORDER] + [cols[i] for i in _SMALL_ORDER], axis=-1)
    pad = [(0, 0)] * (packed.ndim - 1) + [(0, U_COLS - packed.shape[-1])]
    return jnp.pad(packed, pad).astype(BF16)


def _u_gain(lp):
    ones = lambda n: jnp.ones((n,), F32)
    g = jnp.concatenate([
        jnp.tile(lp['nsa_q_norm'], NSA_HEADS) * (HEAD_DIM ** -0.5 * LOG2E),
        ones(2 * NSA_KV_W), jnp.tile(lp['nsa_k_norm'][1], NSA_KV_HEADS), ones(NSA_KV_W),
        jnp.tile(lp['nsa_k_norm'][2], NSA_KV_HEADS), ones(NSA_KV_W),
        jnp.tile(lp['diff_q_norm'], 2 * DIFF_HEADS) * (DIFF_QK_DIM ** -0.5 * LOG2E),
        jnp.tile(lp['diff_k_norm'], 2 * DIFF_HEADS),
        ones(U_COLS - (C_DROWS + DIFF_W))])
    return g.reshape(1, U_COLS)


def _group_rows(u, grp, col0, width):
    return u[grp.row0:grp.row0 + grp.b * grp.t, col0:col0 + width].reshape(grp.b, grp.t, width)


def _tail_rows(buf, new, keep):
    n_new = min(keep, new.shape[1])
    parts = [buf[:, buf.shape[1] - (keep - n_new):], new[:, new.shape[1] - n_new:]] if keep > n_new else \
        [new[:, new.shape[1] - n_new:]]
    return jnp.concatenate(parts, axis=1)


def _trunk(x_prompt, x_sample, cache_nsa, cache_diff, state_win, state_ml_c, state_ml_n, state_ml_m, state_ml_conv,
           page_table, p):
    bp, tp, d = x_prompt.shape
    bs, ts, _ = x_sample.shape
    depth = p['w_in'].shape[0]
    past = page_table.shape[1] * PAGE_SIZE
    gp = _Group(bp, tp, 0, 0)
    gs = _Group(bs, ts, bp * tp, past)
    n_p = bp * tp
    w_in = _pack_w_in(p['w_in'])
    w_out = p['w_out'].astype(BF16)
    w13 = (p['ffn1_w13'].astype(BF16), p['ffn2_w13'].astype(BF16))
    w2 = (p['ffn1_w2'].astype(BF16), p['ffn2_w2'].astype(BF16))
    per_layer = ('mix_norm', 'nsa_q_norm', 'nsa_k_norm', 'nsa_cmp_pos', 'nsa_cmp_w1', 'nsa_cmp_w2', 'diff_q_norm',
                 'diff_k_norm', 'diff_lambda', 'diff_out_norm', 'ml_conv_w', 'ml_conv_b', 'ml_gate_b', 'ml_out_norm')
    cache_nsa = _page_view(cache_nsa)
    cache_diff = _page_view(cache_diff)
    win_state = state_win.reshape(state_win.shape[:3] + (2 * NSA_KV_W,))
    pt = page_table.reshape(-1)
    zeros = lambda *s: jnp.zeros(s, F32)
    p_state = (zeros(bp, MLSTM_HEADS, MLSTM_DIM, MLSTM_DIM), zeros(bp, MLSTM_HEADS, MLSTM_DIM),
               zeros(bp, MLSTM_HEADS), zeros(bp, CONV_WIDTH - 1, 2 * MLSTM_W))
    p_win0 = zeros(bp, 0, 2, NSA_KV_HEADS, HEAD_DIM)
    row_arrays = None

    x = jnp.concatenate([x_prompt.reshape(n_p, d), x_sample.reshape(bs * ts, d)], axis=0)
    outs = []
    for l in range(depth):
        lp = {name: p[name][l] for name in per_layer}
        lam_init = 0.8 - 0.6 * math.exp(-0.3 * l)
        lam_p = lp['diff_lambda']
        lam = (jnp.exp(jnp.sum(lam_p[0] * lam_p[1])) - jnp.exp(jnp.sum(lam_p[2] * lam_p[3])) + lam_init).reshape(1)
        dgain = lp['diff_out_norm'].reshape(1, HEAD_DIM)
        h = _ffn(x, p['ffn1_norm'][l], w13[0], w2[0], l)
        u = _inproj(h, lp['mix_norm'], w_in, _u_gain(lp), l)

        summ_p, n_cmp_p = _compress_prompt(u, gp, lp)
        oc_p, sel_p = _cmp_select(u, gp, summ_p, n_cmp_p)
        nsa_p = _nsa_prompt(u, gp, oc_p, sel_p)
        diff_p = _diff_prompt(u, gp, lam, dgain, 1.0 - lam_init)
        ml_p, c_p, nn_p, m_p = _mlstm(u, gp, lp, *p_state)

        summ_s, n_cmp_s, slc_kv = _compress_sample(u, gs, lp, cache_nsa, pt, l)
        oc_s, sel_s = _cmp_select(u, gs, summ_s, n_cmp_s)
        nsa_s = _nsa_sample(u, gs, oc_s, sel_s, slc_kv, win_state, l)
        diff_s = _diff_sample(u, gs, lam, dgain, 1.0 - lam_init, cache_diff, pt, l)
        ml_s, c_s, nn_s, m_s = _mlstm(u, gs, lp, state_ml_c[l], state_ml_n[l], state_ml_m[l], state_ml_conv[l])

        h = _outproj(h, jnp.concatenate([nsa_p, nsa_s]), jnp.concatenate([diff_p, diff_s]),
                     jnp.concatenate([ml_p, ml_s]), w_out, l)
        x = _ffn(h, p['ffn2_norm'][l], w13[1], w2[1], l)

        row_arrays = _cache_rows(u, l, depth, row_arrays)
        win_shape = (2, NSA_KV_HEADS, HEAD_DIM)
        keep_p, keep_s = min(WINDOW, tp), min(WINDOW, state_win.shape[2] + ts)
        win_new_p = _group_rows(u, gp, C_WIN, 2 * NSA_KV_W)[:, tp - min(keep_p, tp):]
        win_new_s = _group_rows(u, gs, C_WIN, 2 * NSA_KV_W)[:, ts - min(keep_s, ts):]
        win_rows_p = _tail_rows(p_win0, win_new_p.reshape(win_new_p.shape[:2] + win_shape), keep_p)
        win_rows_s = _tail_rows(state_win[l], win_new_s.reshape(win_new_s.shape[:2] + win_shape), keep_s)
        conv_p = _tail_rows(p_state[3], _group_rows(u, gp, C_MQ, 2 * MLSTM_W), CONV_WIDTH - 1)
        conv_s = _tail_rows(state_ml_conv[l], _group_rows(u, gs, C_MQ, 2 * MLSTM_W), CONV_WIDTH - 1)
        outs.append((win_rows_p, win_rows_s, c_p, c_s, nn_p, nn_s, m_p, m_s, conv_p, conv_s))
    stacked = [jnp.stack(z) for z in zip(*outs)]
    nsa_rows, diff_rows = row_arrays
    n_rows = n_p * ROW_GROUPS

    def split(a, shape):
        return (a[:, :n_rows].reshape((depth, bp, tp) + shape), a[:, n_rows:].reshape((depth, bs, ts) + shape))

    return ((x[:n_p].reshape(bp, tp, d), x[n_p:].reshape(bs, ts, d)) + split(nsa_rows, (4, NSA_KV_HEADS, HEAD_DIM))
            + split(diff_rows, (2, DIFF_HEADS, HEAD_DIM)) + tuple(stacked))


def kernel(x_prompt, x_sample, cache_nsa, cache_diff, state_win, state_ml_c, state_ml_n, state_ml_m, state_ml_conv,
           page_table, ffn1_norm, ffn1_w13, ffn1_w2, mix_norm, w_in, nsa_q_norm, nsa_k_norm, nsa_cmp_pos, nsa_cmp_w1,
           nsa_cmp_w2, diff_q_norm, diff_k_norm, diff_lambda, diff_out_norm, ml_conv_w, ml_conv_b, ml_gate_b,
           ml_out_norm, w_out, ffn2_norm, ffn2_w13, ffn2_w2):
    params = dict(ffn1_norm=ffn1_norm, ffn1_w13=ffn1_w13, ffn1_w2=ffn1_w2, mix_norm=mix_norm, w_in=w_in,
                  nsa_q_norm=nsa_q_norm, nsa_k_norm=nsa_k_norm, nsa_cmp_pos=nsa_cmp_pos, nsa_cmp_w1=nsa_cmp_w1,
                  nsa_cmp_w2=nsa_cmp_w2, diff_q_norm=diff_q_norm, diff_k_norm=diff_k_norm, diff_lambda=diff_lambda,
                  diff_out_norm=diff_out_norm, ml_conv_w=ml_conv_w, ml_conv_b=ml_conv_b, ml_gate_b=ml_gate_b,
                  ml_out_norm=ml_out_norm, w_out=w_out, ffn2_norm=ffn2_norm, ffn2_w13=ffn2_w13, ffn2_w2=ffn2_w2)
    return _trunk(x_prompt, x_sample, cache_nsa, cache_diff, state_win, state_ml_c, state_ml_n, state_ml_m,
                  state_ml_conv, page_table, params)
```

```python
import functools
import math
from typing import NamedTuple

import numpy as np
import jax
import jax.numpy as jnp
from jax import lax
from jax.experimental import pallas as pl
from jax.experimental.pallas import tpu as pltpu

F32 = jnp.float32
BF16 = jnp.bfloat16
HIGHEST = lax.Precision.HIGHEST
NT = (((1,), (1,)), ((), ()))

D_MODEL = 2048
D_FF = 5632
PAGE_SIZE = 128
HEAD_DIM = 128
ROW_GROUPS = 8
NSA_HEADS = 8
NSA_KV_HEADS = 2
NSA_GROUP = NSA_HEADS // NSA_KV_HEADS
CMP_BLOCK = 32
CMP_STRIDE = 16
SLC_BLOCK = 64
SLC_SHIFT = 6
SLC_TOPN = 16
WINDOW = 512
DIFF_HEADS = 4
DIFF_QK_DIM = HEAD_DIM // 2
MLSTM_HEADS = 4
MLSTM_DIM = 128
MLSTM_CHUNK = 64
CONV_WIDTH = 4
NORM_EPS = 1e-6
NEG_INF = -1e30
M_INIT = -1e29
LOG2E = 1.4426950408889634
FORCE_BONUS = 1e3

NSA_W = NSA_HEADS * HEAD_DIM
NSA_KV_W = NSA_KV_HEADS * HEAD_DIM
DIFF_W = DIFF_HEADS * HEAD_DIM
MLSTM_W = MLSTM_HEADS * MLSTM_DIM
IN_SPLITS = (NSA_W, 6 * NSA_KV_W, 3 * NSA_HEADS, DIFF_W, DIFF_W, DIFF_W,
             MLSTM_W, MLSTM_W, MLSTM_W, MLSTM_HEADS, MLSTM_HEADS, MLSTM_W)
IN_COLS = sum(IN_SPLITS)

V7X_LANES = 128
V7X_SUBLANES = 8
V7X_VMEM_LIMIT_BYTES = 56 * 1024 * 1024

U_TILE = 256
C_NQ = 0
C_ROWS = C_NQ + NSA_W
C_WIN = C_ROWS + 4 * NSA_KV_W
C_DQ = C_WIN + 2 * NSA_KV_W
C_DROWS = C_DQ + DIFF_W
C_MQ = C_DROWS + 2 * DIFF_W
C_MK = C_MQ + MLSTM_W
C_MV = C_MK + MLSTM_W
C_MO = C_MV + MLSTM_W
C_SMALL = C_MO + MLSTM_W
U_COLS = C_SMALL + U_TILE
U_STEP = 5 * U_TILE
G_NSA, G_MI, G_MF = 0, 3 * NSA_HEADS, 3 * NSA_HEADS + MLSTM_HEADS
U_KINDS = (1, 1, 1, 1, 0, 0, 1, 0, 1, 0, 2, 2, 2, 2, 0, 0, 0, 0, 0, 0, 0, 0, 0, 0, 0)
_MAIN_ORDER = (0, 1, 3, 4, 5, 6, 7, 8, 11)
_SMALL_ORDER = (2, 9, 10)


class _Group(NamedTuple):
    b: int
    t: int
    row0: int
    past: int


def _round_up(n, m):
    return -(-n // m) * m


def _cparams(semantics):
    return pltpu.CompilerParams(dimension_semantics=semantics, vmem_limit_bytes=V7X_VMEM_LIMIT_BYTES)


def _pick_tile(n, prefs):
    for p in prefs:
        if n % p == 0:
            return p
    return n


def _alibi(n):
    return [float(2.0 ** (-8.0 * i / n)) for i in range(1, n + 1)]


def _rms_rows(x, g):
    inv = lax.rsqrt(jnp.mean(x * x, axis=-1, keepdims=True) + NORM_EPS)
    return x * inv * g


def _ffn_body(x_ref, g_ref, wa_ref, wb_ref, w2_ref, o_ref, n_ref):
    @pl.when(pl.program_id(1) == 0)
    def _():
        x = x_ref[...]
        n_ref[...] = _rms_rows(x, g_ref[...]).astype(BF16)
        o_ref[...] = x

    n = n_ref[...]
    a = jnp.dot(n, wa_ref[...], preferred_element_type=F32)
    b = jnp.dot(n, wb_ref[...], preferred_element_type=F32)
    act = (0.5 * a * jax.nn.sigmoid(a) * b).astype(BF16)
    o_ref[...] += jnp.dot(act, w2_ref[...], preferred_element_type=F32)


def _ffn(x, g, w13, w2, layer):
    t, d = x.shape
    f = w2.shape[1]
    tm = _pick_tile(t, (768, 512, 256, 128, 64, 32, 16, 8))
    tf = _pick_tile(f, (512, 256, 128))
    nf = f // tf
    return pl.pallas_call(
        _ffn_body,
        grid=(t // tm, nf),
        in_specs=[
            pl.BlockSpec((tm, d), lambda i, j: (i, 0)),
            pl.BlockSpec((1, d), lambda i, j: (0, 0)),
            pl.BlockSpec((None, d, tf), lambda i, j: (layer, 0, j)),
            pl.BlockSpec((None, d, tf), lambda i, j: (layer, 0, j + nf)),
            pl.BlockSpec((None, tf, d), lambda i, j: (layer, j, 0)),
        ],
        out_specs=pl.BlockSpec((tm, d), lambda i, j: (i, 0)),
        out_shape=jax.ShapeDtypeStruct((t, d), F32),
        scratch_shapes=[pltpu.VMEM((tm, d), BF16)],
        compiler_params=_cparams(("parallel", "arbitrary")),
        name="ffn",
    )(x, g.reshape(1, d), w13, w13, w2)


def _inproj_body(kind_ref, x_ref, g_ref, w_ref, gain_ref, o_ref, n_ref):
    j = pl.program_id(1)

    @pl.when(j == 0)
    def _():
        n_ref[...] = _rms_rows(x_ref[...], g_ref[...]).astype(BF16)

    y = jnp.dot(n_ref[...], w_ref[...], preferred_element_type=F32)
    gain = gain_ref[...]
    for i in range(U_STEP // U_TILE):
        kind = kind_ref[j * (U_STEP // U_TILE) + i]
        halves = [slice(i * U_TILE + h * HEAD_DIM, i * U_TILE + (h + 1) * HEAD_DIM)
                  for h in range(U_TILE // HEAD_DIM)]

        @pl.when(kind == 0)
        def _(halves=halves):
            for cols in halves:
                o_ref[:, cols] = y[:, cols]

        @pl.when(kind == 1)
        def _(halves=halves):
            for cols in halves:
                o_ref[:, cols] = _rms_rows(y[:, cols], gain[:, cols])

        @pl.when(kind == 2)
        def _(halves=halves):
            low = lax.broadcasted_iota(jnp.int32, (y.shape[0], HEAD_DIM), 1) < DIFF_QK_DIM
            for cols in halves:
                yh = y[:, cols]
                sq = yh * yh
                s_lo = jnp.sum(jnp.where(low, sq, 0.0), axis=-1, keepdims=True)
                s_hi = jnp.sum(jnp.where(low, 0.0, sq), axis=-1, keepdims=True)
                inv = lax.rsqrt(jnp.where(low, s_lo, s_hi) * (1.0 / DIFF_QK_DIM) + NORM_EPS)
                o_ref[:, cols] = yh * inv * gain[:, cols]


def _inproj(x, g, w, gain, layer):
    t, d = x.shape
    tm = _pick_tile(t, (768, 512, 256, 128, 64, 32, 16, 8))
    return pl.pallas_call(
        _inproj_body,
        grid=(t // tm, U_COLS // U_STEP),
        in_specs=[
            pl.BlockSpec(memory_space=pltpu.SMEM),
            pl.BlockSpec((tm, d), lambda i, j: (i, 0)),
            pl.BlockSpec((1, d), lambda i, j: (0, 0)),
            pl.BlockSpec((None, d, U_STEP), lambda i, j: (layer, 0, j)),
            pl.BlockSpec((1, U_STEP), lambda i, j: (0, j)),
        ],
        out_specs=pl.BlockSpec((tm, U_STEP), lambda i, j: (i, j)),
        out_shape=jax.ShapeDtypeStruct((t, U_COLS), F32),
        scratch_shapes=[pltpu.VMEM((tm, d), BF16)],
        compiler_params=_cparams(("parallel", "arbitrary")),
        name="inproj",
    )(jnp.asarray(U_KINDS, jnp.int32), x, g.reshape(1, d), w, gain)


def _outproj_body(h_ref, a_ref, b_ref, c_ref, w_ref, y_ref):
    ka, kb = a_ref.shape[1], b_ref.shape[1]
    y = h_ref[...] + jnp.dot(a_ref[...].astype(BF16), w_ref[0:ka, :], preferred_element_type=F32)
    y = y + jnp.dot(b_ref[...].astype(BF16), w_ref[ka:ka + kb, :], preferred_element_type=F32)
    y_ref[...] = y + jnp.dot(c_ref[...].astype(BF16), w_ref[ka + kb:, :], preferred_element_type=F32)


def _outproj(h, o_nsa, o_diff, o_ml, w, layer):
    t, d = h.shape
    tm = _pick_tile(t, (768, 512, 256, 128, 64, 32, 16, 8))
    row = lambda a: pl.BlockSpec((tm, a.shape[1]), lambda i: (i, 0))
    return pl.pallas_call(
        _outproj_body,
        grid=(t // tm,),
        in_specs=[row(h), row(o_nsa), row(o_diff), row(o_ml),
                  pl.BlockSpec((None,) + w.shape[1:], lambda i: (layer, 0, 0))],
        out_specs=row(h),
        out_shape=jax.ShapeDtypeStruct((t, d), F32),
        compiler_params=_cparams(("parallel",)),
        name="outproj",
    )(h, o_nsa, o_diff, o_ml, w)


def _online_update(s, bias, krel, slopes, v, m_ref, l_ref, acc_ref, base, tq):
    def shift_of(slope):
        shift = (slope * LOG2E) * krel
        return jnp.broadcast_to(shift, (tq, shift.shape[1])) if bias is None else bias + shift

    rows = slice(base, base + len(slopes) * tq)
    p, alpha = _online_softmax(s, jnp.concatenate([shift_of(sl) for sl in slopes], axis=0), m_ref, l_ref, rows)
    acc_ref[rows] = alpha * acc_ref[rows] + jnp.dot(p, v, preferred_element_type=F32)


def _online_softmax(s, shift, m_ref, l_ref, rows):
    sg = s + shift
    m_old = m_ref[rows]
    m_new = jnp.maximum(m_old, jnp.max(sg, axis=-1, keepdims=True))
    p = jnp.exp2(sg - jnp.concatenate([m_new] * (sg.shape[1] // V7X_LANES), axis=1))
    alpha = jnp.exp2(m_old - m_new)
    l_ref[rows] = alpha * l_ref[rows] + jnp.sum(p, axis=-1, keepdims=True)
    m_ref[rows] = m_new
    return p.astype(BF16), alpha


def _init_state(m_ref, l_ref, acc_ref):
    m_ref[...] = jnp.full(m_ref.shape, M_INIT, F32)
    l_ref[...] = jnp.zeros(l_ref.shape, F32)
    acc_ref[...] = jnp.zeros(acc_ref.shape, F32)


def _mask_bias(keep):
    return jnp.where(keep, 0.0, NEG_INF)


def _key_offsets(n, first):
    return (first + lax.broadcasted_iota(jnp.int32, (1, n), 1)).astype(F32)


def _normalised(l_ref, acc_ref, rows):
    return acc_ref[rows] / jnp.maximum(l_ref[rows], 1e-30)


def _distance(tq, n, offset):
    return offset + lax.broadcasted_iota(jnp.int32, (tq, n), 0) - lax.broadcasted_iota(jnp.int32, (tq, n), 1)


def _block_expand(n_spad, n, first_key):
    blk = lax.broadcasted_iota(jnp.int32, (n_spad, n), 0)
    key = first_key + lax.broadcasted_iota(jnp.int32, (n_spad, n), 1)
    return jnp.where(lax.shift_right_logical(key, SLC_SHIFT) == blk, 1.0, 0.0).astype(BF16)


def _stack_heads(q_ref, k):
    return jnp.concatenate([q_ref[:, (k * NSA_GROUP + g) * HEAD_DIM:(k * NSA_GROUP + g + 1) * HEAD_DIM]
                            for g in range(NSA_GROUP)], axis=0).astype(BF16)


def _pad_keys(x, n):
    return jnp.concatenate([x, jnp.zeros((n - x.shape[0], x.shape[1]), x.dtype)], axis=0)


def _nsa_combine(gt_ref, oc_ref, o_ref, l_ref, acc_ref, tq):
    gt = jax.nn.sigmoid(gt_ref[:, 0:V7X_LANES])
    for h in range(NSA_HEADS):
        cols = slice(h * HEAD_DIM, (h + 1) * HEAD_DIM)
        o_slc = _normalised(l_ref, acc_ref, slice(h * tq, (h + 1) * tq))
        o_win = _normalised(l_ref, acc_ref, slice((NSA_HEADS + h) * tq, (NSA_HEADS + h + 1) * tq))
        c0 = G_NSA + 3 * h
        o_ref[:, cols] = gt[:, c0:c0 + 1] * oc_ref[:, cols] + gt[:, c0 + 1:c0 + 2] * o_slc + gt[:, c0 + 2:c0 + 3] * o_win


_NSA_SLOPES = _alibi(NSA_HEADS)
_DIFF_SLOPES = _alibi(DIFF_HEADS)


def _softmax_state(rows):
    return [pltpu.VMEM((rows, V7X_LANES), F32), pltpu.VMEM((rows, V7X_LANES), F32), pltpu.VMEM((rows, HEAD_DIM), F32)]


def _nsa_state(tq):
    return _softmax_state(2 * NSA_HEADS * tq)


def _diff_state(tq):
    return _softmax_state(2 * DIFF_HEADS * tq)


def _cmp_accumulate(rows_of, w_ref, o_ref):
    for c in range(4):
        for r in range(CMP_STRIDE):
            part = jnp.dot(rows_of(c, r).astype(BF16), w_ref[c // 2, r * HEAD_DIM:(r + 1) * HEAD_DIM, :],
                           preferred_element_type=F32)
            if r == 0:
                o_ref[0, c] = part
            else:
                o_ref[0, c] += part


def _cmp_partial_body(x0_ref, x1_ref, x2_ref, x3_ref, w_ref, o_ref):
    srcs = (x0_ref, x1_ref, x2_ref, x3_ref)
    per = x0_ref.shape[0] // CMP_STRIDE
    _cmp_accumulate(lambda c, r: srcs[c][pl.ds(r, per, stride=CMP_STRIDE), :], w_ref, o_ref)


def _cmp_partial_paged_body(pt_ref, *refs, pps):
    del pt_ref
    pages, w_ref, o_ref, kv_ref, slab_ref = refs[:pps], refs[pps], refs[pps + 1], refs[pps + 2], refs[pps + 3]
    for i, pg in enumerate(pages):
        for c in range(4):
            slab_ref[c, i * PAGE_SIZE:(i + 1) * PAGE_SIZE, :] = _page_rows(pg, c)
    per = pps * PAGE_SIZE // CMP_STRIDE
    _cmp_accumulate(lambda c, r: slab_ref[c, pl.ds(r, per, stride=CMP_STRIDE), :], w_ref, o_ref)
    for j in range(4):
        kv_ref[0, j] = _page_piece(pages, 4 + j)


def _cmp_finish_body(*refs, has_tail, n_out):
    if has_tail:
        p_ref, tail_ref, w1_ref, pos_ref, w2_ref, g_ref, o_ref = refs
    else:
        p_ref, w1_ref, pos_ref, w2_ref, g_ref, o_ref = refs
    c = pl.program_id(0)
    p = p_ref[0, 0]
    n = p.shape[0]
    w1 = w1_ref[0]
    pb = jnp.dot(pos_ref[0].astype(BF16), w1, preferred_element_type=F32)
    nxt = pltpu.roll(p[:, HEAD_DIM:], n - 1, 0)
    if has_tail:
        rid = lax.broadcasted_iota(jnp.int32, (V7X_SUBLANES, HEAD_DIM), 0)
        tail8 = jnp.zeros((V7X_SUBLANES, HEAD_DIM), F32)
        xt = tail_ref[...].astype(BF16)
        for r in range(tail_ref.shape[0]):
            d = jnp.dot(xt, w1[r * HEAD_DIM:(r + 1) * HEAD_DIM, HEAD_DIM:], preferred_element_type=F32)
            tail8 = tail8 + jnp.where(rid == r, d, 0.0)
        tail = jnp.sum(tail8, axis=0, keepdims=True)
    else:
        tail = jnp.zeros((1, HEAD_DIM), F32)
    last = lax.broadcasted_iota(jnp.int32, (n, HEAD_DIM), 0) == n - 1
    z = p[:, :HEAD_DIM] + pb[0:1, :HEAD_DIM] + jnp.where(last, tail, nxt) + pb[1:2, HEAD_DIM:]
    y = jnp.dot((z * jax.nn.sigmoid(z)).astype(BF16), w2_ref[0].astype(BF16), preferred_element_type=F32)
    y = jnp.where(c < NSA_KV_HEADS, _rms_rows(y, g_ref[...]), y)
    o_ref[0, 0, 0:n] = y
    if n_out > n:
        o_ref[0, 0, n:n_out] = jnp.zeros((n_out - n, HEAD_DIM), F32)


def _cmp_weights(lp):
    half = CMP_STRIDE * HEAD_DIM
    w1 = lp['nsa_cmp_w1']
    w1ab = jnp.concatenate([w1[:, :half], w1[:, half:]], axis=2).astype(BF16)
    pos = lp['nsa_cmp_pos']
    pos2 = jnp.concatenate([pos[:, :CMP_STRIDE].reshape(2, 1, half), pos[:, CMP_STRIDE:].reshape(2, 1, half),
                            jnp.zeros((2, V7X_SUBLANES - 2, half), F32)], axis=1)
    return w1ab, pos2


def _cmp_finish(p, tail_src, w1ab, pos2, lp, n_out, grp, u):
    b, _, n, _ = p.shape
    half = CMP_STRIDE * HEAD_DIM
    has_tail = tail_src is not None
    in_specs = [pl.BlockSpec((1, 1, n, 2 * HEAD_DIM), lambda c, i: (i, c, 0, 0))]
    args = [p]
    if has_tail:
        rb = grp.row0 // grp.t
        in_specs.append(pl.BlockSpec((grp.t, HEAD_DIM), lambda c, i: (rb + i, C_ROWS // HEAD_DIM + c)))
        args.append(u)
    in_specs += [
        pl.BlockSpec((1, half, 2 * HEAD_DIM), lambda c, i: (c // NSA_KV_HEADS, 0, 0)),
        pl.BlockSpec((1, V7X_SUBLANES, half), lambda c, i: (c // NSA_KV_HEADS, 0, 0)),
        pl.BlockSpec((1, HEAD_DIM, HEAD_DIM), lambda c, i: (c // NSA_KV_HEADS, 0, 0)),
        pl.BlockSpec((1, HEAD_DIM), lambda c, i: (0, 0)),
    ]
    args += [w1ab, pos2, lp['nsa_cmp_w2'], lp['nsa_k_norm'][0:1]]
    return pl.pallas_call(
        functools.partial(_cmp_finish_body, has_tail=has_tail, n_out=n_out),
        grid=(4, b),
        in_specs=in_specs,
        out_specs=pl.BlockSpec((1, 1, n_out, HEAD_DIM), lambda c, i: (i, c, 0, 0)),
        out_shape=jax.ShapeDtypeStruct((b, 4, n_out, HEAD_DIM), F32),
        compiler_params=_cparams(("parallel", "parallel")),
        name="nsa_cmp_finish",
    )(*args)


def _compress_prompt(u, grp, lp):
    assert grp.past == 0 and grp.row0 == 0 and grp.t % (CMP_STRIDE * V7X_SUBLANES) == 0
    w1ab, pos2 = _cmp_weights(lp)
    n = grp.t // CMP_STRIDE
    group = lambda c: pl.BlockSpec((grp.t, HEAD_DIM), lambda i: (i, C_ROWS // HEAD_DIM + c))
    p = pl.pallas_call(
        _cmp_partial_body,
        grid=(grp.b,),
        in_specs=[group(0), group(1), group(2), group(3), pl.BlockSpec(w1ab.shape, lambda i: (0, 0, 0))],
        out_specs=pl.BlockSpec((1, 4, n, 2 * HEAD_DIM), lambda i: (i, 0, 0, 0)),
        out_shape=jax.ShapeDtypeStruct((grp.b, 4, n, 2 * HEAD_DIM), F32),
        compiler_params=_cparams(("parallel",)),
        name="nsa_cmp_partial",
    )(u, u, u, u, w1ab)
    return _cmp_finish(p, None, w1ab, pos2, lp, _round_up(n, V7X_LANES), grp, u), n - 1


def _pages_per_step(n_pages):
    return _pick_tile(n_pages, (16, 8, 4, 2, 1))


def _page_view(cache):
    assert cache.shape[3] * cache.shape[4] == ROW_GROUPS and cache.shape[5] == HEAD_DIM
    return cache.reshape(cache.shape[0], cache.shape[1], PAGE_SIZE * ROW_GROUPS, HEAD_DIM)


def _page_specs(layer, n_pages, pps):
    def spec(i):
        return pl.BlockSpec((1, 1, PAGE_SIZE * ROW_GROUPS, HEAD_DIM),
                            lambda b, c, pt: (layer, pt[b * n_pages + c * pps + i], 0, 0))
    return [spec(i) for i in range(pps)]


def _page_rows(page, j):
    return page[0, 0, pl.ds(j, PAGE_SIZE, stride=ROW_GROUPS), :]


def _page_piece(pages, j):
    return jnp.concatenate([_page_rows(pg, j).astype(BF16) for pg in pages], axis=0)


def _compress_sample(u, grp, lp, cache, page_table, layer):
    assert grp.past % PAGE_SIZE == 0 and grp.t < CMP_STRIDE
    w1ab, pos2 = _cmp_weights(lp)
    n_pages = grp.past // PAGE_SIZE
    pps = _pages_per_step(n_pages)
    per = pps * PAGE_SIZE // CMP_STRIDE
    n = grp.past // CMP_STRIDE
    p, slc_kv = pl.pallas_call(
        functools.partial(_cmp_partial_paged_body, pps=pps),
        grid_spec=pltpu.PrefetchScalarGridSpec(
            num_scalar_prefetch=1,
            grid=(grp.b, n_pages // pps),
            in_specs=_page_specs(layer, n_pages, pps) + [
                pl.BlockSpec(w1ab.shape, lambda b, c, pt: (0, 0, 0))],
            out_specs=[
                pl.BlockSpec((1, 4, per, 2 * HEAD_DIM), lambda b, c, pt: (b, 0, c, 0)),
                pl.BlockSpec((1, 4, pps * PAGE_SIZE, HEAD_DIM), lambda b, c, pt: (b, 0, c, 0)),
            ],
            scratch_shapes=[pltpu.VMEM((4, pps * PAGE_SIZE, HEAD_DIM), F32)],
        ),
        out_shape=[
            jax.ShapeDtypeStruct((grp.b, 4, n, 2 * HEAD_DIM), F32),
            jax.ShapeDtypeStruct((grp.b, 4, grp.past, HEAD_DIM), BF16),
        ],
        compiler_params=_cparams(("parallel", "arbitrary")),
        name="nsa_cmp_partial_paged",
    )(page_table, *([cache] * pps), w1ab)
    return _cmp_finish(p, u, w1ab, pos2, lp, _round_up(n + 1, V7X_LANES), grp, u), n, slc_kv


def _cmp_select_body(q_ref, kv_ref, o_ref, sel_ref, *, tq, q_off, n_cmp, n_slc, n_sel):
    qi = pl.program_id(1)
    n_cpad = kv_ref.shape[2]
    n_spad = sel_ref.shape[1] // NSA_KV_HEADS
    q_lo = q_off + qi * tq
    qpos = q_lo + lax.broadcasted_iota(jnp.int32, (tq, n_cpad), 0)
    ends = lax.broadcasted_iota(jnp.int32, (tq, n_cpad), 1) * CMP_STRIDE + (CMP_BLOCK - 1)
    d = qpos - ends
    mask = d >= 0
    df = d.astype(F32)
    c_lo = lax.broadcasted_iota(jnp.int32, (n_cpad, n_spad), 0) * CMP_STRIDE
    j_lo = lax.broadcasted_iota(jnp.int32, (n_cpad, n_spad), 1) * SLC_BLOCK
    cover = jnp.where(c_lo < j_lo + SLC_BLOCK, jnp.where(c_lo + CMP_BLOCK > j_lo, 1.0, 0.0), 0.0)
    cover = jnp.where(c_lo < n_cmp * CMP_STRIDE, cover, 0.0)
    qp = q_lo + lax.broadcasted_iota(jnp.int32, (tq, n_spad), 0)
    j = lax.broadcasted_iota(jnp.int32, (tq, n_spad), 1)
    cur = lax.shift_right_logical(qp, SLC_SHIFT)
    bonus = FORCE_BONUS * jnp.where(j == 0, 1.0, jnp.where(j == cur, 1.0, jnp.where(j == cur - 1, 1.0, 0.0)))
    valid = j * SLC_BLOCK <= qp
    for k in range(NSA_KV_HEADS):
        q = jnp.concatenate([q_ref[:, (k * NSA_GROUP + g) * HEAD_DIM:(k * NSA_GROUP + g + 1) * HEAD_DIM]
                             for g in range(NSA_GROUP)], axis=0)
        s = lax.dot_general(q, kv_ref[0, k], NT, precision=HIGHEST, preferred_element_type=F32)
        vc = kv_ref[0, NSA_KV_HEADS + k].astype(BF16)
        psum = jnp.zeros((tq, n_cpad), F32)
        for g in range(NSA_GROUP):
            h = k * NSA_GROUP + g
            sg = jnp.where(mask, s[g * tq:(g + 1) * tq] - (_NSA_SLOPES[h] * LOG2E) * df, NEG_INF)
            m = jnp.max(sg, axis=-1, keepdims=True)
            p = jnp.where(mask, jnp.exp2(sg - m), 0.0)
            p = p / jnp.maximum(jnp.sum(p, axis=-1, keepdims=True), 1e-30)
            o_ref[:, h * HEAD_DIM:(h + 1) * HEAD_DIM] = jnp.dot(p.astype(BF16), vc, preferred_element_type=F32)
            psum = psum + p
        imp = jnp.dot(psum, cover, precision=HIGHEST, preferred_element_type=F32)
        score = jnp.where(valid, imp + bonus, NEG_INF)
        cnt = jnp.zeros((tq, n_spad), F32)
        for i in range(n_slc):
            ci = score[:, i:i + 1]
            cnt = cnt + jnp.where(ci > score, 1.0, jnp.where(ci == score, jnp.where(j > i, 1.0, 0.0), 0.0))
        sel_ref[:, k * n_spad:(k + 1) * n_spad] = jnp.where(cnt < n_sel, jnp.where(j < n_slc, 0.0, NEG_INF), NEG_INF)


def _cmp_select(u, grp, summaries, n_cmp):
    n_slc = -(-(grp.past + grp.t) // SLC_BLOCK)
    n_spad = _round_up(n_slc, V7X_LANES)
    n_cpad = summaries.shape[2]
    tq = _pick_tile(grp.t, (256, 128, 64, 32, 16, 8))
    nq = grp.t // tq
    rb = grp.row0 // tq
    body = functools.partial(_cmp_select_body, tq=tq, q_off=grp.past, n_cmp=n_cmp, n_slc=n_slc,
                             n_sel=min(SLC_TOPN, n_slc))
    return pl.pallas_call(
        body,
        grid=(grp.b, nq),
        in_specs=[
            pl.BlockSpec((tq, NSA_W), lambda b, i: (rb + b * nq + i, 0)),
            pl.BlockSpec((1, 4, n_cpad, HEAD_DIM), lambda b, i: (b, 0, 0, 0)),
        ],
        out_specs=[
            pl.BlockSpec((tq, NSA_W), lambda b, i: (b * nq + i, 0)),
            pl.BlockSpec((tq, NSA_KV_HEADS * n_spad), lambda b, i: (b * nq + i, 0)),
        ],
        out_shape=[
            jax.ShapeDtypeStruct((grp.b * grp.t, NSA_W), F32),
            jax.ShapeDtypeStruct((grp.b * grp.t, NSA_KV_HEADS * n_spad), F32),
        ],
        compiler_params=_cparams(("parallel", "parallel")),
        name="nsa_cmp_select",
    )(u, summaries)


def _nsa_prompt_body(q_ref, ks_ref, kw_ref, sel_ref, oc_ref, gt_ref, o_ref, m_ref, l_ref, acc_ref, *, tq, tk):
    qi = pl.program_id(1)
    kj = pl.program_id(2)
    n_spad = sel_ref.shape[1] // NSA_KV_HEADS

    @pl.when(kj == 0)
    def _():
        _init_state(m_ref, l_ref, acc_ref)

    q_lo = qi * tq
    k_lo = kj * tk
    slc_on = k_lo <= q_lo + (tq - 1)
    win_on = jnp.logical_and(slc_on, k_lo + (tk - 1) >= q_lo - WINDOW)

    krel = _key_offsets(tk, k_lo - q_lo)

    @pl.when(slc_on)
    def _():
        causal = _mask_bias(_distance(tq, tk, q_lo - k_lo) >= 0)
        expand = _block_expand(n_spad, tk, k_lo)
        kv = ks_ref[...].astype(BF16)
        for k in range(NSA_KV_HEADS):
            s = lax.dot_general(_stack_heads(q_ref, k), kv[:, k * HEAD_DIM:(k + 1) * HEAD_DIM], NT,
                                preferred_element_type=F32)
            chosen = jnp.dot(sel_ref[:, k * n_spad:(k + 1) * n_spad].astype(BF16), expand,
                             preferred_element_type=F32)
            _online_update(s, causal + chosen, krel, _NSA_SLOPES[k * NSA_GROUP:(k + 1) * NSA_GROUP],
                           kv[:, (NSA_KV_HEADS + k) * HEAD_DIM:(NSA_KV_HEADS + k + 1) * HEAD_DIM],
                           m_ref, l_ref, acc_ref, k * NSA_GROUP * tq, tq)

    @pl.when(win_on)
    def _():
        d = _distance(tq, tk, q_lo - k_lo)
        bias = _mask_bias(jnp.logical_and(d >= 0, d <= WINDOW))
        kv = kw_ref[...].astype(BF16)
        for k in range(NSA_KV_HEADS):
            s = lax.dot_general(_stack_heads(q_ref, k), kv[:, k * HEAD_DIM:(k + 1) * HEAD_DIM], NT,
                                preferred_element_type=F32)
            _online_update(s, bias, krel, _NSA_SLOPES[k * NSA_GROUP:(k + 1) * NSA_GROUP],
                           kv[:, (NSA_KV_HEADS + k) * HEAD_DIM:(NSA_KV_HEADS + k + 1) * HEAD_DIM],
                           m_ref, l_ref, acc_ref, (NSA_KV_HEADS + k) * NSA_GROUP * tq, tq)

    @pl.when(kj == pl.num_programs(2) - 1)
    def _():
        _nsa_combine(gt_ref, oc_ref, o_ref, l_ref, acc_ref, tq)


def _nsa_prompt(u, grp, o_cmp, sel):
    assert grp.past == 0 and grp.row0 == 0
    t = grp.t
    tq = _pick_tile(t, (256, 128, 64, 32, 16, 8))
    tk = _pick_tile(t, (512, 256, 128))
    nq, nk = t // tq, t // tk
    kvw = 2 * NSA_KV_W

    def last_tile(i):
        return (i * tq + (tq - 1)) // tk

    def slc_rows(b, i, j):
        return (b * nk + jnp.minimum(j, last_tile(i)), (C_ROWS + kvw) // kvw)

    def win_rows(b, i, j):
        first = jnp.maximum(i * tq - WINDOW, 0) // tk
        return (b * nk + jnp.clip(j, first, last_tile(i)), C_WIN // kvw)

    return pl.pallas_call(
        functools.partial(_nsa_prompt_body, tq=tq, tk=tk),
        grid=(grp.b, nq, nk),
        in_specs=[
            pl.BlockSpec((tq, NSA_W), lambda b, i, j: (b * nq + i, 0)),
            pl.BlockSpec((tk, kvw), slc_rows),
            pl.BlockSpec((tk, kvw), win_rows),
            pl.BlockSpec((tq, sel.shape[1]), lambda b, i, j: (b * nq + i, 0)),
            pl.BlockSpec((tq, NSA_W), lambda b, i, j: (b * nq + i, 0)),
            pl.BlockSpec((tq, U_TILE), lambda b, i, j: (b * nq + i, C_SMALL // U_TILE)),
        ],
        out_specs=pl.BlockSpec((tq, NSA_W), lambda b, i, j: (b * nq + i, 0)),
        out_shape=jax.ShapeDtypeStruct((grp.b * t, NSA_W), F32),
        scratch_shapes=_nsa_state(tq),
        compiler_params=_cparams(("parallel", "parallel", "arbitrary")),
        name="nsa_attn_prompt",
    )(u, u, u, sel, o_cmp, u)


def _nsa_sample_body(q_ref, kv_ref, knew_ref, wst_ref, wnew_ref, sel_ref, oc_ref, gt_ref, o_ref, m_ref, l_ref, acc_ref,
                     *, past, n_win):
    c = pl.program_id(1)
    tq = q_ref.shape[0]
    n_spad = sel_ref.shape[1] // NSA_KV_HEADS
    n = kv_ref.shape[2]

    @pl.when(c == 0)
    def _():
        _init_state(m_ref, l_ref, acc_ref)

    def attend(keys_of, vals_of, bias_of, krel, branch):
        for k in range(NSA_KV_HEADS):
            s = lax.dot_general(_stack_heads(q_ref, k), keys_of(k), NT, preferred_element_type=F32)
            _online_update(s, bias_of(k), krel, _NSA_SLOPES[k * NSA_GROUP:(k + 1) * NSA_GROUP], vals_of(k),
                           m_ref, l_ref, acc_ref, (branch * NSA_KV_HEADS + k) * NSA_GROUP * tq, tq)

    def slab(ref, j):
        return ref[:, j * HEAD_DIM:(j + 1) * HEAD_DIM]

    first_key = c * n
    expand = _block_expand(n_spad, n, first_key)
    attend(lambda k: kv_ref[0, k], lambda k: kv_ref[0, NSA_KV_HEADS + k],
           lambda k: jnp.dot(sel_ref[:, k * n_spad:(k + 1) * n_spad].astype(BF16), expand,
                             preferred_element_type=F32),
           _key_offsets(n, first_key - past), 0)

    @pl.when(c == pl.num_programs(1) - 1)
    def _():
        npad = V7X_LANES
        real = lax.broadcasted_iota(jnp.int32, (tq, npad), 1) < tq
        causal = _mask_bias(jnp.logical_and(_distance(tq, npad, 0) >= 0, real))
        lane = lax.broadcasted_iota(jnp.int32, (tq, n_spad), 1)

        def chosen_new(k):
            col = jnp.where(lane == past // SLC_BLOCK, sel_ref[:, k * n_spad:(k + 1) * n_spad], 0.0)
            return causal + jnp.sum(col, axis=-1, keepdims=True)

        knew = _pad_keys(knew_ref[...], npad).astype(BF16)
        attend(lambda k: slab(knew, k), lambda k: slab(knew, NSA_KV_HEADS + k), chosen_new, _key_offsets(npad, 0), 0)
        wbuf = wst_ref[0, 0].astype(BF16)
        in_window = _mask_bias(_distance(tq, n_win, n_win) <= WINDOW)
        attend(lambda k: slab(wbuf, k), lambda k: slab(wbuf, NSA_KV_HEADS + k), lambda k: in_window,
               _key_offsets(n_win, -n_win), 1)
        wnew = _pad_keys(wnew_ref[...], npad).astype(BF16)
        attend(lambda k: slab(wnew, k), lambda k: slab(wnew, NSA_KV_HEADS + k), lambda k: causal,
               _key_offsets(npad, 0), 1)
        _nsa_combine(gt_ref, oc_ref, o_ref, l_ref, acc_ref, tq)


def _nsa_sample(u, grp, o_cmp, sel, slc_kv, win_state, layer):
    t = grp.t
    n_win = win_state.shape[2]
    assert n_win == WINDOW and grp.past % SLC_BLOCK == 0 and t <= SLC_BLOCK and grp.row0 % t == 0
    n = _pick_tile(grp.past, (2048, 1024, 512, 256, 128))
    kvw = 2 * NSA_KV_W
    rb = grp.row0 // t
    tok = lambda width, col: pl.BlockSpec((t, width), lambda b, c: (rb + b, col))
    own = lambda width: pl.BlockSpec((t, width), lambda b, c: (b, 0))
    return pl.pallas_call(
        functools.partial(_nsa_sample_body, past=grp.past, n_win=n_win),
        grid=(grp.b, grp.past // n),
        in_specs=[
            tok(NSA_W, 0),
            pl.BlockSpec((1, 4, n, HEAD_DIM), lambda b, c: (b, 0, c, 0)),
            tok(kvw, (C_ROWS + kvw) // kvw),
            pl.BlockSpec((1, 1, n_win, kvw), lambda b, c: (layer, b, 0, 0)),
            tok(kvw, C_WIN // kvw),
            own(sel.shape[1]),
            own(NSA_W),
            tok(U_TILE, C_SMALL // U_TILE),
        ],
        out_specs=own(NSA_W),
        out_shape=jax.ShapeDtypeStruct((grp.b * t, NSA_W), F32),
        scratch_shapes=_nsa_state(t),
        compiler_params=_cparams(("parallel", "arbitrary")),
        name="nsa_attn_cached",
    )(u, slc_kv, u, win_state, u, sel, o_cmp, u)


def _diff_update(q_ref, group_of, bias, krel, m_ref, l_ref, acc_ref):
    tq = q_ref.shape[0]
    low = lax.broadcasted_iota(jnp.int32, (tq, HEAD_DIM), 1) < DIFF_QK_DIM
    scores, shifts = [], []
    for h in range(DIFF_HEADS):
        qh = q_ref[:, h * HEAD_DIM:(h + 1) * HEAD_DIM]
        q2 = jnp.concatenate([jnp.where(low, qh, 0.0), jnp.where(low, 0.0, qh)], axis=0).astype(BF16)
        scores.append(lax.dot_general(q2, group_of(h), NT, preferred_element_type=F32))
        shift = (_DIFF_SLOPES[h] * LOG2E) * krel
        shift = jnp.broadcast_to(shift, (tq, shift.shape[1])) if bias is None else bias + shift
        shifts += [shift, shift]
    p, alpha = _online_softmax(jnp.concatenate(scores, axis=0), jnp.concatenate(shifts, axis=0), m_ref, l_ref,
                               slice(0, 2 * DIFF_HEADS * tq))
    for h in range(DIFF_HEADS):
        rows = slice(2 * h * tq, 2 * (h + 1) * tq)
        acc_ref[rows] = alpha[rows] * acc_ref[rows] + jnp.dot(p[rows], group_of(DIFF_HEADS + h),
                                                               preferred_element_type=F32)


def _diff_finish(lam_ref, g_ref, o_ref, l_ref, acc_ref, post):
    tq = o_ref.shape[0]
    lam = lam_ref[0]
    for h in range(DIFF_HEADS):
        o1 = _normalised(l_ref, acc_ref, slice(2 * h * tq, (2 * h + 1) * tq))
        o2 = _normalised(l_ref, acc_ref, slice((2 * h + 1) * tq, (2 * h + 2) * tq))
        o_ref[:, h * HEAD_DIM:(h + 1) * HEAD_DIM] = _rms_rows(o1 - lam * o2, g_ref[...]) * post


def _diff_prompt_body(lam_ref, q_ref, kv_ref, g_ref, o_ref, m_ref, l_ref, acc_ref, *, tq, tk, post):
    qi = pl.program_id(1)
    kj = pl.program_id(2)

    @pl.when(kj == 0)
    def _():
        _init_state(m_ref, l_ref, acc_ref)

    @pl.when(kj * tk <= qi * tq + (tq - 1))
    def _():
        kv = kv_ref[...].astype(BF16)
        _diff_update(q_ref, lambda j: kv[:, j * HEAD_DIM:(j + 1) * HEAD_DIM],
                     _mask_bias(_distance(tq, tk, qi * tq - kj * tk) >= 0), _key_offsets(tk, kj * tk - qi * tq),
                     m_ref, l_ref, acc_ref)

    @pl.when(kj == pl.num_programs(2) - 1)
    def _():
        _diff_finish(lam_ref, g_ref, o_ref, l_ref, acc_ref, post)


def _diff_prompt(u, grp, lam, gain, post):
    assert grp.past == 0 and grp.row0 == 0
    t = grp.t
    tq = _pick_tile(t, (256, 128, 64, 32, 16, 8))
    tk = _pick_tile(t, (512, 256, 128))
    nq, nk = t // tq, t // tk
    return pl.pallas_call(
        functools.partial(_diff_prompt_body, tq=tq, tk=tk, post=post),
        grid=(grp.b, nq, nk),
        in_specs=[
            pl.BlockSpec(memory_space=pltpu.SMEM),
            pl.BlockSpec((tq, DIFF_W), lambda b, i, j: (b * nq + i, C_DQ // DIFF_W)),
            pl.BlockSpec((tk, 2 * DIFF_W),
                         lambda b, i, j: (b * nk + jnp.minimum(j, (i * tq + (tq - 1)) // tk), C_DROWS // (2 * DIFF_W))),
            pl.BlockSpec((1, HEAD_DIM), lambda b, i, j: (0, 0)),
        ],
        out_specs=pl.BlockSpec((tq, DIFF_W), lambda b, i, j: (b * nq + i, 0)),
        out_shape=jax.ShapeDtypeStruct((grp.b * t, DIFF_W), F32),
        scratch_shapes=_diff_state(tq),
        compiler_params=_cparams(("parallel", "parallel", "arbitrary")),
        name="diff_attn_prompt",
    )(lam, u, u, gain)


def _diff_sample_body(pt_ref, lam_ref, q_ref, *rest, pps, past, post):
    pages = rest[:pps]
    new_ref, g_ref, o_ref, m_ref, l_ref, acc_ref = rest[pps:]
    del pt_ref
    c = pl.program_id(1)
    tq = q_ref.shape[0]
    n = pps * PAGE_SIZE

    @pl.when(c == 0)
    def _():
        _init_state(m_ref, l_ref, acc_ref)

    _diff_update(q_ref, lambda j: _page_piece(pages, j), None, _key_offsets(n, c * n - past), m_ref, l_ref, acc_ref)

    @pl.when(c == pl.num_programs(1) - 1)
    def _():
        npad = V7X_LANES
        keep = jnp.logical_and(_distance(tq, npad, 0) >= 0, lax.broadcasted_iota(jnp.int32, (tq, npad), 1) < tq)
        new = _pad_keys(new_ref[...], npad).astype(BF16)
        _diff_update(q_ref, lambda j: new[:, j * HEAD_DIM:(j + 1) * HEAD_DIM], _mask_bias(keep),
                     _key_offsets(npad, 0), m_ref, l_ref, acc_ref)
        _diff_finish(lam_ref, g_ref, o_ref, l_ref, acc_ref, post)


def _diff_sample(u, grp, lam, gain, post, cache, page_table, layer):
    t = grp.t
    assert grp.past % PAGE_SIZE == 0 and grp.row0 % t == 0
    n_pages = grp.past // PAGE_SIZE
    pps = _pages_per_step(n_pages)
    rb = grp.row0 // t
    return pl.pallas_call(
        functools.partial(_diff_sample_body, pps=pps, past=grp.past, post=post),
        grid_spec=pltpu.PrefetchScalarGridSpec(
            num_scalar_prefetch=1,
            grid=(grp.b, n_pages // pps),
            in_specs=[
                pl.BlockSpec(memory_space=pltpu.SMEM),
                pl.BlockSpec((t, DIFF_W), lambda b, c, pt: (rb + b, C_DQ // DIFF_W)),
            ] + _page_specs(layer, n_pages, pps) + [
                pl.BlockSpec((t, 2 * DIFF_W), lambda b, c, pt: (rb + b, C_DROWS // (2 * DIFF_W))),
                pl.BlockSpec((1, HEAD_DIM), lambda b, c, pt: (0, 0)),
            ],
            out_specs=pl.BlockSpec((t, DIFF_W), lambda b, c, pt: (b, 0)),
            scratch_shapes=_diff_state(t),
        ),
        out_shape=jax.ShapeDtypeStruct((grp.b * t, DIFF_W), F32),
        compiler_params=_cparams(("parallel", "arbitrary")),
        name="diff_attn_paged",
    )(page_table, lam, u, *([cache] * pps), u, gain)


def _mlstm_body(xq_ref, xk_ref, v_ref, og_ref, bq_ref, bk_ref, wq_ref, wk_ref, cq_ref, ck_ref, li_ref, lf_ref,
                c0_ref, n0_ref, m0_ref, g_ref, h_ref, c_ref, n_ref, m_ref, q_s, k_s, *, lc, nc):
    t = xq_ref.shape[0]
    row = lax.broadcasted_iota(jnp.int32, (lc, lc), 0)
    col = lax.broadcasted_iota(jnp.int32, (lc, lc), 1)
    tril = row >= col
    triu = row <= col
    eye = row == col
    gain = g_ref[...]

    def conv(x_ref, buf_ref, w_ref, b_ref):
        ext = jnp.concatenate([buf_ref[0], x_ref[...]], axis=0)
        y = b_ref[...]
        for i in range(CONV_WIDTH):
            shifted = ext if i == CONV_WIDTH - 1 else pltpu.roll(ext, CONV_WIDTH - 1 - i, 0)
            y = y + shifted[V7X_SUBLANES:V7X_SUBLANES + t] * w_ref[i:i + 1, :]
        return y * jax.nn.sigmoid(y)

    q_s[...] = conv(xq_ref, bq_ref, wq_ref, cq_ref) * MLSTM_DIM ** -0.5
    k_s[...] = conv(xk_ref, bk_ref, wk_ref, ck_ref)

    def to_col(r):
        return jnp.sum(jnp.where(eye, jnp.broadcast_to(r, (lc, lc)), 0.0), axis=1, keepdims=True)

    def step(ci, carry):
        c, n, m = carry
        rows = pl.ds(ci * lc if nc == 1 else pl.multiple_of(ci * lc, lc), lc)
        qc = q_s[rows, :]
        kc = k_s[rows, :]
        vc = v_ref[rows, :]
        li = li_ref[0, pl.ds(ci, 1), :]
        lf = lf_ref[0, pl.ds(ci, 1), :]
        lf_b = jnp.broadcast_to(lf, (lc, lc))
        bcum_c = jnp.sum(jnp.where(tril, lf_b, 0.0), axis=1, keepdims=True)
        bcum_r = jnp.sum(jnp.where(triu, jnp.broadcast_to(to_col(lf), (lc, lc)), 0.0), axis=0, keepdims=True)
        dmat = jnp.where(tril, bcum_c - bcum_r + li, NEG_INF)
        inter = bcum_c + m
        mj = jnp.maximum(inter, jnp.max(dmat, axis=1, keepdims=True))
        wts = jnp.exp(dmat - mj)
        a = jnp.exp(inter - mj)
        sqk = lax.dot_general(qc, kc, NT, precision=HIGHEST, preferred_element_type=F32) * wts
        num = a * jnp.dot(qc, c, precision=HIGHEST, preferred_element_type=F32) + \
            jnp.dot(sqk, vc, precision=HIGHEST, preferred_element_type=F32)
        den = a * jnp.sum(qc * n, axis=1, keepdims=True) + jnp.sum(sqk, axis=1, keepdims=True)
        h = num / jnp.maximum(jnp.abs(den), jnp.exp(-mj))
        h_ref[rows, :] = _rms_rows(h, gain) * jax.nn.sigmoid(og_ref[rows, :])
        b_last = jnp.sum(lf, axis=1, keepdims=True)
        gl = b_last - bcum_r + li
        m_new = jnp.maximum(b_last + m, jnp.max(gl, axis=1, keepdims=True))
        decay = jnp.exp(b_last + m - m_new)
        kw = kc * to_col(jnp.exp(gl - m_new))
        c_new = decay * c + lax.dot_general(kw, vc, (((0,), (0,)), ((), ())), precision=HIGHEST,
                                            preferred_element_type=F32)
        n_new = decay * n + jnp.sum(kw, axis=0, keepdims=True)
        return c_new, n_new, m_new

    init = (c0_ref[0], n0_ref[0], m0_ref[0][:, 0:1])
    c, n, m = step(0, init) if nc == 1 else lax.fori_loop(0, nc, step, init, unroll=2)
    c_ref[0] = c
    n_ref[0] = n
    m_ref[0] = jnp.broadcast_to(m, (1, MLSTM_DIM))


def _mlstm(u, grp, lp, c0, n0, m0, conv_buf):
    b, t = grp.b, grp.t
    assert grp.row0 % t == 0 and t % V7X_SUBLANES == 0
    bh = b * MLSTM_HEADS
    hd = MLSTM_DIM
    lc = MLSTM_CHUNK if t % MLSTM_CHUNK == 0 else t
    nc = t // lc
    rb = grp.row0 // t
    gates = u[grp.row0:grp.row0 + b * t, C_SMALL:C_SMALL + U_TILE].reshape(b, t, U_TILE)
    log_i = gates[:, :, G_MI:G_MI + MLSTM_HEADS] + lp['ml_gate_b'][0]
    log_f = jax.nn.log_sigmoid(gates[:, :, G_MF:G_MF + MLSTM_HEADS] + lp['ml_gate_b'][1])
    log_i = log_i.transpose(0, 2, 1).reshape(bh, nc, lc)
    log_f = log_f.transpose(0, 2, 1).reshape(bh, nc, lc)
    buf8 = jnp.pad(conv_buf, ((0, 0), (V7X_SUBLANES - (CONV_WIDTH - 1), 0), (0, 0)))
    w8 = jnp.pad(lp['ml_conv_w'], ((0, V7X_SUBLANES - CONV_WIDTH), (0, 0)))
    cb = lp['ml_conv_b'].reshape(1, 2 * MLSTM_W)
    tok = lambda col0: pl.BlockSpec((t, hd), lambda i, h: (rb + i, col0 // hd + h))
    buf = lambda off: pl.BlockSpec((1, V7X_SUBLANES, hd), lambda i, h: (i, 0, off + h))
    wsp = lambda off: pl.BlockSpec((V7X_SUBLANES, hd), lambda i, h: (0, off + h))
    bsp = lambda off: pl.BlockSpec((1, hd), lambda i, h: (0, off + h))
    gate = pl.BlockSpec((1, nc, lc), lambda i, h: (i * MLSTM_HEADS + h, 0, 0))
    vec = pl.BlockSpec((1, 1, hd), lambda i, h: (i * MLSTM_HEADS + h, 0, 0))
    mat = pl.BlockSpec((1, hd, hd), lambda i, h: (i * MLSTM_HEADS + h, 0, 0))
    h, c, n, m = pl.pallas_call(
        functools.partial(_mlstm_body, lc=lc, nc=nc),
        grid=(b, MLSTM_HEADS),
        in_specs=[tok(C_MQ), tok(C_MK), tok(C_MV), tok(C_MO), buf(0), buf(MLSTM_HEADS), wsp(0), wsp(MLSTM_HEADS),
                  bsp(0), bsp(MLSTM_HEADS), gate, gate, mat, vec, vec, pl.BlockSpec((1, hd), lambda i, h: (0, 0))],
        out_specs=[pl.BlockSpec((t, hd), lambda i, h: (i, h)), mat, vec, vec],
        out_shape=[
            jax.ShapeDtypeStruct((b * t, MLSTM_W), F32),
            jax.ShapeDtypeStruct((bh, hd, hd), F32),
            jax.ShapeDtypeStruct((bh, 1, hd), F32),
            jax.ShapeDtypeStruct((bh, 1, hd), F32),
        ],
        scratch_shapes=[pltpu.VMEM((t, hd), F32), pltpu.VMEM((t, hd), F32)],
        compiler_params=_cparams(("parallel", "parallel")),
        name="mlstm",
    )(u, u, u, u, buf8, buf8, w8, w8, cb, cb, log_i, log_f, c0.reshape(bh, hd, hd), n0.reshape(bh, 1, hd),
      jnp.broadcast_to(m0.reshape(bh, 1, 1), (bh, 1, hd)), lp['ml_out_norm'].reshape(1, hd))
    return (h, c.reshape(b, MLSTM_HEADS, hd, hd), n.reshape(b, MLSTM_HEADS, hd), m[:, 0, 0].reshape(b, MLSTM_HEADS))


def _cache_rows_body(nsa_ref, diff_ref, *rest):
    nsa_out, diff_out = rest[-2:]
    tm = nsa_ref.shape[0]
    for j in range(ROW_GROUPS):
        cols = slice(j * HEAD_DIM, (j + 1) * HEAD_DIM)
        nsa_out[pl.ds(j, tm, stride=ROW_GROUPS), :] = nsa_ref[:, cols]
        diff_out[pl.ds(j, tm, stride=ROW_GROUPS), :] = diff_ref[:, cols]


def _cache_rows(u, layer, depth, prev):
    t = u.shape[0]
    tm = _pick_tile(t, (256, 128, 64, 32, 16, 8))
    width = ROW_GROUPS * HEAD_DIM
    in_specs = [pl.BlockSpec((tm, width), lambda i: (i, C_ROWS // width)),
                pl.BlockSpec((tm, width), lambda i: (i, C_DROWS // width))]
    args = [u, u]
    aliases = {}
    if prev is not None:
        in_specs += [pl.BlockSpec(memory_space=pl.ANY)] * 2
        args += list(prev)
        aliases = {2: 0, 3: 1}
    out = pl.BlockSpec((None, tm * ROW_GROUPS, HEAD_DIM), lambda i: (layer, i, 0))
    shape = jax.ShapeDtypeStruct((depth, t * ROW_GROUPS, HEAD_DIM), F32)
    return pl.pallas_call(
        _cache_rows_body,
        grid=(t // tm,),
        in_specs=in_specs,
        out_specs=[out, out],
        out_shape=[shape, shape],
        input_output_aliases=aliases,
        compiler_params=_cparams(("parallel",)),
        name="cache_rows",
    )(*args)


def _pack_w_in(w_in):
    cuts = np.cumsum((0,) + IN_SPLITS)
    cols = [w_in[..., cuts[i]:cuts[i + 1]] for i in range(len(IN_SPLITS))]
    packed = jnp.concatenate([cols[i] for i in _MAIN_ORDER] + [cols[i] for i in _SMALL_ORDER], axis=-1)
    pad = [(0, 0)] * (packed.ndim - 1) + [(0, U_COLS - packed.shape[-1])]
    return jnp.pad(packed, pad).astype(BF16)


def _u_gain(lp):
    ones = lambda n: jnp.ones((n,), F32)
    g = jnp.concatenate([
        jnp.tile(lp['nsa_q_norm'], NSA_HEADS) * (HEAD_DIM ** -0.5 * LOG2E),
        ones(2 * NSA_KV_W), jnp.tile(lp['nsa_k_norm'][1], NSA_KV_HEADS), ones(NSA_KV_W),
        jnp.tile(lp['nsa_k_norm'][2], NSA_KV_HEADS), ones(NSA_KV_W),
        jnp.tile(lp['diff_q_norm'], 2 * DIFF_HEADS) * (DIFF_QK_DIM ** -0.5 * LOG2E),
        jnp.tile(lp['diff_k_norm'], 2 * DIFF_HEADS),
        ones(U_COLS - (C_DROWS + DIFF_W))])
    return g.reshape(1, U_COLS)


def _group_rows(u, grp, col0, width):
    return u[grp.row0:grp.row0 + grp.b * grp.t, col0:col0 + width].reshape(grp.b, grp.t, width)


def _tail_rows(buf, new, keep):
    n_new = min(keep, new.shape[1])
    parts = [buf[:, buf.shape[1] - (keep - n_new):], new[:, new.shape[1] - n_new:]] if keep > n_new else \
        [new[:, new.shape[1] - n_new:]]
    return jnp.concatenate(parts, axis=1)


def _trunk(x_prompt, x_sample, cache_nsa, cache_diff, state_win, state_ml_c, state_ml_n, state_ml_m, state_ml_conv,
           page_table, p):
    bp, tp, d = x_prompt.shape
    bs, ts, _ = x_sample.shape
    depth = p['w_in'].shape[0]
    past = page_table.shape[1] * PAGE_SIZE
    gp = _Group(bp, tp, 0, 0)
    gs = _Group(bs, ts, bp * tp, past)
    n_p = bp * tp
    w_in = _pack_w_in(p['w_in'])
    w_out = p['w_out'].astype(BF16)
    w13 = (p['ffn1_w13'].astype(BF16), p['ffn2_w13'].astype(BF16))
    w2 = (p['ffn1_w2'].astype(BF16), p['ffn2_w2'].astype(BF16))
    per_layer = ('mix_norm', 'nsa_q_norm', 'nsa_k_norm', 'nsa_cmp_pos', 'nsa_cmp_w1', 'nsa_cmp_w2', 'diff_q_norm',
                 'diff_k_norm', 'diff_lambda', 'diff_out_norm', 'ml_conv_w', 'ml_conv_b', 'ml_gate_b', 'ml_out_norm')
    cache_nsa = _page_view(cache_nsa)
    cache_diff = _page_view(cache_diff)
    win_state = state_win.reshape(state_win.shape[:3] + (2 * NSA_KV_W,))
    pt = page_table.reshape(-1)
    zeros = lambda *s: jnp.zeros(s, F32)
    p_state = (zeros(bp, MLSTM_HEADS, MLSTM_DIM, MLSTM_DIM), zeros(bp, MLSTM_HEADS, MLSTM_DIM),
               zeros(bp, MLSTM_HEADS), zeros(bp, CONV_WIDTH - 1, 2 * MLSTM_W))
    p_win0 = zeros(bp, 0, 2, NSA_KV_HEADS, HEAD_DIM)
    row_arrays = None

    x = jnp.concatenate([x_prompt.reshape(n_p, d), x_sample.reshape(bs * ts, d)], axis=0)
    outs = []
    for l in range(depth):
        lp = {name: p[name][l] for name in per_layer}
        lam_init = 0.8 - 0.6 * math.exp(-0.3 * l)
        lam_p = lp['diff_lambda']
        lam = (jnp.exp(jnp.sum(lam_p[0] * lam_p[1])) - jnp.exp(jnp.sum(lam_p[2] * lam_p[3])) + lam_init).reshape(1)
        dgain = lp['diff_out_norm'].reshape(1, HEAD_DIM)
        h = _ffn(x, p['ffn1_norm'][l], w13[0], w2[0], l)
        u = _inproj(h, lp['mix_norm'], w_in, _u_gain(lp), l)

        summ_p, n_cmp_p = _compress_prompt(u, gp, lp)
        oc_p, sel_p = _cmp_select(u, gp, summ_p, n_cmp_p)
        nsa_p = _nsa_prompt(u, gp, oc_p, sel_p)
        diff_p = _diff_prompt(u, gp, lam, dgain, 1.0 - lam_init)
        ml_p, c_p, nn_p, m_p = _mlstm(u, gp, lp, *p_state)

        summ_s, n_cmp_s, slc_kv = _compress_sample(u, gs, lp, cache_nsa, pt, l)
        oc_s, sel_s = _cmp_select(u, gs, summ_s, n_cmp_s)
        nsa_s = _nsa_sample(u, gs, oc_s, sel_s, slc_kv, win_state, l)
        diff_s = _diff_sample(u, gs, lam, dgain, 1.0 - lam_init, cache_diff, pt, l)
        ml_s, c_s, nn_s, m_s = _mlstm(u, gs, lp, state_ml_c[l], state_ml_n[l], state_ml_m[l], state_ml_conv[l])

        h = _outproj(h, jnp.concatenate([nsa_p, nsa_s]), jnp.concatenate([diff_p, diff_s]),
                     jnp.concatenate([ml_p, ml_s]), w_out, l)
        x = _ffn(h, p['ffn2_norm'][l], w13[1], w2[1], l)

        row_arrays = _cache_rows(u, l, depth, row_arrays)
        win_shape = (2, NSA_KV_HEADS, HEAD_DIM)
        keep_p, keep_s = min(WINDOW, tp), min(WINDOW, state_win.shape[2] + ts)
        win_new_p = _group_rows(u, gp, C_WIN, 2 * NSA_KV_W)[:, tp - min(keep_p, tp):]
        win_new_s = _group_rows(u, gs, C_WIN, 2 * NSA_KV_W)[:, ts - min(keep_s, ts):]
        win_rows_p = _tail_rows(p_win0, win_new_p.reshape(win_new_p.shape[:2] + win_shape), keep_p)
        win_rows_s = _tail_rows(state_win[l], win_new_s.reshape(win_new_s.shape[:2] + win_shape), keep_s)
        conv_p = _tail_rows(p_state[3], _group_rows(u, gp, C_MQ, 2 * MLSTM_W), CONV_WIDTH - 1)
        conv_s = _tail_rows(state_ml_conv[l], _group_rows(u, gs, C_MQ, 2 * MLSTM_W), CONV_WIDTH - 1)
        outs.append((win_rows_p, win_rows_s, c_p, c_s, nn_p, nn_s, m_p, m_s, conv_p, conv_s))
    stacked = [jnp.stack(z) for z in zip(*outs)]
    nsa_rows, diff_rows = row_arrays
    n_rows = n_p * ROW_GROUPS

    def split(a, shape):
        return (a[:, :n_rows].reshape((depth, bp, tp) + shape), a[:, n_rows:].reshape((depth, bs, ts) + shape))

    return ((x[:n_p].reshape(bp, tp, d), x[n_p:].reshape(bs, ts, d)) + split(nsa_rows, (4, NSA_KV_HEADS, HEAD_DIM))
            + split(diff_rows, (2, DIFF_HEADS, HEAD_DIM)) + tuple(stacked))


def kernel(x_prompt, x_sample, cache_nsa, cache_diff, state_win, state_ml_c, state_ml_n, state_ml_m, state_ml_conv,
           page_table, ffn1_norm, ffn1_w13, ffn1_w2, mix_norm, w_in, nsa_q_norm, nsa_k_norm, nsa_cmp_pos, nsa_cmp_w1,
           nsa_cmp_w2, diff_q_norm, diff_k_norm, diff_lambda, diff_out_norm, ml_conv_w, ml_conv_b, ml_gate_b,
           ml_out_norm, w_out, ffn2_norm, ffn2_w13, ffn2_w2):
    params = dict(ffn1_norm=ffn1_norm, ffn1_w13=ffn1_w13, ffn1_w2=ffn1_w2, mix_norm=mix_norm, w_in=w_in,
                  nsa_q_norm=nsa_q_norm, nsa_k_norm=nsa_k_norm, nsa_cmp_pos=nsa_cmp_pos, nsa_cmp_w1=nsa_cmp_w1,
                  nsa_cmp_w2=nsa_cmp_w2, diff_q_norm=diff_q_norm, diff_k_norm=diff_k_norm, diff_lambda=diff_lambda,
                  diff_out_norm=diff_out_norm, ml_conv_w=ml_conv_w, ml_conv_b=ml_conv_b, ml_gate_b=ml_gate_b,
                  ml_out_norm=ml_out_norm, w_out=w_out, ffn2_norm=ffn2_norm, ffn2_w13=ffn2_w13, ffn2_w2=ffn2_w2)
    return _trunk(x_prompt, x_sample, cache_nsa, cache_diff, state_win, state_ml_c, state_ml_n, state_ml_m,
                  state_ml_conv, page_table, params)
```

```python
import functools
import math
from typing import NamedTuple

import numpy as np
import jax
import jax.numpy as jnp
from jax import lax
from jax.experimental import pallas as pl
from jax.experimental.pallas import tpu as pltpu

F32 = jnp.float32
BF16 = jnp.bfloat16
HIGHEST = lax.Precision.HIGHEST
NT = (((1,), (1,)), ((), ()))

D_MODEL = 2048
D_FF = 5632
PAGE_SIZE = 128
HEAD_DIM = 128
ROW_GROUPS = 8
NSA_HEADS = 8
NSA_KV_HEADS = 2
NSA_GROUP = NSA_HEADS // NSA_KV_HEADS
CMP_BLOCK = 32
CMP_STRIDE = 16
SLC_BLOCK = 64
SLC_SHIFT = 6
SLC_TOPN = 16
WINDOW = 512
DIFF_HEADS = 4
DIFF_QK_DIM = HEAD_DIM // 2
MLSTM_HEADS = 4
MLSTM_DIM = 128
MLSTM_CHUNK = 64
CONV_WIDTH = 4
NORM_EPS = 1e-6
NEG_INF = -1e30
M_INIT = -1e29
LOG2E = 1.4426950408889634
FORCE_BONUS = 1e3

NSA_W = NSA_HEADS * HEAD_DIM
NSA_KV_W = NSA_KV_HEADS * HEAD_DIM
DIFF_W = DIFF_HEADS * HEAD_DIM
MLSTM_W = MLSTM_HEADS * MLSTM_DIM
IN_SPLITS = (NSA_W, 6 * NSA_KV_W, 3 * NSA_HEADS, DIFF_W, DIFF_W, DIFF_W,
             MLSTM_W, MLSTM_W, MLSTM_W, MLSTM_HEADS, MLSTM_HEADS, MLSTM_W)
IN_COLS = sum(IN_SPLITS)

V7X_LANES = 128
V7X_SUBLANES = 8
V7X_VMEM_LIMIT_BYTES = 56 * 1024 * 1024

U_TILE = 256
C_NQ = 0
C_ROWS = C_NQ + NSA_W
C_WIN = C_ROWS + 4 * NSA_KV_W
C_DQ = C_WIN + 2 * NSA_KV_W
C_DROWS = C_DQ + DIFF_W
C_MQ = C_DROWS + 2 * DIFF_W
C_MK = C_MQ + MLSTM_W
C_MV = C_MK + MLSTM_W
C_MO = C_MV + MLSTM_W
C_SMALL = C_MO + MLSTM_W
U_COLS = C_SMALL + U_TILE
U_STEP = 5 * U_TILE
G_NSA, G_MI, G_MF = 0, 3 * NSA_HEADS, 3 * NSA_HEADS + MLSTM_HEADS
U_KINDS = (1, 1, 1, 1, 0, 0, 1, 0, 1, 0, 2, 2, 2, 2, 0, 0, 0, 0, 0, 0, 0, 0, 0, 0, 0)
_MAIN_ORDER = (0, 1, 3, 4, 5, 6, 7, 8, 11)
_SMALL_ORDER = (2, 9, 10)


class _Group(NamedTuple):
    b: int
    t: int
    row0: int
    past: int


def _round_up(n, m):
    return -(-n // m) * m


def _cparams(semantics):
    return pltpu.CompilerParams(dimension_semantics=semantics, vmem_limit_bytes=V7X_VMEM_LIMIT_BYTES)


def _pick_tile(n, prefs):
    for p in prefs:
        if n % p == 0:
            return p
    return n


def _alibi(n):
    return [float(2.0 ** (-8.0 * i / n)) for i in range(1, n + 1)]


def _rms_rows(x, g):
    inv = lax.rsqrt(jnp.mean(x * x, axis=-1, keepdims=True) + NORM_EPS)
    return x * inv * g


def _ffn_body(x_ref, g_ref, wa_ref, wb_ref, w2_ref, o_ref, n_ref):
    @pl.when(pl.program_id(1) == 0)
    def _():
        x = x_ref[...]
        n_ref[...] = _rms_rows(x, g_ref[...]).astype(BF16)
        o_ref[...] = x

    n = n_ref[...]
    a = jnp.dot(n, wa_ref[...], preferred_element_type=F32)
    b = jnp.dot(n, wb_ref[...], preferred_element_type=F32)
    act = (0.5 * a * jax.nn.sigmoid(a) * b).astype(BF16)
    o_ref[...] += jnp.dot(act, w2_ref[...], preferred_element_type=F32)


def _ffn(x, g, w13, w2, layer):
    t, d = x.shape
    f = w2.shape[1]
    tm = _pick_tile(t, (768, 512, 256, 128, 64, 32, 16, 8))
    tf = _pick_tile(f, (512, 256, 128))
    nf = f // tf
    return pl.pallas_call(
        _ffn_body,
        grid=(t // tm, nf),
        in_specs=[
            pl.BlockSpec((tm, d), lambda i, j: (i, 0)),
            pl.BlockSpec((1, d), lambda i, j: (0, 0)),
            pl.BlockSpec((None, d, tf), lambda i, j: (layer, 0, j)),
            pl.BlockSpec((None, d, tf), lambda i, j: (layer, 0, j + nf)),
            pl.BlockSpec((None, tf, d), lambda i, j: (layer, j, 0)),
        ],
        out_specs=pl.BlockSpec((tm, d), lambda i, j: (i, 0)),
        out_shape=jax.ShapeDtypeStruct((t, d), F32),
        scratch_shapes=[pltpu.VMEM((tm, d), BF16)],
        compiler_params=_cparams(("parallel", "arbitrary")),
        name="ffn",
    )(x, g.reshape(1, d), w13, w13, w2)


def _inproj_body(kind_ref, x_ref, g_ref, w_ref, gain_ref, o_ref, n_ref):
    j = pl.program_id(1)

    @pl.when(j == 0)
    def _():
        n_ref[...] = _rms_rows(x_ref[...], g_ref[...]).astype(BF16)

    y = jnp.dot(n_ref[...], w_ref[...], preferred_element_type=F32)
    gain = gain_ref[...]
    for i in range(U_STEP // U_TILE):
        kind = kind_ref[j * (U_STEP // U_TILE) + i]
        halves = [slice(i * U_TILE + h * HEAD_DIM, i * U_TILE + (h + 1) * HEAD_DIM)
                  for h in range(U_TILE // HEAD_DIM)]

        @pl.when(kind == 0)
        def _(halves=halves):
            for cols in halves:
                o_ref[:, cols] = y[:, cols]

        @pl.when(kind == 1)
        def _(halves=halves):
            for cols in halves:
                o_ref[:, cols] = _rms_rows(y[:, cols], gain[:, cols])

        @pl.when(kind == 2)
        def _(halves=halves):
            low = lax.broadcasted_iota(jnp.int32, (y.shape[0], HEAD_DIM), 1) < DIFF_QK_DIM
            for cols in halves:
                yh = y[:, cols]
                sq = yh * yh
                s_lo = jnp.sum(jnp.where(low, sq, 0.0), axis=-1, keepdims=True)
                s_hi = jnp.sum(jnp.where(low, 0.0, sq), axis=-1, keepdims=True)
                inv = lax.rsqrt(jnp.where(low, s_lo, s_hi) * (1.0 / DIFF_QK_DIM) + NORM_EPS)
                o_ref[:, cols] = yh * inv * gain[:, cols]


def _inproj(x, g, w, gain, layer):
    t, d = x.shape
    tm = _pick_tile(t, (768, 512, 256, 128, 64, 32, 16, 8))
    return pl.pallas_call(
        _inproj_body,
        grid=(t // tm, U_COLS // U_STEP),
        in_specs=[
            pl.BlockSpec(memory_space=pltpu.SMEM),
            pl.BlockSpec((tm, d), lambda i, j: (i, 0)),
            pl.BlockSpec((1, d), lambda i, j: (0, 0)),
            pl.BlockSpec((None, d, U_STEP), lambda i, j: (layer, 0, j)),
            pl.BlockSpec((1, U_STEP), lambda i, j: (0, j)),
        ],
        out_specs=pl.BlockSpec((tm, U_STEP), lambda i, j: (i, j)),
        out_shape=jax.ShapeDtypeStruct((t, U_COLS), F32),
        scratch_shapes=[pltpu.VMEM((tm, d), BF16)],
        compiler_params=_cparams(("parallel", "arbitrary")),
        name="inproj",
    )(jnp.asarray(U_KINDS, jnp.int32), x, g.reshape(1, d), w, gain)


def _outproj_body(h_ref, a_ref, b_ref, c_ref, w_ref, y_ref):
    ka, kb = a_ref.shape[1], b_ref.shape[1]
    y = h_ref[...] + jnp.dot(a_ref[...].astype(BF16), w_ref[0:ka, :], preferred_element_type=F32)
    y = y + jnp.dot(b_ref[...].astype(BF16), w_ref[ka:ka + kb, :], preferred_element_type=F32)
    y_ref[...] = y + jnp.dot(c_ref[...].astype(BF16), w_ref[ka + kb:, :], preferred_element_type=F32)


def _outproj(h, o_nsa, o_diff, o_ml, w, layer):
    t, d = h.shape
    tm = _pick_tile(t, (768, 512, 256, 128, 64, 32, 16, 8))
    row = lambda a: pl.BlockSpec((tm, a.shape[1]), lambda i: (i, 0))
    return pl.pallas_call(
        _outproj_body,
        grid=(t // tm,),
        in_specs=[row(h), row(o_nsa), row(o_diff), row(o_ml),
                  pl.BlockSpec((None,) + w.shape[1:], lambda i: (layer, 0, 0))],
        out_specs=row(h),
        out_shape=jax.ShapeDtypeStruct((t, d), F32),
        compiler_params=_cparams(("parallel",)),
        name="outproj",
    )(h, o_nsa, o_diff, o_ml, w)


def _online_update(s, bias, krel, slopes, v, m_ref, l_ref, acc_ref, base, tq):
    def shift_of(slope):
        shift = (slope * LOG2E) * krel
        return jnp.broadcast_to(shift, (tq, shift.shape[1])) if bias is None else bias + shift

    rows = slice(base, base + len(slopes) * tq)
    p, alpha = _online_softmax(s, jnp.concatenate([shift_of(sl) for sl in slopes], axis=0), m_ref, l_ref, rows)
    acc_ref[rows] = alpha * acc_ref[rows] + jnp.dot(p, v, preferred_element_type=F32)


def _online_softmax(s, shift, m_ref, l_ref, rows):
    sg = s + shift
    m_old = m_ref[rows]
    m_new = jnp.maximum(m_old, jnp.max(sg, axis=-1, keepdims=True))
    p = jnp.exp2(sg - jnp.concatenate([m_new] * (sg.shape[1] // V7X_LANES), axis=1))
    alpha = jnp.exp2(m_old - m_new)
    l_ref[rows] = alpha * l_ref[rows] + jnp.sum(p, axis=-1, keepdims=True)
    m_ref[rows] = m_new
    return p.astype(BF16), alpha


def _init_state(m_ref, l_ref, acc_ref):
    m_ref[...] = jnp.full(m_ref.shape, M_INIT, F32)
    l_ref[...] = jnp.zeros(l_ref.shape, F32)
    acc_ref[...] = jnp.zeros(acc_ref.shape, F32)


def _mask_bias(keep):
    return jnp.where(keep, 0.0, NEG_INF)


def _key_offsets(n, first):
    return (first + lax.broadcasted_iota(jnp.int32, (1, n), 1)).astype(F32)


def _normalised(l_ref, acc_ref, rows):
    return acc_ref[rows] / jnp.maximum(l_ref[rows], 1e-30)


def _distance(tq, n, offset):
    return offset + lax.broadcasted_iota(jnp.int32, (tq, n), 0) - lax.broadcasted_iota(jnp.int32, (tq, n), 1)


def _block_expand(n_spad, n, first_key):
    blk = lax.broadcasted_iota(jnp.int32, (n_spad, n), 0)
    key = first_key + lax.broadcasted_iota(jnp.int32, (n_spad, n), 1)
    return jnp.where(lax.shift_right_logical(key, SLC_SHIFT) == blk, 1.0, 0.0).astype(BF16)


def _stack_heads(q_ref, k):
    return jnp.concatenate([q_ref[:, (k * NSA_GROUP + g) * HEAD_DIM:(k * NSA_GROUP + g + 1) * HEAD_DIM]
                            for g in range(NSA_GROUP)], axis=0).astype(BF16)


def _pad_keys(x, n):
    return jnp.concatenate([x, jnp.zeros((n - x.shape[0], x.shape[1]), x.dtype)], axis=0)


def _nsa_combine(gt_ref, oc_ref, o_ref, l_ref, acc_ref, tq):
    gt = jax.nn.sigmoid(gt_ref[:, 0:V7X_LANES])
    for h in range(NSA_HEADS):
        cols = slice(h * HEAD_DIM, (h + 1) * HEAD_DIM)
        o_slc = _normalised(l_ref, acc_ref, slice(h * tq, (h + 1) * tq))
        o_win = _normalised(l_ref, acc_ref, slice((NSA_HEADS + h) * tq, (NSA_HEADS + h + 1) * tq))
        c0 = G_NSA + 3 * h
        o_ref[:, cols] = gt[:, c0:c0 + 1] * oc_ref[:, cols] + gt[:, c0 + 1:c0 + 2] * o_slc + gt[:, c0 + 2:c0 + 3] * o_win


_NSA_SLOPES = _alibi(NSA_HEADS)
_DIFF_SLOPES = _alibi(DIFF_HEADS)


def _softmax_state(rows):
    return [pltpu.VMEM((rows, V7X_LANES), F32), pltpu.VMEM((rows, V7X_LANES), F32), pltpu.VMEM((rows, HEAD_DIM), F32)]


def _nsa_state(tq):
    return _softmax_state(2 * NSA_HEADS * tq)


def _diff_state(tq):
    return _softmax_state(2 * DIFF_HEADS * tq)


def _cmp_accumulate(rows_of, w_ref, o_ref):
    for c in range(4):
        for r in range(CMP_STRIDE):
            part = jnp.dot(rows_of(c, r).astype(BF16), w_ref[c // 2, r * HEAD_DIM:(r + 1) * HEAD_DIM, :],
                           preferred_element_type=F32)
            if r == 0:
                o_ref[0, c] = part
            else:
                o_ref[0, c] += part


def _cmp_partial_body(x0_ref, x1_ref, x2_ref, x3_ref, w_ref, o_ref):
    srcs = (x0_ref, x1_ref, x2_ref, x3_ref)
    per = x0_ref.shape[0] // CMP_STRIDE
    _cmp_accumulate(lambda c, r: srcs[c][pl.ds(r, per, stride=CMP_STRIDE), :], w_ref, o_ref)


def _cmp_partial_paged_body(pt_ref, *refs, pps):
    del pt_ref
    pages, w_ref, o_ref, kv_ref, slab_ref = refs[:pps], refs[pps], refs[pps + 1], refs[pps + 2], refs[pps + 3]
    for i, pg in enumerate(pages):
        for c in range(4):
            slab_ref[c, i * PAGE_SIZE:(i + 1) * PAGE_SIZE, :] = _page_rows(pg, c)
    per = pps * PAGE_SIZE // CMP_STRIDE
    _cmp_accumulate(lambda c, r: slab_ref[c, pl.ds(r, per, stride=CMP_STRIDE), :], w_ref, o_ref)
    for j in range(4):
        kv_ref[0, j] = _page_piece(pages, 4 + j)


def _cmp_finish_body(*refs, has_tail, n_out):
    if has_tail:
        p_ref, tail_ref, w1_ref, pos_ref, w2_ref, g_ref, o_ref = refs
    else:
        p_ref, w1_ref, pos_ref, w2_ref, g_ref, o_ref = refs
    c = pl.program_id(0)
    p = p_ref[0, 0]
    n = p.shape[0]
    w1 = w1_ref[0]
    pb = jnp.dot(pos_ref[0].astype(BF16), w1, preferred_element_type=F32)
    nxt = pltpu.roll(p[:, HEAD_DIM:], n - 1, 0)
    if has_tail:
        rid = lax.broadcasted_iota(jnp.int32, (V7X_SUBLANES, HEAD_DIM), 0)
        tail8 = jnp.zeros((V7X_SUBLANES, HEAD_DIM), F32)
        xt = tail_ref[...].astype(BF16)
        for r in range(tail_ref.shape[0]):
            d = jnp.dot(xt, w1[r * HEAD_DIM:(r + 1) * HEAD_DIM, HEAD_DIM:], preferred_element_type=F32)
            tail8 = tail8 + jnp.where(rid == r, d, 0.0)
        tail = jnp.sum(tail8, axis=0, keepdims=True)
    else:
        tail = jnp.zeros((1, HEAD_DIM), F32)
    last = lax.broadcasted_iota(jnp.int32, (n, HEAD_DIM), 0) == n - 1
    z = p[:, :HEAD_DIM] + pb[0:1, :HEAD_DIM] + jnp.where(last, tail, nxt) + pb[1:2, HEAD_DIM:]
    y = jnp.dot((z * jax.nn.sigmoid(z)).astype(BF16), w2_ref[0].astype(BF16), preferred_element_type=F32)
    y = jnp.where(c < NSA_KV_HEADS, _rms_rows(y, g_ref[...]), y)
    o_ref[0, 0, 0:n] = y
    if n_out > n:
        o_ref[0, 0, n:n_out] = jnp.zeros((n_out - n, HEAD_DIM), F32)


def _cmp_weights(lp):
    half = CMP_STRIDE * HEAD_DIM
    w1 = lp['nsa_cmp_w1']
    w1ab = jnp.concatenate([w1[:, :half], w1[:, half:]], axis=2).astype(BF16)
    pos = lp['nsa_cmp_pos']
    pos2 = jnp.concatenate([pos[:, :CMP_STRIDE].reshape(2, 1, half), pos[:, CMP_STRIDE:].reshape(2, 1, half),
                            jnp.zeros((2, V7X_SUBLANES - 2, half), F32)], axis=1)
    return w1ab, pos2


def _cmp_finish(p, tail_src, w1ab, pos2, lp, n_out, grp, u):
    b, _, n, _ = p.shape
    half = CMP_STRIDE * HEAD_DIM
    has_tail = tail_src is not None
    in_specs = [pl.BlockSpec((1, 1, n, 2 * HEAD_DIM), lambda c, i: (i, c, 0, 0))]
    args = [p]
    if has_tail:
        rb = grp.row0 // grp.t
        in_specs.append(pl.BlockSpec((grp.t, HEAD_DIM), lambda c, i: (rb + i, C_ROWS // HEAD_DIM + c)))
        args.append(u)
    in_specs += [
        pl.BlockSpec((1, half, 2 * HEAD_DIM), lambda c, i: (c // NSA_KV_HEADS, 0, 0)),
        pl.BlockSpec((1, V7X_SUBLANES, half), lambda c, i: (c // NSA_KV_HEADS, 0, 0)),
        pl.BlockSpec((1, HEAD_DIM, HEAD_DIM), lambda c, i: (c // NSA_KV_HEADS, 0, 0)),
        pl.BlockSpec((1, HEAD_DIM), lambda c, i: (0, 0)),
    ]
    args += [w1ab, pos2, lp['nsa_cmp_w2'], lp['nsa_k_norm'][0:1]]
    return pl.pallas_call(
        functools.partial(_cmp_finish_body, has_tail=has_tail, n_out=n_out),
        grid=(4, b),
        in_specs=in_specs,
        out_specs=pl.BlockSpec((1, 1, n_out, HEAD_DIM), lambda c, i: (i, c, 0, 0)),
        out_shape=jax.ShapeDtypeStruct((b, 4, n_out, HEAD_DIM), F32),
        compiler_params=_cparams(("parallel", "parallel")),
        name="nsa_cmp_finish",
    )(*args)


def _compress_prompt(u, grp, lp):
    assert grp.past == 0 and grp.row0 == 0 and grp.t % (CMP_STRIDE * V7X_SUBLANES) == 0
    w1ab, pos2 = _cmp_weights(lp)
    n = grp.t // CMP_STRIDE
    group = lambda c: pl.BlockSpec((grp.t, HEAD_DIM), lambda i: (i, C_ROWS // HEAD_DIM + c))
    p = pl.pallas_call(
        _cmp_partial_body,
        grid=(grp.b,),
        in_specs=[group(0), group(1), group(2), group(3), pl.BlockSpec(w1ab.shape, lambda i: (0, 0, 0))],
        out_specs=pl.BlockSpec((1, 4, n, 2 * HEAD_DIM), lambda i: (i, 0, 0, 0)),
        out_shape=jax.ShapeDtypeStruct((grp.b, 4, n, 2 * HEAD_DIM), F32),
        compiler_params=_cparams(("parallel",)),
        name="nsa_cmp_partial",
    )(u, u, u, u, w1ab)
    return _cmp_finish(p, None, w1ab, pos2, lp, _round_up(n, V7X_LANES), grp, u), n - 1


def _pages_per_step(n_pages):
    return _pick_tile(n_pages, (16, 8, 4, 2, 1))


def _page_view(cache):
    assert cache.shape[3] * cache.shape[4] == ROW_GROUPS and cache.shape[5] == HEAD_DIM
    return cache.reshape(cache.shape[0], cache.shape[1], PAGE_SIZE * ROW_GROUPS, HEAD_DIM)


def _page_specs(layer, n_pages, pps):
    def spec(i):
        return pl.BlockSpec((1, 1, PAGE_SIZE * ROW_GROUPS, HEAD_DIM),
                            lambda b, c, pt: (layer, pt[b * n_pages + c * pps + i], 0, 0))
    return [spec(i) for i in range(pps)]


def _page_rows(page, j):
    return page[0, 0, pl.ds(j, PAGE_SIZE, stride=ROW_GROUPS), :]


def _page_piece(pages, j):
    return jnp.concatenate([_page_rows(pg, j).astype(BF16) for pg in pages], axis=0)


def _compress_sample(u, grp, lp, cache, page_table, layer):
    assert grp.past % PAGE_SIZE == 0 and grp.t < CMP_STRIDE
    w1ab, pos2 = _cmp_weights(lp)
    n_pages = grp.past // PAGE_SIZE
    pps = _pages_per_step(n_pages)
    per = pps * PAGE_SIZE // CMP_STRIDE
    n = grp.past // CMP_STRIDE
    p, slc_kv = pl.pallas_call(
        functools.partial(_cmp_partial_paged_body, pps=pps),
        grid_spec=pltpu.PrefetchScalarGridSpec(
            num_scalar_prefetch=1,
            grid=(grp.b, n_pages // pps),
            in_specs=_page_specs(layer, n_pages, pps) + [
                pl.BlockSpec(w1ab.shape, lambda b, c, pt: (0, 0, 0))],
            out_specs=[
                pl.BlockSpec((1, 4, per, 2 * HEAD_DIM), lambda b, c, pt: (b, 0, c, 0)),
                pl.BlockSpec((1, 4, pps * PAGE_SIZE, HEAD_DIM), lambda b, c, pt: (b, 0, c, 0)),
            ],
            scratch_shapes=[pltpu.VMEM((4, pps * PAGE_SIZE, HEAD_DIM), F32)],
        ),
        out_shape=[
            jax.ShapeDtypeStruct((grp.b, 4, n, 2 * HEAD_DIM), F32),
            jax.ShapeDtypeStruct((grp.b, 4, grp.past, HEAD_DIM), BF16),
        ],
        compiler_params=_cparams(("parallel", "arbitrary")),
        name="nsa_cmp_partial_paged",
    )(page_table, *([cache] * pps), w1ab)
    return _cmp_finish(p, u, w1ab, pos2, lp, _round_up(n + 1, V7X_LANES), grp, u), n, slc_kv


def _cmp_select_body(q_ref, kv_ref, o_ref, sel_ref, *, tq, q_off, n_cmp, n_slc, n_sel):
    qi = pl.program_id(1)
    n_cpad = kv_ref.shape[2]
    n_spad = sel_ref.shape[1] // NSA_KV_HEADS
    q_lo = q_off + qi * tq
    qpos = q_lo + lax.broadcasted_iota(jnp.int32, (tq, n_cpad), 0)
    ends = lax.broadcasted_iota(jnp.int32, (tq, n_cpad), 1) * CMP_STRIDE + (CMP_BLOCK - 1)
    d = qpos - ends
    mask = d >= 0
    df = d.astype(F32)
    c_lo = lax.broadcasted_iota(jnp.int32, (n_cpad, n_spad), 0) * CMP_STRIDE
    j_lo = lax.broadcasted_iota(jnp.int32, (n_cpad, n_spad), 1) * SLC_BLOCK
    cover = jnp.where(c_lo < j_lo + SLC_BLOCK, jnp.where(c_lo + CMP_BLOCK > j_lo, 1.0, 0.0), 0.0)
    cover = jnp.where(c_lo < n_cmp * CMP_STRIDE, cover, 0.0)
    qp = q_lo + lax.broadcasted_iota(jnp.int32, (tq, n_spad), 0)
    j = lax.broadcasted_iota(jnp.int32, (tq, n_spad), 1)
    cur = lax.shift_right_logical(qp, SLC_SHIFT)
    bonus = FORCE_BONUS * jnp.where(j == 0, 1.0, jnp.where(j == cur, 1.0, jnp.where(j == cur - 1, 1.0, 0.0)))
    valid = j * SLC_BLOCK <= qp
    for k in range(NSA_KV_HEADS):
        q = jnp.concatenate([q_ref[:, (k * NSA_GROUP + g) * HEAD_DIM:(k * NSA_GROUP + g + 1) * HEAD_DIM]
                             for g in range(NSA_GROUP)], axis=0)
        s = lax.dot_general(q, kv_ref[0, k], NT, precision=HIGHEST, preferred_element_type=F32)
        vc = kv_ref[0, NSA_KV_HEADS + k].astype(BF16)
        psum = jnp.zeros((tq, n_cpad), F32)
        for g in range(NSA_GROUP):
            h = k * NSA_GROUP + g
            sg = jnp.where(mask, s[g * tq:(g + 1) * tq] - (_NSA_SLOPES[h] * LOG2E) * df, NEG_INF)
            m = jnp.max(sg, axis=-1, keepdims=True)
            p = jnp.where(mask, jnp.exp2(sg - m), 0.0)
            p = p / jnp.maximum(jnp.sum(p, axis=-1, keepdims=True), 1e-30)
            o_ref[:, h * HEAD_DIM:(h + 1) * HEAD_DIM] = jnp.dot(p.astype(BF16), vc, preferred_element_type=F32)
            psum = psum + p
        imp = jnp.dot(psum, cover, precision=HIGHEST, preferred_element_type=F32)
        score = jnp.where(valid, imp + bonus, NEG_INF)
        cnt = jnp.zeros((tq, n_spad), F32)
        for i in range(n_slc):
            ci = score[:, i:i + 1]
            cnt = cnt + jnp.where(ci > score, 1.0, jnp.where(ci == score, jnp.where(j > i, 1.0, 0.0), 0.0))
        sel_ref[:, k * n_spad:(k + 1) * n_spad] = jnp.where(cnt < n_sel, jnp.where(j < n_slc, 0.0, NEG_INF), NEG_INF)


def _cmp_select(u, grp, summaries, n_cmp):
    n_slc = -(-(grp.past + grp.t) // SLC_BLOCK)
    n_spad = _round_up(n_slc, V7X_LANES)
    n_cpad = summaries.shape[2]
    tq = _pick_tile(grp.t, (256, 128, 64, 32, 16, 8))
    nq = grp.t // tq
    rb = grp.row0 // tq
    body = functools.partial(_cmp_select_body, tq=tq, q_off=grp.past, n_cmp=n_cmp, n_slc=n_slc,
                             n_sel=min(SLC_TOPN, n_slc))
    return pl.pallas_call(
        body,
        grid=(grp.b, nq),
        in_specs=[
            pl.BlockSpec((tq, NSA_W), lambda b, i: (rb + b * nq + i, 0)),
            pl.BlockSpec((1, 4, n_cpad, HEAD_DIM), lambda b, i: (b, 0, 0, 0)),
        ],
        out_specs=[
            pl.BlockSpec((tq, NSA_W), lambda b, i: (b * nq + i, 0)),
            pl.BlockSpec((tq, NSA_KV_HEADS * n_spad), lambda b, i: (b * nq + i, 0)),
        ],
        out_shape=[
            jax.ShapeDtypeStruct((grp.b * grp.t, NSA_W), F32),
            jax.ShapeDtypeStruct((grp.b * grp.t, NSA_KV_HEADS * n_spad), F32),
        ],
        compiler_params=_cparams(("parallel", "parallel")),
        name="nsa_cmp_select",
    )(u, summaries)


def _nsa_prompt_body(q_ref, ks_ref, kw_ref, sel_ref, oc_ref, gt_ref, o_ref, m_ref, l_ref, acc_ref, *, tq, tk):
    qi = pl.program_id(1)
    kj = pl.program_id(2)
    n_spad = sel_ref.shape[1] // NSA_KV_HEADS

    @pl.when(kj == 0)
    def _():
        _init_state(m_ref, l_ref, acc_ref)

    q_lo = qi * tq
    k_lo = kj * tk
    slc_on = k_lo <= q_lo + (tq - 1)
    win_on = jnp.logical_and(slc_on, k_lo + (tk - 1) >= q_lo - WINDOW)

    krel = _key_offsets(tk, k_lo - q_lo)

    @pl.when(slc_on)
    def _():
        causal = _mask_bias(_distance(tq, tk, q_lo - k_lo) >= 0)
        expand = _block_expand(n_spad, tk, k_lo)
        kv = ks_ref[...].astype(BF16)
        for k in range(NSA_KV_HEADS):
            s = lax.dot_general(_stack_heads(q_ref, k), kv[:, k * HEAD_DIM:(k + 1) * HEAD_DIM], NT,
                                preferred_element_type=F32)
            chosen = jnp.dot(sel_ref[:, k * n_spad:(k + 1) * n_spad].astype(BF16), expand,
                             preferred_element_type=F32)
            _online_update(s, causal + chosen, krel, _NSA_SLOPES[k * NSA_GROUP:(k + 1) * NSA_GROUP],
                           kv[:, (NSA_KV_HEADS + k) * HEAD_DIM:(NSA_KV_HEADS + k + 1) * HEAD_DIM],
                           m_ref, l_ref, acc_ref, k * NSA_GROUP * tq, tq)

    @pl.when(win_on)
    def _():
        d = _distance(tq, tk, q_lo - k_lo)
        bias = _mask_bias(jnp.logical_and(d >= 0, d <= WINDOW))
        kv = kw_ref[...].astype(BF16)
        for k in range(NSA_KV_HEADS):
            s = lax.dot_general(_stack_heads(q_ref, k), kv[:, k * HEAD_DIM:(k + 1) * HEAD_DIM], NT,
                                preferred_element_type=F32)
            _online_update(s, bias, krel, _NSA_SLOPES[k * NSA_GROUP:(k + 1) * NSA_GROUP],
                           kv[:, (NSA_KV_HEADS + k) * HEAD_DIM:(NSA_KV_HEADS + k + 1) * HEAD_DIM],
                           m_ref, l_ref, acc_ref, (NSA_KV_HEADS + k) * NSA_GROUP * tq, tq)

    @pl.when(kj == pl.num_programs(2) - 1)
    def _():
        _nsa_combine(gt_ref, oc_ref, o_ref, l_ref, acc_ref, tq)


def _nsa_prompt(u, grp, o_cmp, sel):
    assert grp.past == 0 and grp.row0 == 0
    t = grp.t
    tq = _pick_tile(t, (512, 256, 128, 64, 32, 16, 8))
    tk = _pick_tile(t, (512, 256, 128))
    nq, nk = t // tq, t // tk
    kvw = 2 * NSA_KV_W

    def last_tile(i):
        return (i * tq + (tq - 1)) // tk

    def slc_rows(b, i, j):
        return (b * nk + jnp.minimum(j, last_tile(i)), (C_ROWS + kvw) // kvw)

    def win_rows(b, i, j):
        first = jnp.maximum(i * tq - WINDOW, 0) // tk
        return (b * nk + jnp.clip(j, first, last_tile(i)), C_WIN // kvw)

    return pl.pallas_call(
        functools.partial(_nsa_prompt_body, tq=tq, tk=tk),
        grid=(grp.b, nq, nk),
        in_specs=[
            pl.BlockSpec((tq, NSA_W), lambda b, i, j: (b * nq + i, 0)),
            pl.BlockSpec((tk, kvw), slc_rows),
            pl.BlockSpec((tk, kvw), win_rows),
            pl.BlockSpec((tq, sel.shape[1]), lambda b, i, j: (b * nq + i, 0)),
            pl.BlockSpec((tq, NSA_W), lambda b, i, j: (b * nq + i, 0)),
            pl.BlockSpec((tq, U_TILE), lambda b, i, j: (b * nq + i, C_SMALL // U_TILE)),
        ],
        out_specs=pl.BlockSpec((tq, NSA_W), lambda b, i, j: (b * nq + i, 0)),
        out_shape=jax.ShapeDtypeStruct((grp.b * t, NSA_W), F32),
        scratch_shapes=_nsa_state(tq),
        compiler_params=_cparams(("parallel", "parallel", "arbitrary")),
        name="nsa_attn_prompt",
    )(u, u, u, sel, o_cmp, u)


def _nsa_sample_body(q_ref, kv_ref, knew_ref, wst_ref, wnew_ref, sel_ref, oc_ref, gt_ref, o_ref, m_ref, l_ref, acc_ref,
                     *, past, n_win):
    c = pl.program_id(1)
    tq = q_ref.shape[0]
    n_spad = sel_ref.shape[1] // NSA_KV_HEADS
    n = kv_ref.shape[2]

    @pl.when(c == 0)
    def _():
        _init_state(m_ref, l_ref, acc_ref)

    def attend(keys_of, vals_of, bias_of, krel, branch):
        for k in range(NSA_KV_HEADS):
            s = lax.dot_general(_stack_heads(q_ref, k), keys_of(k), NT, preferred_element_type=F32)
            _online_update(s, bias_of(k), krel, _NSA_SLOPES[k * NSA_GROUP:(k + 1) * NSA_GROUP], vals_of(k),
                           m_ref, l_ref, acc_ref, (branch * NSA_KV_HEADS + k) * NSA_GROUP * tq, tq)

    def slab(ref, j):
        return ref[:, j * HEAD_DIM:(j + 1) * HEAD_DIM]

    first_key = c * n
    expand = _block_expand(n_spad, n, first_key)
    attend(lambda k: kv_ref[0, k], lambda k: kv_ref[0, NSA_KV_HEADS + k],
           lambda k: jnp.dot(sel_ref[:, k * n_spad:(k + 1) * n_spad].astype(BF16), expand,
                             preferred_element_type=F32),
           _key_offsets(n, first_key - past), 0)

    @pl.when(c == pl.num_programs(1) - 1)
    def _():
        npad = V7X_LANES
        real = lax.broadcasted_iota(jnp.int32, (tq, npad), 1) < tq
        causal = _mask_bias(jnp.logical_and(_distance(tq, npad, 0) >= 0, real))
        lane = lax.broadcasted_iota(jnp.int32, (tq, n_spad), 1)

        def chosen_new(k):
            col = jnp.where(lane == past // SLC_BLOCK, sel_ref[:, k * n_spad:(k + 1) * n_spad], 0.0)
            return causal + jnp.sum(col, axis=-1, keepdims=True)

        knew = _pad_keys(knew_ref[...], npad).astype(BF16)
        attend(lambda k: slab(knew, k), lambda k: slab(knew, NSA_KV_HEADS + k), chosen_new, _key_offsets(npad, 0), 0)
        wbuf = wst_ref[0, 0].astype(BF16)
        in_window = _mask_bias(_distance(tq, n_win, n_win) <= WINDOW)
        attend(lambda k: slab(wbuf, k), lambda k: slab(wbuf, NSA_KV_HEADS + k), lambda k: in_window,
               _key_offsets(n_win, -n_win), 1)
        wnew = _pad_keys(wnew_ref[...], npad).astype(BF16)
        attend(lambda k: slab(wnew, k), lambda k: slab(wnew, NSA_KV_HEADS + k), lambda k: causal,
               _key_offsets(npad, 0), 1)
        _nsa_combine(gt_ref, oc_ref, o_ref, l_ref, acc_ref, tq)


def _nsa_sample(u, grp, o_cmp, sel, slc_kv, win_state, layer):
    t = grp.t
    n_win = win_state.shape[2]
    assert n_win == WINDOW and grp.past % SLC_BLOCK == 0 and t <= SLC_BLOCK and grp.row0 % t == 0
    n = _pick_tile(grp.past, (2048, 1024, 512, 256, 128))
    kvw = 2 * NSA_KV_W
    rb = grp.row0 // t
    tok = lambda width, col: pl.BlockSpec((t, width), lambda b, c: (rb + b, col))
    own = lambda width: pl.BlockSpec((t, width), lambda b, c: (b, 0))
    return pl.pallas_call(
        functools.partial(_nsa_sample_body, past=grp.past, n_win=n_win),
        grid=(grp.b, grp.past // n),
        in_specs=[
            tok(NSA_W, 0),
            pl.BlockSpec((1, 4, n, HEAD_DIM), lambda b, c: (b, 0, c, 0)),
            tok(kvw, (C_ROWS + kvw) // kvw),
            pl.BlockSpec((1, 1, n_win, kvw), lambda b, c: (layer, b, 0, 0)),
            tok(kvw, C_WIN // kvw),
            own(sel.shape[1]),
            own(NSA_W),
            tok(U_TILE, C_SMALL // U_TILE),
        ],
        out_specs=own(NSA_W),
        out_shape=jax.ShapeDtypeStruct((grp.b * t, NSA_W), F32),
        scratch_shapes=_nsa_state(t),
        compiler_params=_cparams(("parallel", "arbitrary")),
        name="nsa_attn_cached",
    )(u, slc_kv, u, win_state, u, sel, o_cmp, u)


def _diff_update(q_ref, group_of, bias, krel, m_ref, l_ref, acc_ref):
    tq = q_ref.shape[0]
    low = lax.broadcasted_iota(jnp.int32, (tq, HEAD_DIM), 1) < DIFF_QK_DIM
    scores, shifts = [], []
    for h in range(DIFF_HEADS):
        qh = q_ref[:, h * HEAD_DIM:(h + 1) * HEAD_DIM]
        q2 = jnp.concatenate([jnp.where(low, qh, 0.0), jnp.where(low, 0.0, qh)], axis=0).astype(BF16)
        scores.append(lax.dot_general(q2, group_of(h), NT, preferred_element_type=F32))
        shift = (_DIFF_SLOPES[h] * LOG2E) * krel
        shift = jnp.broadcast_to(shift, (tq, shift.shape[1])) if bias is None else bias + shift
        shifts += [shift, shift]
    p, alpha = _online_softmax(jnp.concatenate(scores, axis=0), jnp.concatenate(shifts, axis=0), m_ref, l_ref,
                               slice(0, 2 * DIFF_HEADS * tq))
    for h in range(DIFF_HEADS):
        rows = slice(2 * h * tq, 2 * (h + 1) * tq)
        acc_ref[rows] = alpha[rows] * acc_ref[rows] + jnp.dot(p[rows], group_of(DIFF_HEADS + h),
                                                               preferred_element_type=F32)


def _diff_finish(lam_ref, g_ref, o_ref, l_ref, acc_ref, post):
    tq = o_ref.shape[0]
    lam = lam_ref[0]
    for h in range(DIFF_HEADS):
        o1 = _normalised(l_ref, acc_ref, slice(2 * h * tq, (2 * h + 1) * tq))
        o2 = _normalised(l_ref, acc_ref, slice((2 * h + 1) * tq, (2 * h + 2) * tq))
        o_ref[:, h * HEAD_DIM:(h + 1) * HEAD_DIM] = _rms_rows(o1 - lam * o2, g_ref[...]) * post


def _diff_prompt_body(lam_ref, q_ref, kv_ref, g_ref, o_ref, m_ref, l_ref, acc_ref, *, tq, tk, post):
    qi = pl.program_id(1)
    kj = pl.program_id(2)

    @pl.when(kj == 0)
    def _():
        _init_state(m_ref, l_ref, acc_ref)

    @pl.when(kj * tk <= qi * tq + (tq - 1))
    def _():
        kv = kv_ref[...].astype(BF16)
        _diff_update(q_ref, lambda j: kv[:, j * HEAD_DIM:(j + 1) * HEAD_DIM],
                     _mask_bias(_distance(tq, tk, qi * tq - kj * tk) >= 0), _key_offsets(tk, kj * tk - qi * tq),
                     m_ref, l_ref, acc_ref)

    @pl.when(kj == pl.num_programs(2) - 1)
    def _():
        _diff_finish(lam_ref, g_ref, o_ref, l_ref, acc_ref, post)


def _diff_prompt(u, grp, lam, gain, post):
    assert grp.past == 0 and grp.row0 == 0
    t = grp.t
    tq = _pick_tile(t, (512, 256, 128, 64, 32, 16, 8))
    tk = _pick_tile(t, (512, 256, 128))
    nq, nk = t // tq, t // tk
    return pl.pallas_call(
        functools.partial(_diff_prompt_body, tq=tq, tk=tk, post=post),
        grid=(grp.b, nq, nk),
        in_specs=[
            pl.BlockSpec(memory_space=pltpu.SMEM),
            pl.BlockSpec((tq, DIFF_W), lambda b, i, j: (b * nq + i, C_DQ // DIFF_W)),
            pl.BlockSpec((tk, 2 * DIFF_W),
                         lambda b, i, j: (b * nk + jnp.minimum(j, (i * tq + (tq - 1)) // tk), C_DROWS // (2 * DIFF_W))),
            pl.BlockSpec((1, HEAD_DIM), lambda b, i, j: (0, 0)),
        ],
        out_specs=pl.BlockSpec((tq, DIFF_W), lambda b, i, j: (b * nq + i, 0)),
        out_shape=jax.ShapeDtypeStruct((grp.b * t, DIFF_W), F32),
        scratch_shapes=_diff_state(tq),
        compiler_params=_cparams(("parallel", "parallel", "arbitrary")),
        name="diff_attn_prompt",
    )(lam, u, u, gain)


def _diff_sample_body(pt_ref, lam_ref, q_ref, *rest, pps, past, post):
    pages = rest[:pps]
    new_ref, g_ref, o_ref, m_ref, l_ref, acc_ref = rest[pps:]
    del pt_ref
    c = pl.program_id(1)
    tq = q_ref.shape[0]
    n = pps * PAGE_SIZE

    @pl.when(c == 0)
    def _():
        _init_state(m_ref, l_ref, acc_ref)

    _diff_update(q_ref, lambda j: _page_piece(pages, j), None, _key_offsets(n, c * n - past), m_ref, l_ref, acc_ref)

    @pl.when(c == pl.num_programs(1) - 1)
    def _():
        npad = V7X_LANES
        keep = jnp.logical_and(_distance(tq, npad, 0) >= 0, lax.broadcasted_iota(jnp.int32, (tq, npad), 1) < tq)
        new = _pad_keys(new_ref[...], npad).astype(BF16)
        _diff_update(q_ref, lambda j: new[:, j * HEAD_DIM:(j + 1) * HEAD_DIM], _mask_bias(keep),
                     _key_offsets(npad, 0), m_ref, l_ref, acc_ref)
        _diff_finish(lam_ref, g_ref, o_ref, l_ref, acc_ref, post)


def _diff_sample(u, grp, lam, gain, post, cache, page_table, layer):
    t = grp.t
    assert grp.past % PAGE_SIZE == 0 and grp.row0 % t == 0
    n_pages = grp.past // PAGE_SIZE
    pps = _pages_per_step(n_pages)
    rb = grp.row0 // t
    return pl.pallas_call(
        functools.partial(_diff_sample_body, pps=pps, past=grp.past, post=post),
        grid_spec=pltpu.PrefetchScalarGridSpec(
            num_scalar_prefetch=1,
            grid=(grp.b, n_pages // pps),
            in_specs=[
                pl.BlockSpec(memory_space=pltpu.SMEM),
                pl.BlockSpec((t, DIFF_W), lambda b, c, pt: (rb + b, C_DQ // DIFF_W)),
            ] + _page_specs(layer, n_pages, pps) + [
                pl.BlockSpec((t, 2 * DIFF_W), lambda b, c, pt: (rb + b, C_DROWS // (2 * DIFF_W))),
                pl.BlockSpec((1, HEAD_DIM), lambda b, c, pt: (0, 0)),
            ],
            out_specs=pl.BlockSpec((t, DIFF_W), lambda b, c, pt: (b, 0)),
            scratch_shapes=_diff_state(t),
        ),
        out_shape=jax.ShapeDtypeStruct((grp.b * t, DIFF_W), F32),
        compiler_params=_cparams(("parallel", "arbitrary")),
        name="diff_attn_paged",
    )(page_table, lam, u, *([cache] * pps), u, gain)


def _mlstm_body(xq_ref, xk_ref, v_ref, og_ref, bq_ref, bk_ref, wq_ref, wk_ref, cq_ref, ck_ref, li_ref, lf_ref,
                c0_ref, n0_ref, m0_ref, g_ref, h_ref, c_ref, n_ref, m_ref, q_s, k_s, *, lc, nc):
    t = xq_ref.shape[0]
    row = lax.broadcasted_iota(jnp.int32, (lc, lc), 0)
    col = lax.broadcasted_iota(jnp.int32, (lc, lc), 1)
    tril = row >= col
    triu = row <= col
    eye = row == col
    gain = g_ref[...]

    def conv(x_ref, buf_ref, w_ref, b_ref):
        ext = jnp.concatenate([buf_ref[0], x_ref[...]], axis=0)
        y = b_ref[...]
        for i in range(CONV_WIDTH):
            shifted = ext if i == CONV_WIDTH - 1 else pltpu.roll(ext, CONV_WIDTH - 1 - i, 0)
            y = y + shifted[V7X_SUBLANES:V7X_SUBLANES + t] * w_ref[i:i + 1, :]
        return y * jax.nn.sigmoid(y)

    q_s[...] = conv(xq_ref, bq_ref, wq_ref, cq_ref) * MLSTM_DIM ** -0.5
    k_s[...] = conv(xk_ref, bk_ref, wk_ref, ck_ref)

    def to_col(r):
        return jnp.sum(jnp.where(eye, jnp.broadcast_to(r, (lc, lc)), 0.0), axis=1, keepdims=True)

    def step(ci, carry):
        c, n, m = carry
        rows = pl.ds(ci * lc if nc == 1 else pl.multiple_of(ci * lc, lc), lc)
        qc = q_s[rows, :]
        kc = k_s[rows, :]
        vc = v_ref[rows, :]
        li = li_ref[0, pl.ds(ci, 1), :]
        lf = lf_ref[0, pl.ds(ci, 1), :]
        lf_b = jnp.broadcast_to(lf, (lc, lc))
        bcum_c = jnp.sum(jnp.where(tril, lf_b, 0.0), axis=1, keepdims=True)
        bcum_r = jnp.sum(jnp.where(triu, jnp.broadcast_to(to_col(lf), (lc, lc)), 0.0), axis=0, keepdims=True)
        dmat = jnp.where(tril, bcum_c - bcum_r + li, NEG_INF)
        inter = bcum_c + m
        mj = jnp.maximum(inter, jnp.max(dmat, axis=1, keepdims=True))
        wts = jnp.exp(dmat - mj)
        a = jnp.exp(inter - mj)
        sqk = lax.dot_general(qc, kc, NT, precision=HIGHEST, preferred_element_type=F32) * wts
        num = a * jnp.dot(qc, c, precision=HIGHEST, preferred_element_type=F32) + \
            jnp.dot(sqk, vc, precision=HIGHEST, preferred_element_type=F32)
        den = a * jnp.sum(qc * n, axis=1, keepdims=True) + jnp.sum(sqk, axis=1, keepdims=True)
        h = num / jnp.maximum(jnp.abs(den), jnp.exp(-mj))
        h_ref[rows, :] = _rms_rows(h, gain) * jax.nn.sigmoid(og_ref[rows, :])
        b_last = jnp.sum(lf, axis=1, keepdims=True)
        gl = b_last - bcum_r + li
        m_new = jnp.maximum(b_last + m, jnp.max(gl, axis=1, keepdims=True))
        decay = jnp.exp(b_last + m - m_new)
        kw = kc * to_col(jnp.exp(gl - m_new))
        c_new = decay * c + lax.dot_general(kw, vc, (((0,), (0,)), ((), ())), precision=HIGHEST,
                                            preferred_element_type=F32)
        n_new = decay * n + jnp.sum(kw, axis=0, keepdims=True)
        return c_new, n_new, m_new

    init = (c0_ref[0], n0_ref[0], m0_ref[0][:, 0:1])
    c, n, m = step(0, init) if nc == 1 else lax.fori_loop(0, nc, step, init, unroll=2)
    c_ref[0] = c
    n_ref[0] = n
    m_ref[0] = jnp.broadcast_to(m, (1, MLSTM_DIM))


def _mlstm(u, grp, lp, c0, n0, m0, conv_buf):
    b, t = grp.b, grp.t
    assert grp.row0 % t == 0 and t % V7X_SUBLANES == 0
    bh = b * MLSTM_HEADS
    hd = MLSTM_DIM
    lc = MLSTM_CHUNK if t % MLSTM_CHUNK == 0 else t
    nc = t // lc
    rb = grp.row0 // t
    gates = u[grp.row0:grp.row0 + b * t, C_SMALL:C_SMALL + U_TILE].reshape(b, t, U_TILE)
    log_i = gates[:, :, G_MI:G_MI + MLSTM_HEADS] + lp['ml_gate_b'][0]
    log_f = jax.nn.log_sigmoid(gates[:, :, G_MF:G_MF + MLSTM_HEADS] + lp['ml_gate_b'][1])
    log_i = log_i.transpose(0, 2, 1).reshape(bh, nc, lc)
    log_f = log_f.transpose(0, 2, 1).reshape(bh, nc, lc)
    buf8 = jnp.pad(conv_buf, ((0, 0), (V7X_SUBLANES - (CONV_WIDTH - 1), 0), (0, 0)))
    w8 = jnp.pad(lp['ml_conv_w'], ((0, V7X_SUBLANES - CONV_WIDTH), (0, 0)))
    cb = lp['ml_conv_b'].reshape(1, 2 * MLSTM_W)
    tok = lambda col0: pl.BlockSpec((t, hd), lambda i, h: (rb + i, col0 // hd + h))
    buf = lambda off: pl.BlockSpec((1, V7X_SUBLANES, hd), lambda i, h: (i, 0, off + h))
    wsp = lambda off: pl.BlockSpec((V7X_SUBLANES, hd), lambda i, h: (0, off + h))
    bsp = lambda off: pl.BlockSpec((1, hd), lambda i, h: (0, off + h))
    gate = pl.BlockSpec((1, nc, lc), lambda i, h: (i * MLSTM_HEADS + h, 0, 0))
    vec = pl.BlockSpec((1, 1, hd), lambda i, h: (i * MLSTM_HEADS + h, 0, 0))
    mat = pl.BlockSpec((1, hd, hd), lambda i, h: (i * MLSTM_HEADS + h, 0, 0))
    h, c, n, m = pl.pallas_call(
        functools.partial(_mlstm_body, lc=lc, nc=nc),
        grid=(b, MLSTM_HEADS),
        in_specs=[tok(C_MQ), tok(C_MK), tok(C_MV), tok(C_MO), buf(0), buf(MLSTM_HEADS), wsp(0), wsp(MLSTM_HEADS),
                  bsp(0), bsp(MLSTM_HEADS), gate, gate, mat, vec, vec, pl.BlockSpec((1, hd), lambda i, h: (0, 0))],
        out_specs=[pl.BlockSpec((t, hd), lambda i, h: (i, h)), mat, vec, vec],
        out_shape=[
            jax.ShapeDtypeStruct((b * t, MLSTM_W), F32),
            jax.ShapeDtypeStruct((bh, hd, hd), F32),
            jax.ShapeDtypeStruct((bh, 1, hd), F32),
            jax.ShapeDtypeStruct((bh, 1, hd), F32),
        ],
        scratch_shapes=[pltpu.VMEM((t, hd), F32), pltpu.VMEM((t, hd), F32)],
        compiler_params=_cparams(("parallel", "parallel")),
        name="mlstm",
    )(u, u, u, u, buf8, buf8, w8, w8, cb, cb, log_i, log_f, c0.reshape(bh, hd, hd), n0.reshape(bh, 1, hd),
      jnp.broadcast_to(m0.reshape(bh, 1, 1), (bh, 1, hd)), lp['ml_out_norm'].reshape(1, hd))
    return (h, c.reshape(b, MLSTM_HEADS, hd, hd), n.reshape(b, MLSTM_HEADS, hd), m[:, 0, 0].reshape(b, MLSTM_HEADS))


def _cache_rows_body(nsa_ref, diff_ref, *rest):
    nsa_out, diff_out = rest[-2:]
    tm = nsa_ref.shape[0]
    for j in range(ROW_GROUPS):
        cols = slice(j * HEAD_DIM, (j + 1) * HEAD_DIM)
        nsa_out[pl.ds(j, tm, stride=ROW_GROUPS), :] = nsa_ref[:, cols]
        diff_out[pl.ds(j, tm, stride=ROW_GROUPS), :] = diff_ref[:, cols]


def _cache_rows(u, layer, depth, prev):
    t = u.shape[0]
    tm = _pick_tile(t, (256, 128, 64, 32, 16, 8))
    width = ROW_GROUPS * HEAD_DIM
    in_specs = [pl.BlockSpec((tm, width), lambda i: (i, C_ROWS // width)),
                pl.BlockSpec((tm, width), lambda i: (i, C_DROWS // width))]
    args = [u, u]
    aliases = {}
    if prev is not None:
        in_specs += [pl.BlockSpec(memory_space=pl.ANY)] * 2
        args += list(prev)
        aliases = {2: 0, 3: 1}
    out = pl.BlockSpec((None, tm * ROW_GROUPS, HEAD_DIM), lambda i: (layer, i, 0))
    shape = jax.ShapeDtypeStruct((depth, t * ROW_GROUPS, HEAD_DIM), F32)
    return pl.pallas_call(
        _cache_rows_body,
        grid=(t // tm,),
        in_specs=in_specs,
        out_specs=[out, out],
        out_shape=[shape, shape],
        input_output_aliases=aliases,
        compiler_params=_cparams(("parallel",)),
        name="cache_rows",
    )(*args)


def _pack_w_in(w_in):
    cuts = np.cumsum((0,) + IN_SPLITS)
    cols = [w_in[..., cuts[i]:cuts[i + 1]] for i in range(len(IN_SPLITS))]
    packed = jnp.concatenate([cols[i] for i in _MAIN_ORDER] + [cols[i] for i in _SMALL_ORDER], axis=-1)
    pad = [(0, 0)] * (packed.ndim - 1) + [(0, U_COLS - packed.shape[-1])]
    return jnp.pad(packed, pad).astype(BF16)


def _u_gain(lp):
    ones = lambda n: jnp.ones((n,), F32)
    g = jnp.concatenate([
        jnp.tile(lp['nsa_q_norm'], NSA_HEADS) * (HEAD_DIM ** -0.5 * LOG2E),
        ones(2 * NSA_KV_W), jnp.tile(lp['nsa_k_norm'][1], NSA_KV_HEADS), ones(NSA_KV_W),
        jnp.tile(lp['nsa_k_norm'][2], NSA_KV_HEADS), ones(NSA_KV_W),
        jnp.tile(lp['diff_q_norm'], 2 * DIFF_HEADS) * (DIFF_QK_DIM ** -0.5 * LOG2E),
        jnp.tile(lp['diff_k_norm'], 2 * DIFF_HEADS),
        ones(U_COLS - (C_DROWS + DIFF_W))])
    return g.reshape(1, U_COLS)


def _group_rows(u, grp, col0, width):
    return u[grp.row0:grp.row0 + grp.b * grp.t, col0:col0 + width].reshape(grp.b, grp.t, width)


def _tail_rows(buf, new, keep):
    n_new = min(keep, new.shape[1])
    parts = [buf[:, buf.shape[1] - (keep - n_new):], new[:, new.shape[1] - n_new:]] if keep > n_new else \
        [new[:, new.shape[1] - n_new:]]
    return jnp.concatenate(parts, axis=1)


def _trunk(x_prompt, x_sample, cache_nsa, cache_diff, state_win, state_ml_c, state_ml_n, state_ml_m, state_ml_conv,
           page_table, p):
    bp, tp, d = x_prompt.shape
    bs, ts, _ = x_sample.shape
    depth = p['w_in'].shape[0]
    past = page_table.shape[1] * PAGE_SIZE
    gp = _Group(bp, tp, 0, 0)
    gs = _Group(bs, ts, bp * tp, past)
    n_p = bp * tp
    w_in = _pack_w_in(p['w_in'])
    w_out = p['w_out'].astype(BF16)
    w13 = (p['ffn1_w13'].astype(BF16), p['ffn2_w13'].astype(BF16))
    w2 = (p['ffn1_w2'].astype(BF16), p['ffn2_w2'].astype(BF16))
    per_layer = ('mix_norm', 'nsa_q_norm', 'nsa_k_norm', 'nsa_cmp_pos', 'nsa_cmp_w1', 'nsa_cmp_w2', 'diff_q_norm',
                 'diff_k_norm', 'diff_lambda', 'diff_out_norm', 'ml_conv_w', 'ml_conv_b', 'ml_gate_b', 'ml_out_norm')
    cache_nsa = _page_view(cache_nsa)
    cache_diff = _page_view(cache_diff)
    win_state = state_win.reshape(state_win.shape[:3] + (2 * NSA_KV_W,))
    pt = page_table.reshape(-1)
    zeros = lambda *s: jnp.zeros(s, F32)
    p_state = (zeros(bp, MLSTM_HEADS, MLSTM_DIM, MLSTM_DIM), zeros(bp, MLSTM_HEADS, MLSTM_DIM),
               zeros(bp, MLSTM_HEADS), zeros(bp, CONV_WIDTH - 1, 2 * MLSTM_W))
    p_win0 = zeros(bp, 0, 2, NSA_KV_HEADS, HEAD_DIM)
    row_arrays = None

    x = jnp.concatenate([x_prompt.reshape(n_p, d), x_sample.reshape(bs * ts, d)], axis=0)
    outs = []
    for l in range(depth):
        lp = {name: p[name][l] for name in per_layer}
        lam_init = 0.8 - 0.6 * math.exp(-0.3 * l)
        lam_p = lp['diff_lambda']
        lam = (jnp.exp(jnp.sum(lam_p[0] * lam_p[1])) - jnp.exp(jnp.sum(lam_p[2] * lam_p[3])) + lam_init).reshape(1)
        dgain = lp['diff_out_norm'].reshape(1, HEAD_DIM)
        h = _ffn(x, p['ffn1_norm'][l], w13[0], w2[0], l)
        u = _inproj(h, lp['mix_norm'], w_in, _u_gain(lp), l)

        summ_p, n_cmp_p = _compress_prompt(u, gp, lp)
        oc_p, sel_p = _cmp_select(u, gp, summ_p, n_cmp_p)
        nsa_p = _nsa_prompt(u, gp, oc_p, sel_p)
        diff_p = _diff_prompt(u, gp, lam, dgain, 1.0 - lam_init)
        ml_p, c_p, nn_p, m_p = _mlstm(u, gp, lp, *p_state)

        summ_s, n_cmp_s, slc_kv = _compress_sample(u, gs, lp, cache_nsa, pt, l)
        oc_s, sel_s = _cmp_select(u, gs, summ_s, n_cmp_s)
        nsa_s = _nsa_sample(u, gs, oc_s, sel_s, slc_kv, win_state, l)
        diff_s = _diff_sample(u, gs, lam, dgain, 1.0 - lam_init, cache_diff, pt, l)
        ml_s, c_s, nn_s, m_s = _mlstm(u, gs, lp, state_ml_c[l], state_ml_n[l], state_ml_m[l], state_ml_conv[l])

        h = _outproj(h, jnp.concatenate([nsa_p, nsa_s]), jnp.concatenate([diff_p, diff_s]),
                     jnp.concatenate([ml_p, ml_s]), w_out, l)
        x = _ffn(h, p['ffn2_norm'][l], w13[1], w2[1], l)

        row_arrays = _cache_rows(u, l, depth, row_arrays)
        win_shape = (2, NSA_KV_HEADS, HEAD_DIM)
        keep_p, keep_s = min(WINDOW, tp), min(WINDOW, state_win.shape[2] + ts)
        win_new_p = _group_rows(u, gp, C_WIN, 2 * NSA_KV_W)[:, tp - min(keep_p, tp):]
        win_new_s = _group_rows(u, gs, C_WIN, 2 * NSA_KV_W)[:, ts - min(keep_s, ts):]
        win_rows_p = _tail_rows(p_win0, win_new_p.reshape(win_new_p.shape[:2] + win_shape), keep_p)
        win_rows_s = _tail_rows(state_win[l], win_new_s.reshape(win_new_s.shape[:2] + win_shape), keep_s)
        conv_p = _tail_rows(p_state[3], _group_rows(u, gp, C_MQ, 2 * MLSTM_W), CONV_WIDTH - 1)
        conv_s = _tail_rows(state_ml_conv[l], _group_rows(u, gs, C_MQ, 2 * MLSTM_W), CONV_WIDTH - 1)
        outs.append((win_rows_p, win_rows_s, c_p, c_s, nn_p, nn_s, m_p, m_s, conv_p, conv_s))
    stacked = [jnp.stack(z) for z in zip(*outs)]
    nsa_rows, diff_rows = row_arrays
    n_rows = n_p * ROW_GROUPS

    def split(a, shape):
        return (a[:, :n_rows].reshape((depth, bp, tp) + shape), a[:, n_rows:].reshape((depth, bs, ts) + shape))

    return ((x[:n_p].reshape(bp, tp, d), x[n_p:].reshape(bs, ts, d)) + split(nsa_rows, (4, NSA_KV_HEADS, HEAD_DIM))
            + split(diff_rows, (2, DIFF_HEADS, HEAD_DIM)) + tuple(stacked))


def kernel(x_prompt, x_sample, cache_nsa, cache_diff, state_win, state_ml_c, state_ml_n, state_ml_m, state_ml_conv,
           page_table, ffn1_norm, ffn1_w13, ffn1_w2, mix_norm, w_in, nsa_q_norm, nsa_k_norm, nsa_cmp_pos, nsa_cmp_w1,
           nsa_cmp_w2, diff_q_norm, diff_k_norm, diff_lambda, diff_out_norm, ml_conv_w, ml_conv_b, ml_gate_b,
           ml_out_norm, w_out, ffn2_norm, ffn2_w13, ffn2_w2):
    params = dict(ffn1_norm=ffn1_norm, ffn1_w13=ffn1_w13, ffn1_w2=ffn1_w2, mix_norm=mix_norm, w_in=w_in,
                  nsa_q_norm=nsa_q_norm, nsa_k_norm=nsa_k_norm, nsa_cmp_pos=nsa_cmp_pos, nsa_cmp_w1=nsa_cmp_w1,
                  nsa_cmp_w2=nsa_cmp_w2, diff_q_norm=diff_q_norm, diff_k_norm=diff_k_norm, diff_lambda=diff_lambda,
                  diff_out_norm=diff_out_norm, ml_conv_w=ml_conv_w, ml_conv_b=ml_conv_b, ml_gate_b=ml_gate_b,
                  ml_out_norm=ml_out_norm, w_out=w_out, ffn2_norm=ffn2_norm, ffn2_w13=ffn2_w13, ffn2_w2=ffn2_w2)
    return _trunk(x_prompt, x_sample, cache_nsa, cache_diff, state_win, state_ml_c, state_ml_n, state_ml_m,
                  state_ml_conv, page_table, params)
```

```python
import functools
import math
from typing import NamedTuple

import numpy as np
import jax
import jax.numpy as jnp
from jax import lax
from jax.experimental import pallas as pl
from jax.experimental.pallas import tpu as pltpu

F32 = jnp.float32
BF16 = jnp.bfloat16
HIGHEST = lax.Precision.HIGHEST
NT = (((1,), (1,)), ((), ()))

D_MODEL = 2048
D_FF = 5632
PAGE_SIZE = 128
HEAD_DIM = 128
ROW_GROUPS = 8
NSA_HEADS = 8
NSA_KV_HEADS = 2
NSA_GROUP = NSA_HEADS // NSA_KV_HEADS
CMP_BLOCK = 32
CMP_STRIDE = 16
SLC_BLOCK = 64
SLC_SHIFT = 6
SLC_TOPN = 16
WINDOW = 512
DIFF_HEADS = 4
DIFF_QK_DIM = HEAD_DIM // 2
MLSTM_HEADS = 4
MLSTM_DIM = 128
MLSTM_CHUNK = 64
CONV_WIDTH = 4
NORM_EPS = 1e-6
NEG_INF = -1e30
M_INIT = -1e29
LOG2E = 1.4426950408889634
FORCE_BONUS = 1e3

NSA_W = NSA_HEADS * HEAD_DIM
NSA_KV_W = NSA_KV_HEADS * HEAD_DIM
DIFF_W = DIFF_HEADS * HEAD_DIM
MLSTM_W = MLSTM_HEADS * MLSTM_DIM
IN_SPLITS = (NSA_W, 6 * NSA_KV_W, 3 * NSA_HEADS, DIFF_W, DIFF_W, DIFF_W,
             MLSTM_W, MLSTM_W, MLSTM_W, MLSTM_HEADS, MLSTM_HEADS, MLSTM_W)
IN_COLS = sum(IN_SPLITS)

V7X_LANES = 128
V7X_SUBLANES = 8
V7X_VMEM_LIMIT_BYTES = 56 * 1024 * 1024

U_TILE = 256
C_NQ = 0
C_ROWS = C_NQ + NSA_W
C_WIN = C_ROWS + 4 * NSA_KV_W
C_DQ = C_WIN + 2 * NSA_KV_W
C_DROWS = C_DQ + DIFF_W
C_MQ = C_DROWS + 2 * DIFF_W
C_MK = C_MQ + MLSTM_W
C_MV = C_MK + MLSTM_W
C_MO = C_MV + MLSTM_W
C_SMALL = C_MO + MLSTM_W
U_COLS = C_SMALL + U_TILE
U_STEP = 5 * U_TILE
G_NSA, G_MI, G_MF = 0, 3 * NSA_HEADS, 3 * NSA_HEADS + MLSTM_HEADS
U_KINDS = (1, 1, 1, 1, 0, 0, 1, 0, 1, 0, 2, 2, 2, 2, 0, 0, 0, 0, 0, 0, 0, 0, 0, 0, 0)
_MAIN_ORDER = (0, 1, 3, 4, 5, 6, 7, 8, 11)
_SMALL_ORDER = (2, 9, 10)


class _Group(NamedTuple):
    b: int
    t: int
    row0: int
    past: int


def _round_up(n, m):
    return -(-n // m) * m


def _cparams(semantics):
    return pltpu.CompilerParams(dimension_semantics=semantics, vmem_limit_bytes=V7X_VMEM_LIMIT_BYTES)


def _pick_tile(n, prefs):
    for p in prefs:
        if n % p == 0:
            return p
    return n


def _alibi(n):
    return [float(2.0 ** (-8.0 * i / n)) for i in range(1, n + 1)]


def _rms_rows(x, g):
    inv = lax.rsqrt(jnp.mean(x * x, axis=-1, keepdims=True) + NORM_EPS)
    return x * inv * g


def _ffn_body(x_ref, g_ref, wa_ref, wb_ref, w2_ref, o_ref, n_ref):
    @pl.when(pl.program_id(1) == 0)
    def _():
        x = x_ref[...]
        n_ref[...] = _rms_rows(x, g_ref[...]).astype(BF16)
        o_ref[...] = x

    n = n_ref[...]
    a = jnp.dot(n, wa_ref[...], preferred_element_type=F32)
    b = jnp.dot(n, wb_ref[...], preferred_element_type=F32)
    act = (0.5 * a * jax.nn.sigmoid(a) * b).astype(BF16)
    o_ref[...] += jnp.dot(act, w2_ref[...], preferred_element_type=F32)


def _ffn(x, g, w13, w2, layer):
    t, d = x.shape
    f = w2.shape[1]
    tm = _pick_tile(t, (768, 512, 256, 128, 64, 32, 16, 8))
    tf = _pick_tile(f, (512, 256, 128))
    nf = f // tf
    return pl.pallas_call(
        _ffn_body,
        grid=(t // tm, nf),
        in_specs=[
            pl.BlockSpec((tm, d), lambda i, j: (i, 0)),
            pl.BlockSpec((1, d), lambda i, j: (0, 0)),
            pl.BlockSpec((None, d, tf), lambda i, j: (layer, 0, j)),
            pl.BlockSpec((None, d, tf), lambda i, j: (layer, 0, j + nf)),
            pl.BlockSpec((None, tf, d), lambda i, j: (layer, j, 0)),
        ],
        out_specs=pl.BlockSpec((tm, d), lambda i, j: (i, 0)),
        out_shape=jax.ShapeDtypeStruct((t, d), F32),
        scratch_shapes=[pltpu.VMEM((tm, d), BF16)],
        compiler_params=_cparams(("parallel", "arbitrary")),
        name="ffn",
    )(x, g.reshape(1, d), w13, w13, w2)


def _inproj_body(kind_ref, x_ref, g_ref, w_ref, gain_ref, o_ref, n_ref):
    j = pl.program_id(1)

    @pl.when(j == 0)
    def _():
        n_ref[...] = _rms_rows(x_ref[...], g_ref[...]).astype(BF16)

    y = jnp.dot(n_ref[...], w_ref[...], preferred_element_type=F32)
    gain = gain_ref[...]
    for i in range(U_STEP // U_TILE):
        kind = kind_ref[j * (U_STEP // U_TILE) + i]
        halves = [slice(i * U_TILE + h * HEAD_DIM, i * U_TILE + (h + 1) * HEAD_DIM)
                  for h in range(U_TILE // HEAD_DIM)]

        @pl.when(kind == 0)
        def _(halves=halves):
            for cols in halves:
                o_ref[:, cols] = y[:, cols]

        @pl.when(kind == 1)
        def _(halves=halves):
            for cols in halves:
                o_ref[:, cols] = _rms_rows(y[:, cols], gain[:, cols])

        @pl.when(kind == 2)
        def _(halves=halves):
            low = lax.broadcasted_iota(jnp.int32, (y.shape[0], HEAD_DIM), 1) < DIFF_QK_DIM
            for cols in halves:
                yh = y[:, cols]
                sq = yh * yh
                s_lo = jnp.sum(jnp.where(low, sq, 0.0), axis=-1, keepdims=True)
                s_hi = jnp.sum(jnp.where(low, 0.0, sq), axis=-1, keepdims=True)
                inv = lax.rsqrt(jnp.where(low, s_lo, s_hi) * (1.0 / DIFF_QK_DIM) + NORM_EPS)
                o_ref[:, cols] = yh * inv * gain[:, cols]


def _inproj(x, g, w, gain, layer):
    t, d = x.shape
    tm = _pick_tile(t, (768, 512, 256, 128, 64, 32, 16, 8))
    return pl.pallas_call(
        _inproj_body,
        grid=(t // tm, U_COLS // U_STEP),
        in_specs=[
            pl.BlockSpec(memory_space=pltpu.SMEM),
            pl.BlockSpec((tm, d), lambda i, j: (i, 0)),
            pl.BlockSpec((1, d), lambda i, j: (0, 0)),
            pl.BlockSpec((None, d, U_STEP), lambda i, j: (layer, 0, j)),
            pl.BlockSpec((1, U_STEP), lambda i, j: (0, j)),
        ],
        out_specs=pl.BlockSpec((tm, U_STEP), lambda i, j: (i, j)),
        out_shape=jax.ShapeDtypeStruct((t, U_COLS), F32),
        scratch_shapes=[pltpu.VMEM((tm, d), BF16)],
        compiler_params=_cparams(("parallel", "arbitrary")),
        name="inproj",
    )(jnp.asarray(U_KINDS, jnp.int32), x, g.reshape(1, d), w, gain)


def _outproj_body(h_ref, a_ref, b_ref, c_ref, w_ref, y_ref):
    ka, kb = a_ref.shape[1], b_ref.shape[1]
    y = h_ref[...] + jnp.dot(a_ref[...].astype(BF16), w_ref[0:ka, :], preferred_element_type=F32)
    y = y + jnp.dot(b_ref[...].astype(BF16), w_ref[ka:ka + kb, :], preferred_element_type=F32)
    y_ref[...] = y + jnp.dot(c_ref[...].astype(BF16), w_ref[ka + kb:, :], preferred_element_type=F32)


def _outproj(h, o_nsa, o_diff, o_ml, w, layer):
    t, d = h.shape
    tm = _pick_tile(t, (768, 512, 256, 128, 64, 32, 16, 8))
    row = lambda a: pl.BlockSpec((tm, a.shape[1]), lambda i: (i, 0))
    return pl.pallas_call(
        _outproj_body,
        grid=(t // tm,),
        in_specs=[row(h), row(o_nsa), row(o_diff), row(o_ml),
                  pl.BlockSpec((None,) + w.shape[1:], lambda i: (layer, 0, 0))],
        out_specs=row(h),
        out_shape=jax.ShapeDtypeStruct((t, d), F32),
        compiler_params=_cparams(("parallel",)),
        name="outproj",
    )(h, o_nsa, o_diff, o_ml, w)


def _online_update(s, bias, krel, slopes, v, m_ref, l_ref, acc_ref, base, tq):
    def shift_of(slope):
        shift = (slope * LOG2E) * krel
        return jnp.broadcast_to(shift, (tq, shift.shape[1])) if bias is None else bias + shift

    rows = slice(base, base + len(slopes) * tq)
    p, alpha = _online_softmax(s, jnp.concatenate([shift_of(sl) for sl in slopes], axis=0), m_ref, l_ref, rows)
    acc_ref[rows] = alpha * acc_ref[rows] + jnp.dot(p, v, preferred_element_type=F32)


def _online_softmax(s, shift, m_ref, l_ref, rows):
    sg = s + shift
    m_old = m_ref[rows]
    m_new = jnp.maximum(m_old, jnp.max(sg, axis=-1, keepdims=True))
    p = jnp.exp2(sg - jnp.concatenate([m_new] * (sg.shape[1] // V7X_LANES), axis=1))
    alpha = jnp.exp2(m_old - m_new)
    l_ref[rows] = alpha * l_ref[rows] + jnp.sum(p, axis=-1, keepdims=True)
    m_ref[rows] = m_new
    return p.astype(BF16), alpha


def _init_state(m_ref, l_ref, acc_ref):
    m_ref[...] = jnp.full(m_ref.shape, M_INIT, F32)
    l_ref[...] = jnp.zeros(l_ref.shape, F32)
    acc_ref[...] = jnp.zeros(acc_ref.shape, F32)


def _mask_bias(keep):
    return jnp.where(keep, 0.0, NEG_INF)


def _key_offsets(n, first):
    return (first + lax.broadcasted_iota(jnp.int32, (1, n), 1)).astype(F32)


def _normalised(l_ref, acc_ref, rows):
    return acc_ref[rows] / jnp.maximum(l_ref[rows], 1e-30)


def _distance(tq, n, offset):
    return offset + lax.broadcasted_iota(jnp.int32, (tq, n), 0) - lax.broadcasted_iota(jnp.int32, (tq, n), 1)


def _block_expand(n_spad, n, first_key):
    blk = lax.broadcasted_iota(jnp.int32, (n_spad, n), 0)
    key = first_key + lax.broadcasted_iota(jnp.int32, (n_spad, n), 1)
    return jnp.where(lax.shift_right_logical(key, SLC_SHIFT) == blk, 1.0, 0.0).astype(BF16)


def _stack_heads(q_ref, k):
    return jnp.concatenate([q_ref[:, (k * NSA_GROUP + g) * HEAD_DIM:(k * NSA_GROUP + g + 1) * HEAD_DIM]
                            for g in range(NSA_GROUP)], axis=0).astype(BF16)


def _pad_keys(x, n):
    return jnp.concatenate([x, jnp.zeros((n - x.shape[0], x.shape[1]), x.dtype)], axis=0)


def _nsa_combine(gt_ref, oc_ref, o_ref, l_ref, acc_ref, tq):
    gt = jax.nn.sigmoid(gt_ref[:, 0:V7X_LANES])
    for h in range(NSA_HEADS):
        cols = slice(h * HEAD_DIM, (h + 1) * HEAD_DIM)
        o_slc = _normalised(l_ref, acc_ref, slice(h * tq, (h + 1) * tq))
        o_win = _normalised(l_ref, acc_ref, slice((NSA_HEADS + h) * tq, (NSA_HEADS + h + 1) * tq))
        c0 = G_NSA + 3 * h
        o_ref[:, cols] = gt[:, c0:c0 + 1] * oc_ref[:, cols] + gt[:, c0 + 1:c0 + 2] * o_slc + gt[:, c0 + 2:c0 + 3] * o_win


_NSA_SLOPES = _alibi(NSA_HEADS)
_DIFF_SLOPES = _alibi(DIFF_HEADS)


def _softmax_state(rows):
    return [pltpu.VMEM((rows, V7X_LANES), F32), pltpu.VMEM((rows, V7X_LANES), F32), pltpu.VMEM((rows, HEAD_DIM), F32)]


def _nsa_state(tq):
    return _softmax_state(2 * NSA_HEADS * tq)


def _diff_state(tq):
    return _softmax_state(2 * DIFF_HEADS * tq)


def _cmp_accumulate(rows_of, w_ref, o_ref):
    for c in range(4):
        for r in range(CMP_STRIDE):
            part = jnp.dot(rows_of(c, r).astype(BF16), w_ref[c // 2, r * HEAD_DIM:(r + 1) * HEAD_DIM, :],
                           preferred_element_type=F32)
            if r == 0:
                o_ref[0, c] = part
            else:
                o_ref[0, c] += part


def _cmp_partial_body(x0_ref, x1_ref, x2_ref, x3_ref, w_ref, o_ref):
    srcs = (x0_ref, x1_ref, x2_ref, x3_ref)
    per = x0_ref.shape[0] // CMP_STRIDE
    _cmp_accumulate(lambda c, r: srcs[c][pl.ds(r, per, stride=CMP_STRIDE), :], w_ref, o_ref)


def _cmp_partial_paged_body(pt_ref, *refs, pps):
    del pt_ref
    pages, w_ref, o_ref, kv_ref, slab_ref = refs[:pps], refs[pps], refs[pps + 1], refs[pps + 2], refs[pps + 3]
    for i, pg in enumerate(pages):
        for c in range(4):
            slab_ref[c, i * PAGE_SIZE:(i + 1) * PAGE_SIZE, :] = _page_rows(pg, c)
    per = pps * PAGE_SIZE // CMP_STRIDE
    _cmp_accumulate(lambda c, r: slab_ref[c, pl.ds(r, per, stride=CMP_STRIDE), :], w_ref, o_ref)
    for j in range(4):
        kv_ref[0, j] = _page_piece(pages, 4 + j)


def _cmp_finish_body(*refs, has_tail, n_out):
    if has_tail:
        p_ref, tail_ref, w1_ref, pos_ref, w2_ref, g_ref, o_ref = refs
    else:
        p_ref, w1_ref, pos_ref, w2_ref, g_ref, o_ref = refs
    c = pl.program_id(0)
    p = p_ref[0, 0]
    n = p.shape[0]
    w1 = w1_ref[0]
    pb = jnp.dot(pos_ref[0].astype(BF16), w1, preferred_element_type=F32)
    nxt = pltpu.roll(p[:, HEAD_DIM:], n - 1, 0)
    if has_tail:
        rid = lax.broadcasted_iota(jnp.int32, (V7X_SUBLANES, HEAD_DIM), 0)
        tail8 = jnp.zeros((V7X_SUBLANES, HEAD_DIM), F32)
        xt = tail_ref[...].astype(BF16)
        for r in range(tail_ref.shape[0]):
            d = jnp.dot(xt, w1[r * HEAD_DIM:(r + 1) * HEAD_DIM, HEAD_DIM:], preferred_element_type=F32)
            tail8 = tail8 + jnp.where(rid == r, d, 0.0)
        tail = jnp.sum(tail8, axis=0, keepdims=True)
    else:
        tail = jnp.zeros((1, HEAD_DIM), F32)
    last = lax.broadcasted_iota(jnp.int32, (n, HEAD_DIM), 0) == n - 1
    z = p[:, :HEAD_DIM] + pb[0:1, :HEAD_DIM] + jnp.where(last, tail, nxt) + pb[1:2, HEAD_DIM:]
    y = jnp.dot((z * jax.nn.sigmoid(z)).astype(BF16), w2_ref[0].astype(BF16), preferred_element_type=F32)
    y = jnp.where(c < NSA_KV_HEADS, _rms_rows(y, g_ref[...]), y)
    o_ref[0, 0, 0:n] = y
    if n_out > n:
        o_ref[0, 0, n:n_out] = jnp.zeros((n_out - n, HEAD_DIM), F32)


def _cmp_weights(lp):
    half = CMP_STRIDE * HEAD_DIM
    w1 = lp['nsa_cmp_w1']
    w1ab = jnp.concatenate([w1[:, :half], w1[:, half:]], axis=2).astype(BF16)
    pos = lp['nsa_cmp_pos']
    pos2 = jnp.concatenate([pos[:, :CMP_STRIDE].reshape(2, 1, half), pos[:, CMP_STRIDE:].reshape(2, 1, half),
                            jnp.zeros((2, V7X_SUBLANES - 2, half), F32)], axis=1)
    return w1ab, pos2


def _cmp_finish(p, tail_src, w1ab, pos2, lp, n_out, grp, u):
    b, _, n, _ = p.shape
    half = CMP_STRIDE * HEAD_DIM
    has_tail = tail_src is not None
    in_specs = [pl.BlockSpec((1, 1, n, 2 * HEAD_DIM), lambda c, i: (i, c, 0, 0))]
    args = [p]
    if has_tail:
        rb = grp.row0 // grp.t
        in_specs.append(pl.BlockSpec((grp.t, HEAD_DIM), lambda c, i: (rb + i, C_ROWS // HEAD_DIM + c)))
        args.append(u)
    in_specs += [
        pl.BlockSpec((1, half, 2 * HEAD_DIM), lambda c, i: (c // NSA_KV_HEADS, 0, 0)),
        pl.BlockSpec((1, V7X_SUBLANES, half), lambda c, i: (c // NSA_KV_HEADS, 0, 0)),
        pl.BlockSpec((1, HEAD_DIM, HEAD_DIM), lambda c, i: (c // NSA_KV_HEADS, 0, 0)),
        pl.BlockSpec((1, HEAD_DIM), lambda c, i: (0, 0)),
    ]
    args += [w1ab, pos2, lp['nsa_cmp_w2'], lp['nsa_k_norm'][0:1]]
    return pl.pallas_call(
        functools.partial(_cmp_finish_body, has_tail=has_tail, n_out=n_out),
        grid=(4, b),
        in_specs=in_specs,
        out_specs=pl.BlockSpec((1, 1, n_out, HEAD_DIM), lambda c, i: (i, c, 0, 0)),
        out_shape=jax.ShapeDtypeStruct((b, 4, n_out, HEAD_DIM), F32),
        compiler_params=_cparams(("parallel", "parallel")),
        name="nsa_cmp_finish",
    )(*args)


def _compress_prompt(u, grp, lp):
    assert grp.past == 0 and grp.row0 == 0 and grp.t % (CMP_STRIDE * V7X_SUBLANES) == 0
    w1ab, pos2 = _cmp_weights(lp)
    n = grp.t // CMP_STRIDE
    group = lambda c: pl.BlockSpec((grp.t, HEAD_DIM), lambda i: (i, C_ROWS // HEAD_DIM + c))
    p = pl.pallas_call(
        _cmp_partial_body,
        grid=(grp.b,),
        in_specs=[group(0), group(1), group(2), group(3), pl.BlockSpec(w1ab.shape, lambda i: (0, 0, 0))],
        out_specs=pl.BlockSpec((1, 4, n, 2 * HEAD_DIM), lambda i: (i, 0, 0, 0)),
        out_shape=jax.ShapeDtypeStruct((grp.b, 4, n, 2 * HEAD_DIM), F32),
        compiler_params=_cparams(("parallel",)),
        name="nsa_cmp_partial",
    )(u, u, u, u, w1ab)
    return _cmp_finish(p, None, w1ab, pos2, lp, _round_up(n, V7X_LANES), grp, u), n - 1


def _pages_per_step(n_pages):
    return _pick_tile(n_pages, (16, 8, 4, 2, 1))


def _page_view(cache):
    assert cache.shape[3] * cache.shape[4] == ROW_GROUPS and cache.shape[5] == HEAD_DIM
    return cache.reshape(cache.shape[0], cache.shape[1], PAGE_SIZE * ROW_GROUPS, HEAD_DIM)


def _page_specs(layer, n_pages, pps):
    def spec(i):
        return pl.BlockSpec((1, 1, PAGE_SIZE * ROW_GROUPS, HEAD_DIM),
                            lambda b, c, pt: (layer, pt[b * n_pages + c * pps + i], 0, 0))
    return [spec(i) for i in range(pps)]


def _page_rows(page, j):
    return page[0, 0, pl.ds(j, PAGE_SIZE, stride=ROW_GROUPS), :]


def _page_piece(pages, j):
    return jnp.concatenate([_page_rows(pg, j).astype(BF16) for pg in pages], axis=0)


def _compress_sample(u, grp, lp, cache, page_table, layer):
    assert grp.past % PAGE_SIZE == 0 and grp.t < CMP_STRIDE
    w1ab, pos2 = _cmp_weights(lp)
    n_pages = grp.past // PAGE_SIZE
    pps = _pages_per_step(n_pages)
    per = pps * PAGE_SIZE // CMP_STRIDE
    n = grp.past // CMP_STRIDE
    p, slc_kv = pl.pallas_call(
        functools.partial(_cmp_partial_paged_body, pps=pps),
        grid_spec=pltpu.PrefetchScalarGridSpec(
            num_scalar_prefetch=1,
            grid=(grp.b, n_pages // pps),
            in_specs=_page_specs(layer, n_pages, pps) + [
                pl.BlockSpec(w1ab.shape, lambda b, c, pt: (0, 0, 0))],
            out_specs=[
                pl.BlockSpec((1, 4, per, 2 * HEAD_DIM), lambda b, c, pt: (b, 0, c, 0)),
                pl.BlockSpec((1, 4, pps * PAGE_SIZE, HEAD_DIM), lambda b, c, pt: (b, 0, c, 0)),
            ],
            scratch_shapes=[pltpu.VMEM((4, pps * PAGE_SIZE, HEAD_DIM), F32)],
        ),
        out_shape=[
            jax.ShapeDtypeStruct((grp.b, 4, n, 2 * HEAD_DIM), F32),
            jax.ShapeDtypeStruct((grp.b, 4, grp.past, HEAD_DIM), BF16),
        ],
        compiler_params=_cparams(("parallel", "arbitrary")),
        name="nsa_cmp_partial_paged",
    )(page_table, *([cache] * pps), w1ab)
    return _cmp_finish(p, u, w1ab, pos2, lp, _round_up(n + 1, V7X_LANES), grp, u), n, slc_kv


def _cmp_select_body(q_ref, kv_ref, o_ref, sel_ref, *, tq, q_off, n_cmp, n_slc, n_sel):
    qi = pl.program_id(1)
    n_cpad = kv_ref.shape[2]
    n_spad = sel_ref.shape[1] // NSA_KV_HEADS
    q_lo = q_off + qi * tq
    qpos = q_lo + lax.broadcasted_iota(jnp.int32, (tq, n_cpad), 0)
    ends = lax.broadcasted_iota(jnp.int32, (tq, n_cpad), 1) * CMP_STRIDE + (CMP_BLOCK - 1)
    d = qpos - ends
    mask = d >= 0
    df = d.astype(F32)
    c_lo = lax.broadcasted_iota(jnp.int32, (n_cpad, n_spad), 0) * CMP_STRIDE
    j_lo = lax.broadcasted_iota(jnp.int32, (n_cpad, n_spad), 1) * SLC_BLOCK
    cover = jnp.where(c_lo < j_lo + SLC_BLOCK, jnp.where(c_lo + CMP_BLOCK > j_lo, 1.0, 0.0), 0.0)
    cover = jnp.where(c_lo < n_cmp * CMP_STRIDE, cover, 0.0)
    qp = q_lo + lax.broadcasted_iota(jnp.int32, (tq, n_spad), 0)
    j = lax.broadcasted_iota(jnp.int32, (tq, n_spad), 1)
    cur = lax.shift_right_logical(qp, SLC_SHIFT)
    bonus = FORCE_BONUS * jnp.where(j == 0, 1.0, jnp.where(j == cur, 1.0, jnp.where(j == cur - 1, 1.0, 0.0)))
    valid = j * SLC_BLOCK <= qp
    for k in range(NSA_KV_HEADS):
        q = jnp.concatenate([q_ref[:, (k * NSA_GROUP + g) * HEAD_DIM:(k * NSA_GROUP + g + 1) * HEAD_DIM]
                             for g in range(NSA_GROUP)], axis=0)
        s = lax.dot_general(q, kv_ref[0, k], NT, precision=HIGHEST, preferred_element_type=F32)
        vc = kv_ref[0, NSA_KV_HEADS + k].astype(BF16)
        psum = jnp.zeros((tq, n_cpad), F32)
        for g in range(NSA_GROUP):
            h = k * NSA_GROUP + g
            sg = jnp.where(mask, s[g * tq:(g + 1) * tq] - (_NSA_SLOPES[h] * LOG2E) * df, NEG_INF)
            m = jnp.max(sg, axis=-1, keepdims=True)
            p = jnp.where(mask, jnp.exp2(sg - m), 0.0)
            p = p / jnp.maximum(jnp.sum(p, axis=-1, keepdims=True), 1e-30)
            o_ref[:, h * HEAD_DIM:(h + 1) * HEAD_DIM] = jnp.dot(p.astype(BF16), vc, preferred_element_type=F32)
            psum = psum + p
        imp = jnp.dot(psum, cover, precision=HIGHEST, preferred_element_type=F32)
        score = jnp.where(valid, imp + bonus, NEG_INF)
        cnt = jnp.zeros((tq, n_spad), F32)
        for i in range(n_slc):
            ci = score[:, i:i + 1]
            cnt = cnt + jnp.where(ci > score, 1.0, jnp.where(ci == score, jnp.where(j > i, 1.0, 0.0), 0.0))
        sel_ref[:, k * n_spad:(k + 1) * n_spad] = jnp.where(cnt < n_sel, jnp.where(j < n_slc, 0.0, NEG_INF), NEG_INF)


def _cmp_select(u, grp, summaries, n_cmp):
    n_slc = -(-(grp.past + grp.t) // SLC_BLOCK)
    n_spad = _round_up(n_slc, V7X_LANES)
    n_cpad = summaries.shape[2]
    tq = _pick_tile(grp.t, (256, 128, 64, 32, 16, 8))
    nq = grp.t // tq
    rb = grp.row0 // tq
    body = functools.partial(_cmp_select_body, tq=tq, q_off=grp.past, n_cmp=n_cmp, n_slc=n_slc,
                             n_sel=min(SLC_TOPN, n_slc))
    return pl.pallas_call(
        body,
        grid=(grp.b, nq),
        in_specs=[
            pl.BlockSpec((tq, NSA_W), lambda b, i: (rb + b * nq + i, 0)),
            pl.BlockSpec((1, 4, n_cpad, HEAD_DIM), lambda b, i: (b, 0, 0, 0)),
        ],
        out_specs=[
            pl.BlockSpec((tq, NSA_W), lambda b, i: (b * nq + i, 0)),
            pl.BlockSpec((tq, NSA_KV_HEADS * n_spad), lambda b, i: (b * nq + i, 0)),
        ],
        out_shape=[
            jax.ShapeDtypeStruct((grp.b * grp.t, NSA_W), F32),
            jax.ShapeDtypeStruct((grp.b * grp.t, NSA_KV_HEADS * n_spad), F32),
        ],
        compiler_params=_cparams(("parallel", "parallel")),
        name="nsa_cmp_select",
    )(u, summaries)


def _nsa_prompt_body(q_ref, ks_ref, kw_ref, sel_ref, oc_ref, gt_ref, o_ref, m_ref, l_ref, acc_ref, *, tq, tk):
    qi = pl.program_id(1)
    kj = pl.program_id(2)
    n_spad = sel_ref.shape[1] // NSA_KV_HEADS

    @pl.when(kj == 0)
    def _():
        _init_state(m_ref, l_ref, acc_ref)

    q_lo = qi * tq
    k_lo = kj * tk
    slc_on = k_lo <= q_lo + (tq - 1)
    win_on = jnp.logical_and(slc_on, k_lo + (tk - 1) >= q_lo - WINDOW)

    krel = _key_offsets(tk, k_lo - q_lo)

    @pl.when(slc_on)
    def _():
        causal = _mask_bias(_distance(tq, tk, q_lo - k_lo) >= 0)
        expand = _block_expand(n_spad, tk, k_lo)
        kv = ks_ref[...].astype(BF16)
        for k in range(NSA_KV_HEADS):
            s = lax.dot_general(_stack_heads(q_ref, k), kv[:, k * HEAD_DIM:(k + 1) * HEAD_DIM], NT,
                                preferred_element_type=F32)
            chosen = jnp.dot(sel_ref[:, k * n_spad:(k + 1) * n_spad].astype(BF16), expand,
                             preferred_element_type=F32)
            _online_update(s, causal + chosen, krel, _NSA_SLOPES[k * NSA_GROUP:(k + 1) * NSA_GROUP],
                           kv[:, (NSA_KV_HEADS + k) * HEAD_DIM:(NSA_KV_HEADS + k + 1) * HEAD_DIM],
                           m_ref, l_ref, acc_ref, k * NSA_GROUP * tq, tq)

    @pl.when(win_on)
    def _():
        d = _distance(tq, tk, q_lo - k_lo)
        bias = _mask_bias(jnp.logical_and(d >= 0, d <= WINDOW))
        kv = kw_ref[...].astype(BF16)
        for k in range(NSA_KV_HEADS):
            s = lax.dot_general(_stack_heads(q_ref, k), kv[:, k * HEAD_DIM:(k + 1) * HEAD_DIM], NT,
                                preferred_element_type=F32)
            _online_update(s, bias, krel, _NSA_SLOPES[k * NSA_GROUP:(k + 1) * NSA_GROUP],
                           kv[:, (NSA_KV_HEADS + k) * HEAD_DIM:(NSA_KV_HEADS + k + 1) * HEAD_DIM],
                           m_ref, l_ref, acc_ref, (NSA_KV_HEADS + k) * NSA_GROUP * tq, tq)

    @pl.when(kj == pl.num_programs(2) - 1)
    def _():
        _nsa_combine(gt_ref, oc_ref, o_ref, l_ref, acc_ref, tq)


def _nsa_prompt(u, grp, o_cmp, sel):
    assert grp.past == 0 and grp.row0 == 0
    t = grp.t
    tq = _pick_tile(t, (512, 256, 128, 64, 32, 16, 8))
    tk = _pick_tile(t, (512, 256, 128))
    nq, nk = t // tq, t // tk
    kvw = 2 * NSA_KV_W

    def last_tile(i):
        return (i * tq + (tq - 1)) // tk

    def slc_rows(b, i, j):
        return (b * nk + jnp.minimum(j, last_tile(i)), (C_ROWS + kvw) // kvw)

    def win_rows(b, i, j):
        first = jnp.maximum(i * tq - WINDOW, 0) // tk
        return (b * nk + jnp.clip(j, first, last_tile(i)), C_WIN // kvw)

    return pl.pallas_call(
        functools.partial(_nsa_prompt_body, tq=tq, tk=tk),
        grid=(grp.b, nq, nk),
        in_specs=[
            pl.BlockSpec((tq, NSA_W), lambda b, i, j: (b * nq + i, 0)),
            pl.BlockSpec((tk, kvw), slc_rows),
            pl.BlockSpec((tk, kvw), win_rows),
            pl.BlockSpec((tq, sel.shape[1]), lambda b, i, j: (b * nq + i, 0)),
            pl.BlockSpec((tq, NSA_W), lambda b, i, j: (b * nq + i, 0)),
            pl.BlockSpec((tq, U_TILE), lambda b, i, j: (b * nq + i, C_SMALL // U_TILE)),
        ],
        out_specs=pl.BlockSpec((tq, NSA_W), lambda b, i, j: (b * nq + i, 0)),
        out_shape=jax.ShapeDtypeStruct((grp.b * t, NSA_W), F32),
        scratch_shapes=_nsa_state(tq),
        compiler_params=_cparams(("parallel", "parallel", "arbitrary")),
        name="nsa_attn_prompt",
    )(u, u, u, sel, o_cmp, u)


def _nsa_sample_body(q_ref, kv_ref, knew_ref, wst_ref, wnew_ref, sel_ref, oc_ref, gt_ref, o_ref, m_ref, l_ref, acc_ref,
                     *, past, n_win):
    c = pl.program_id(1)
    tq = q_ref.shape[0]
    n_spad = sel_ref.shape[1] // NSA_KV_HEADS
    n = kv_ref.shape[2]

    @pl.when(c == 0)
    def _():
        _init_state(m_ref, l_ref, acc_ref)

    def attend(keys_of, vals_of, bias_of, krel, branch):
        for k in range(NSA_KV_HEADS):
            s = lax.dot_general(_stack_heads(q_ref, k), keys_of(k), NT, preferred_element_type=F32)
            _online_update(s, bias_of(k), krel, _NSA_SLOPES[k * NSA_GROUP:(k + 1) * NSA_GROUP], vals_of(k),
                           m_ref, l_ref, acc_ref, (branch * NSA_KV_HEADS + k) * NSA_GROUP * tq, tq)

    def slab(ref, j):
        return ref[:, j * HEAD_DIM:(j + 1) * HEAD_DIM]

    first_key = c * n
    expand = _block_expand(n_spad, n, first_key)
    attend(lambda k: kv_ref[0, k], lambda k: kv_ref[0, NSA_KV_HEADS + k],
           lambda k: jnp.dot(sel_ref[:, k * n_spad:(k + 1) * n_spad].astype(BF16), expand,
                             preferred_element_type=F32),
           _key_offsets(n, first_key - past), 0)

    @pl.when(c == pl.num_programs(1) - 1)
    def _():
        npad = V7X_LANES
        real = lax.broadcasted_iota(jnp.int32, (tq, npad), 1) < tq
        causal = _mask_bias(jnp.logical_and(_distance(tq, npad, 0) >= 0, real))
        lane = lax.broadcasted_iota(jnp.int32, (tq, n_spad), 1)

        def chosen_new(k):
            col = jnp.where(lane == past // SLC_BLOCK, sel_ref[:, k * n_spad:(k + 1) * n_spad], 0.0)
            return causal + jnp.sum(col, axis=-1, keepdims=True)

        knew = _pad_keys(knew_ref[...], npad).astype(BF16)
        attend(lambda k: slab(knew, k), lambda k: slab(knew, NSA_KV_HEADS + k), chosen_new, _key_offsets(npad, 0), 0)
        wbuf = wst_ref[0, 0].astype(BF16)
        in_window = _mask_bias(_distance(tq, n_win, n_win) <= WINDOW)
        attend(lambda k: slab(wbuf, k), lambda k: slab(wbuf, NSA_KV_HEADS + k), lambda k: in_window,
               _key_offsets(n_win, -n_win), 1)
        wnew = _pad_keys(wnew_ref[...], npad).astype(BF16)
        attend(lambda k: slab(wnew, k), lambda k: slab(wnew, NSA_KV_HEADS + k), lambda k: causal,
               _key_offsets(npad, 0), 1)
        _nsa_combine(gt_ref, oc_ref, o_ref, l_ref, acc_ref, tq)


def _nsa_sample(u, grp, o_cmp, sel, slc_kv, win_state, layer):
    t = grp.t
    n_win = win_state.shape[2]
    assert n_win == WINDOW and grp.past % SLC_BLOCK == 0 and t <= SLC_BLOCK and grp.row0 % t == 0
    n = _pick_tile(grp.past, (2048, 1024, 512, 256, 128))
    kvw = 2 * NSA_KV_W
    rb = grp.row0 // t
    tok = lambda width, col: pl.BlockSpec((t, width), lambda b, c: (rb + b, col))
    own = lambda width: pl.BlockSpec((t, width), lambda b, c: (b, 0))
    return pl.pallas_call(
        functools.partial(_nsa_sample_body, past=grp.past, n_win=n_win),
        grid=(grp.b, grp.past // n),
        in_specs=[
            tok(NSA_W, 0),
            pl.BlockSpec((1, 4, n, HEAD_DIM), lambda b, c: (b, 0, c, 0)),
            tok(kvw, (C_ROWS + kvw) // kvw),
            pl.BlockSpec((1, 1, n_win, kvw), lambda b, c: (layer, b, 0, 0)),
            tok(kvw, C_WIN // kvw),
            own(sel.shape[1]),
            own(NSA_W),
            tok(U_TILE, C_SMALL // U_TILE),
        ],
        out_specs=own(NSA_W),
        out_shape=jax.ShapeDtypeStruct((grp.b * t, NSA_W), F32),
        scratch_shapes=_nsa_state(t),
        compiler_params=_cparams(("parallel", "arbitrary")),
        name="nsa_attn_cached",
    )(u, slc_kv, u, win_state, u, sel, o_cmp, u)


def _diff_update(q_ref, group_of, bias, krel, m_ref, l_ref, acc_ref):
    tq = q_ref.shape[0]
    low = lax.broadcasted_iota(jnp.int32, (tq, HEAD_DIM), 1) < DIFF_QK_DIM
    scores, shifts = [], []
    for h in range(DIFF_HEADS):
        qh = q_ref[:, h * HEAD_DIM:(h + 1) * HEAD_DIM]
        q2 = jnp.concatenate([jnp.where(low, qh, 0.0), jnp.where(low, 0.0, qh)], axis=0).astype(BF16)
        scores.append(lax.dot_general(q2, group_of(h), NT, preferred_element_type=F32))
        shift = (_DIFF_SLOPES[h] * LOG2E) * krel
        shift = jnp.broadcast_to(shift, (tq, shift.shape[1])) if bias is None else bias + shift
        shifts += [shift, shift]
    p, alpha = _online_softmax(jnp.concatenate(scores, axis=0), jnp.concatenate(shifts, axis=0), m_ref, l_ref,
                               slice(0, 2 * DIFF_HEADS * tq))
    for h in range(DIFF_HEADS):
        rows = slice(2 * h * tq, 2 * (h + 1) * tq)
        acc_ref[rows] = alpha[rows] * acc_ref[rows] + jnp.dot(p[rows], group_of(DIFF_HEADS + h),
                                                               preferred_element_type=F32)


def _diff_finish(lam_ref, g_ref, o_ref, l_ref, acc_ref, post):
    tq = o_ref.shape[0]
    lam = lam_ref[0]
    for h in range(DIFF_HEADS):
        o1 = _normalised(l_ref, acc_ref, slice(2 * h * tq, (2 * h + 1) * tq))
        o2 = _normalised(l_ref, acc_ref, slice((2 * h + 1) * tq, (2 * h + 2) * tq))
        o_ref[:, h * HEAD_DIM:(h + 1) * HEAD_DIM] = _rms_rows(o1 - lam * o2, g_ref[...]) * post


def _diff_prompt_body(lam_ref, q_ref, kv_ref, g_ref, o_ref, m_ref, l_ref, acc_ref, *, tq, tk, post):
    qi = pl.program_id(1)
    kj = pl.program_id(2)

    @pl.when(kj == 0)
    def _():
        _init_state(m_ref, l_ref, acc_ref)

    @pl.when(kj * tk <= qi * tq + (tq - 1))
    def _():
        kv = kv_ref[...].astype(BF16)
        _diff_update(q_ref, lambda j: kv[:, j * HEAD_DIM:(j + 1) * HEAD_DIM],
                     _mask_bias(_distance(tq, tk, qi * tq - kj * tk) >= 0), _key_offsets(tk, kj * tk - qi * tq),
                     m_ref, l_ref, acc_ref)

    @pl.when(kj == pl.num_programs(2) - 1)
    def _():
        _diff_finish(lam_ref, g_ref, o_ref, l_ref, acc_ref, post)


def _diff_prompt(u, grp, lam, gain, post):
    assert grp.past == 0 and grp.row0 == 0
    t = grp.t
    tq = _pick_tile(t, (512, 256, 128, 64, 32, 16, 8))
    tk = _pick_tile(t, (512, 256, 128))
    nq, nk = t // tq, t // tk
    return pl.pallas_call(
        functools.partial(_diff_prompt_body, tq=tq, tk=tk, post=post),
        grid=(grp.b, nq, nk),
        in_specs=[
            pl.BlockSpec(memory_space=pltpu.SMEM),
            pl.BlockSpec((tq, DIFF_W), lambda b, i, j: (b * nq + i, C_DQ // DIFF_W)),
            pl.BlockSpec((tk, 2 * DIFF_W),
                         lambda b, i, j: (b * nk + jnp.minimum(j, (i * tq + (tq - 1)) // tk), C_DROWS // (2 * DIFF_W))),
            pl.BlockSpec((1, HEAD_DIM), lambda b, i, j: (0, 0)),
        ],
        out_specs=pl.BlockSpec((tq, DIFF_W), lambda b, i, j: (b * nq + i, 0)),
        out_shape=jax.ShapeDtypeStruct((grp.b * t, DIFF_W), F32),
        scratch_shapes=_diff_state(tq),
        compiler_params=_cparams(("parallel", "parallel", "arbitrary")),
        name="diff_attn_prompt",
    )(lam, u, u, gain)


def _diff_sample_body(pt_ref, lam_ref, q_ref, *rest, pps, past, post):
    pages = rest[:pps]
    new_ref, g_ref, o_ref, m_ref, l_ref, acc_ref = rest[pps:]
    del pt_ref
    c = pl.program_id(1)
    tq = q_ref.shape[0]
    n = pps * PAGE_SIZE

    @pl.when(c == 0)
    def _():
        _init_state(m_ref, l_ref, acc_ref)

    _diff_update(q_ref, lambda j: _page_piece(pages, j), None, _key_offsets(n, c * n - past), m_ref, l_ref, acc_ref)

    @pl.when(c == pl.num_programs(1) - 1)
    def _():
        npad = V7X_LANES
        keep = jnp.logical_and(_distance(tq, npad, 0) >= 0, lax.broadcasted_iota(jnp.int32, (tq, npad), 1) < tq)
        new = _pad_keys(new_ref[...], npad).astype(BF16)
        _diff_update(q_ref, lambda j: new[:, j * HEAD_DIM:(j + 1) * HEAD_DIM], _mask_bias(keep),
                     _key_offsets(npad, 0), m_ref, l_ref, acc_ref)
        _diff_finish(lam_ref, g_ref, o_ref, l_ref, acc_ref, post)


def _diff_sample(u, grp, lam, gain, post, cache, page_table, layer):
    t = grp.t
    assert grp.past % PAGE_SIZE == 0 and grp.row0 % t == 0
    n_pages = grp.past // PAGE_SIZE
    pps = _pages_per_step(n_pages)
    rb = grp.row0 // t
    return pl.pallas_call(
        functools.partial(_diff_sample_body, pps=pps, past=grp.past, post=post),
        grid_spec=pltpu.PrefetchScalarGridSpec(
            num_scalar_prefetch=1,
            grid=(grp.b, n_pages // pps),
            in_specs=[
                pl.BlockSpec(memory_space=pltpu.SMEM),
                pl.BlockSpec((t, DIFF_W), lambda b, c, pt: (rb + b, C_DQ // DIFF_W)),
            ] + _page_specs(layer, n_pages, pps) + [
                pl.BlockSpec((t, 2 * DIFF_W), lambda b, c, pt: (rb + b, C_DROWS // (2 * DIFF_W))),
                pl.BlockSpec((1, HEAD_DIM), lambda b, c, pt: (0, 0)),
            ],
            out_specs=pl.BlockSpec((t, DIFF_W), lambda b, c, pt: (b, 0)),
            scratch_shapes=_diff_state(t),
        ),
        out_shape=jax.ShapeDtypeStruct((grp.b * t, DIFF_W), F32),
        compiler_params=_cparams(("parallel", "arbitrary")),
        name="diff_attn_paged",
    )(page_table, lam, u, *([cache] * pps), u, gain)


def _mlstm_body(xq_ref, xk_ref, v_ref, og_ref, bq_ref, bk_ref, wq_ref, wk_ref, cq_ref, ck_ref, li_ref, lf_ref,
                c0_ref, n0_ref, m0_ref, g_ref, h_ref, c_ref, n_ref, m_ref, q_s, k_s, *, lc, nc):
    t = xq_ref.shape[0]
    row = lax.broadcasted_iota(jnp.int32, (lc, lc), 0)
    col = lax.broadcasted_iota(jnp.int32, (lc, lc), 1)
    tril = row >= col
    triu = row <= col
    eye = row == col
    gain = g_ref[...]

    def conv(x_ref, buf_ref, w_ref, b_ref):
        ext = jnp.concatenate([buf_ref[0], x_ref[...]], axis=0)
        y = b_ref[...]
        for i in range(CONV_WIDTH):
            shifted = ext if i == CONV_WIDTH - 1 else pltpu.roll(ext, CONV_WIDTH - 1 - i, 0)
            y = y + shifted[V7X_SUBLANES:V7X_SUBLANES + t] * w_ref[i:i + 1, :]
        return y * jax.nn.sigmoid(y)

    q_s[...] = conv(xq_ref, bq_ref, wq_ref, cq_ref) * MLSTM_DIM ** -0.5
    k_s[...] = conv(xk_ref, bk_ref, wk_ref, ck_ref)

    def to_col(r):
        return jnp.sum(jnp.where(eye, jnp.broadcast_to(r, (lc, lc)), 0.0), axis=1, keepdims=True)

    def step(ci, carry):
        c, n, m = carry
        rows = pl.ds(ci * lc if nc == 1 else pl.multiple_of(ci * lc, lc), lc)
        qc = q_s[rows, :]
        kc = k_s[rows, :]
        vc = v_ref[rows, :]
        li = li_ref[0, pl.ds(ci, 1), :]
        lf = lf_ref[0, pl.ds(ci, 1), :]
        lf_b = jnp.broadcast_to(lf, (lc, lc))
        bcum_c = jnp.sum(jnp.where(tril, lf_b, 0.0), axis=1, keepdims=True)
        bcum_r = jnp.sum(jnp.where(triu, jnp.broadcast_to(to_col(lf), (lc, lc)), 0.0), axis=0, keepdims=True)
        dmat = jnp.where(tril, bcum_c - bcum_r + li, NEG_INF)
        inter = bcum_c + m
        mj = jnp.maximum(inter, jnp.max(dmat, axis=1, keepdims=True))
        wts = jnp.exp(dmat - mj)
        a = jnp.exp(inter - mj)
        sqk = lax.dot_general(qc, kc, NT, preferred_element_type=F32) * wts
        num = a * jnp.dot(qc, c, preferred_element_type=F32) + jnp.dot(sqk, vc, preferred_element_type=F32)
        den = a * jnp.sum(qc * n, axis=1, keepdims=True) + jnp.sum(sqk, axis=1, keepdims=True)
        h = num / jnp.maximum(jnp.abs(den), jnp.exp(-mj))
        h_ref[rows, :] = _rms_rows(h, gain) * jax.nn.sigmoid(og_ref[rows, :])
        b_last = jnp.sum(lf, axis=1, keepdims=True)
        gl = b_last - bcum_r + li
        m_new = jnp.maximum(b_last + m, jnp.max(gl, axis=1, keepdims=True))
        decay = jnp.exp(b_last + m - m_new)
        kw = kc * to_col(jnp.exp(gl - m_new))
        c_new = decay * c + lax.dot_general(kw, vc, (((0,), (0,)), ((), ())), precision=HIGHEST,
                                            preferred_element_type=F32)
        n_new = decay * n + jnp.sum(kw, axis=0, keepdims=True)
        return c_new, n_new, m_new

    init = (c0_ref[0], n0_ref[0], m0_ref[0][:, 0:1])
    c, n, m = step(0, init) if nc == 1 else lax.fori_loop(0, nc, step, init, unroll=2)
    c_ref[0] = c
    n_ref[0] = n
    m_ref[0] = jnp.broadcast_to(m, (1, MLSTM_DIM))


def _mlstm(u, grp, lp, c0, n0, m0, conv_buf):
    b, t = grp.b, grp.t
    assert grp.row0 % t == 0 and t % V7X_SUBLANES == 0
    bh = b * MLSTM_HEADS
    hd = MLSTM_DIM
    lc = MLSTM_CHUNK if t % MLSTM_CHUNK == 0 else t
    nc = t // lc
    rb = grp.row0 // t
    gates = u[grp.row0:grp.row0 + b * t, C_SMALL:C_SMALL + U_TILE].reshape(b, t, U_TILE)
    log_i = gates[:, :, G_MI:G_MI + MLSTM_HEADS] + lp['ml_gate_b'][0]
    log_f = jax.nn.log_sigmoid(gates[:, :, G_MF:G_MF + MLSTM_HEADS] + lp['ml_gate_b'][1])
    log_i = log_i.transpose(0, 2, 1).reshape(bh, nc, lc)
    log_f = log_f.transpose(0, 2, 1).reshape(bh, nc, lc)
    buf8 = jnp.pad(conv_buf, ((0, 0), (V7X_SUBLANES - (CONV_WIDTH - 1), 0), (0, 0)))
    w8 = jnp.pad(lp['ml_conv_w'], ((0, V7X_SUBLANES - CONV_WIDTH), (0, 0)))
    cb = lp['ml_conv_b'].reshape(1, 2 * MLSTM_W)
    tok = lambda col0: pl.BlockSpec((t, hd), lambda i, h: (rb + i, col0 // hd + h))
    buf = lambda off: pl.BlockSpec((1, V7X_SUBLANES, hd), lambda i, h: (i, 0, off + h))
    wsp = lambda off: pl.BlockSpec((V7X_SUBLANES, hd), lambda i, h: (0, off + h))
    bsp = lambda off: pl.BlockSpec((1, hd), lambda i, h: (0, off + h))
    gate = pl.BlockSpec((1, nc, lc), lambda i, h: (i * MLSTM_HEADS + h, 0, 0))
    vec = pl.BlockSpec((1, 1, hd), lambda i, h: (i * MLSTM_HEADS + h, 0, 0))
    mat = pl.BlockSpec((1, hd, hd), lambda i, h: (i * MLSTM_HEADS + h, 0, 0))
    h, c, n, m = pl.pallas_call(
        functools.partial(_mlstm_body, lc=lc, nc=nc),
        grid=(b, MLSTM_HEADS),
        in_specs=[tok(C_MQ), tok(C_MK), tok(C_MV), tok(C_MO), buf(0), buf(MLSTM_HEADS), wsp(0), wsp(MLSTM_HEADS),
                  bsp(0), bsp(MLSTM_HEADS), gate, gate, mat, vec, vec, pl.BlockSpec((1, hd), lambda i, h: (0, 0))],
        out_specs=[pl.BlockSpec((t, hd), lambda i, h: (i, h)), mat, vec, vec],
        out_shape=[
            jax.ShapeDtypeStruct((b * t, MLSTM_W), F32),
            jax.ShapeDtypeStruct((bh, hd, hd), F32),
            jax.ShapeDtypeStruct((bh, 1, hd), F32),
            jax.ShapeDtypeStruct((bh, 1, hd), F32),
        ],
        scratch_shapes=[pltpu.VMEM((t, hd), F32), pltpu.VMEM((t, hd), F32)],
        compiler_params=_cparams(("parallel", "parallel")),
        name="mlstm",
    )(u, u, u, u, buf8, buf8, w8, w8, cb, cb, log_i, log_f, c0.reshape(bh, hd, hd), n0.reshape(bh, 1, hd),
      jnp.broadcast_to(m0.reshape(bh, 1, 1), (bh, 1, hd)), lp['ml_out_norm'].reshape(1, hd))
    return (h, c.reshape(b, MLSTM_HEADS, hd, hd), n.reshape(b, MLSTM_HEADS, hd), m[:, 0, 0].reshape(b, MLSTM_HEADS))


def _cache_rows_body(nsa_ref, diff_ref, *rest):
    nsa_out, diff_out = rest[-2:]
    tm = nsa_ref.shape[0]
    for j in range(ROW_GROUPS):
        cols = slice(j * HEAD_DIM, (j + 1) * HEAD_DIM)
        nsa_out[pl.ds(j, tm, stride=ROW_GROUPS), :] = nsa_ref[:, cols]
        diff_out[pl.ds(j, tm, stride=ROW_GROUPS), :] = diff_ref[:, cols]


def _cache_rows(u, layer, depth, prev):
    t = u.shape[0]
    tm = _pick_tile(t, (256, 128, 64, 32, 16, 8))
    width = ROW_GROUPS * HEAD_DIM
    in_specs = [pl.BlockSpec((tm, width), lambda i: (i, C_ROWS // width)),
                pl.BlockSpec((tm, width), lambda i: (i, C_DROWS // width))]
    args = [u, u]
    aliases = {}
    if prev is not None:
        in_specs += [pl.BlockSpec(memory_space=pl.ANY)] * 2
        args += list(prev)
        aliases = {2: 0, 3: 1}
    out = pl.BlockSpec((None, tm * ROW_GROUPS, HEAD_DIM), lambda i: (layer, i, 0))
    shape = jax.ShapeDtypeStruct((depth, t * ROW_GROUPS, HEAD_DIM), F32)
    return pl.pallas_call(
        _cache_rows_body,
        grid=(t // tm,),
        in_specs=in_specs,
        out_specs=[out, out],
        out_shape=[shape, shape],
        input_output_aliases=aliases,
        compiler_params=_cparams(("parallel",)),
        name="cache_rows",
    )(*args)


def _pack_w_in(w_in):
    cuts = np.cumsum((0,) + IN_SPLITS)
    cols = [w_in[..., cuts[i]:cuts[i + 1]] for i in range(len(IN_SPLITS))]
    packed = jnp.concatenate([cols[i] for i in _MAIN_ORDER] + [cols[i] for i in _SMALL_ORDER], axis=-1)
    pad = [(0, 0)] * (packed.ndim - 1) + [(0, U_COLS - packed.shape[-1])]
    return jnp.pad(packed, pad).astype(BF16)


def _u_gain(lp):
    ones = lambda n: jnp.ones((n,), F32)
    g = jnp.concatenate([
        jnp.tile(lp['nsa_q_norm'], NSA_HEADS) * (HEAD_DIM ** -0.5 * LOG2E),
        ones(2 * NSA_KV_W), jnp.tile(lp['nsa_k_norm'][1], NSA_KV_HEADS), ones(NSA_KV_W),
        jnp.tile(lp['nsa_k_norm'][2], NSA_KV_HEADS), ones(NSA_KV_W),
        jnp.tile(lp['diff_q_norm'], 2 * DIFF_HEADS) * (DIFF_QK_DIM ** -0.5 * LOG2E),
        jnp.tile(lp['diff_k_norm'], 2 * DIFF_HEADS),
        ones(U_COLS - (C_DROWS + DIFF_W))])
    return g.reshape(1, U_COLS)


def _group_rows(u, grp, col0, width):
    return u[grp.row0:grp.row0 + grp.b * grp.t, col0:col0 + width].reshape(grp.b, grp.t, width)


def _tail_rows(buf, new, keep):
    n_new = min(keep, new.shape[1])
    parts = [buf[:, buf.shape[1] - (keep - n_new):], new[:, new.shape[1] - n_new:]] if keep > n_new else \
        [new[:, new.shape[1] - n_new:]]
    return jnp.concatenate(parts, axis=1)


def _trunk(x_prompt, x_sample, cache_nsa, cache_diff, state_win, state_ml_c, state_ml_n, state_ml_m, state_ml_conv,
           page_table, p):
    bp, tp, d = x_prompt.shape
    bs, ts, _ = x_sample.shape
    depth = p['w_in'].shape[0]
    past = page_table.shape[1] * PAGE_SIZE
    gp = _Group(bp, tp, 0, 0)
    gs = _Group(bs, ts, bp * tp, past)
    n_p = bp * tp
    w_in = _pack_w_in(p['w_in'])
    w_out = p['w_out'].astype(BF16)
    w13 = (p['ffn1_w13'].astype(BF16), p['ffn2_w13'].astype(BF16))
    w2 = (p['ffn1_w2'].astype(BF16), p['ffn2_w2'].astype(BF16))
    per_layer = ('mix_norm', 'nsa_q_norm', 'nsa_k_norm', 'nsa_cmp_pos', 'nsa_cmp_w1', 'nsa_cmp_w2', 'diff_q_norm',
                 'diff_k_norm', 'diff_lambda', 'diff_out_norm', 'ml_conv_w', 'ml_conv_b', 'ml_gate_b', 'ml_out_norm')
    cache_nsa = _page_view(cache_nsa)
    cache_diff = _page_view(cache_diff)
    win_state = state_win.reshape(state_win.shape[:3] + (2 * NSA_KV_W,))
    pt = page_table.reshape(-1)
    zeros = lambda *s: jnp.zeros(s, F32)
    p_state = (zeros(bp, MLSTM_HEADS, MLSTM_DIM, MLSTM_DIM), zeros(bp, MLSTM_HEADS, MLSTM_DIM),
               zeros(bp, MLSTM_HEADS), zeros(bp, CONV_WIDTH - 1, 2 * MLSTM_W))
    p_win0 = zeros(bp, 0, 2, NSA_KV_HEADS, HEAD_DIM)
    row_arrays = None

    x = jnp.concatenate([x_prompt.reshape(n_p, d), x_sample.reshape(bs * ts, d)], axis=0)
    outs = []
    for l in range(depth):
        lp = {name: p[name][l] for name in per_layer}
        lam_init = 0.8 - 0.6 * math.exp(-0.3 * l)
        lam_p = lp['diff_lambda']
        lam = (jnp.exp(jnp.sum(lam_p[0] * lam_p[1])) - jnp.exp(jnp.sum(lam_p[2] * lam_p[3])) + lam_init).reshape(1)
        dgain = lp['diff_out_norm'].reshape(1, HEAD_DIM)
        h = _ffn(x, p['ffn1_norm'][l], w13[0], w2[0], l)
        u = _inproj(h, lp['mix_norm'], w_in, _u_gain(lp), l)

        summ_p, n_cmp_p = _compress_prompt(u, gp, lp)
        oc_p, sel_p = _cmp_select(u, gp, summ_p, n_cmp_p)
        nsa_p = _nsa_prompt(u, gp, oc_p, sel_p)
        diff_p = _diff_prompt(u, gp, lam, dgain, 1.0 - lam_init)
        ml_p, c_p, nn_p, m_p = _mlstm(u, gp, lp, *p_state)

        summ_s, n_cmp_s, slc_kv = _compress_sample(u, gs, lp, cache_nsa, pt, l)
        oc_s, sel_s = _cmp_select(u, gs, summ_s, n_cmp_s)
        nsa_s = _nsa_sample(u, gs, oc_s, sel_s, slc_kv, win_state, l)
        diff_s = _diff_sample(u, gs, lam, dgain, 1.0 - lam_init, cache_diff, pt, l)
        ml_s, c_s, nn_s, m_s = _mlstm(u, gs, lp, state_ml_c[l], state_ml_n[l], state_ml_m[l], state_ml_conv[l])

        h = _outproj(h, jnp.concatenate([nsa_p, nsa_s]), jnp.concatenate([diff_p, diff_s]),
                     jnp.concatenate([ml_p, ml_s]), w_out, l)
        x = _ffn(h, p['ffn2_norm'][l], w13[1], w2[1], l)

        row_arrays = _cache_rows(u, l, depth, row_arrays)
        win_shape = (2, NSA_KV_HEADS, HEAD_DIM)
        keep_p, keep_s = min(WINDOW, tp), min(WINDOW, state_win.shape[2] + ts)
        win_new_p = _group_rows(u, gp, C_WIN, 2 * NSA_KV_W)[:, tp - min(keep_p, tp):]
        win_new_s = _group_rows(u, gs, C_WIN, 2 * NSA_KV_W)[:, ts - min(keep_s, ts):]
        win_rows_p = _tail_rows(p_win0, win_new_p.reshape(win_new_p.shape[:2] + win_shape), keep_p)
        win_rows_s = _tail_rows(state_win[l], win_new_s.reshape(win_new_s.shape[:2] + win_shape), keep_s)
        conv_p = _tail_rows(p_state[3], _group_rows(u, gp, C_MQ, 2 * MLSTM_W), CONV_WIDTH - 1)
        conv_s = _tail_rows(state_ml_conv[l], _group_rows(u, gs, C_MQ, 2 * MLSTM_W), CONV_WIDTH - 1)
        outs.append((win_rows_p, win_rows_s, c_p, c_s, nn_p, nn_s, m_p, m_s, conv_p, conv_s))
    stacked = [jnp.stack(z) for z in zip(*outs)]
    nsa_rows, diff_rows = row_arrays
    n_rows = n_p * ROW_GROUPS

    def split(a, shape):
        return (a[:, :n_rows].reshape((depth, bp, tp) + shape), a[:, n_rows:].reshape((depth, bs, ts) + shape))

    return ((x[:n_p].reshape(bp, tp, d), x[n_p:].reshape(bs, ts, d)) + split(nsa_rows, (4, NSA_KV_HEADS, HEAD_DIM))
            + split(diff_rows, (2, DIFF_HEADS, HEAD_DIM)) + tuple(stacked))


def kernel(x_prompt, x_sample, cache_nsa, cache_diff, state_win, state_ml_c, state_ml_n, state_ml_m, state_ml_conv,
           page_table, ffn1_norm, ffn1_w13, ffn1_w2, mix_norm, w_in, nsa_q_norm, nsa_k_norm, nsa_cmp_pos, nsa_cmp_w1,
           nsa_cmp_w2, diff_q_norm, diff_k_norm, diff_lambda, diff_out_norm, ml_conv_w, ml_conv_b, ml_gate_b,
           ml_out_norm, w_out, ffn2_norm, ffn2_w13, ffn2_w2):
    params = dict(ffn1_norm=ffn1_norm, ffn1_w13=ffn1_w13, ffn1_w2=ffn1_w2, mix_norm=mix_norm, w_in=w_in,
                  nsa_q_norm=nsa_q_norm, nsa_k_norm=nsa_k_norm, nsa_cmp_pos=nsa_cmp_pos, nsa_cmp_w1=nsa_cmp_w1,
                  nsa_cmp_w2=nsa_cmp_w2, diff_q_norm=diff_q_norm, diff_k_norm=diff_k_norm, diff_lambda=diff_lambda,
                  diff_out_norm=diff_out_norm, ml_conv_w=ml_conv_w, ml_conv_b=ml_conv_b, ml_gate_b=ml_gate_b,
                  ml_out_norm=ml_out_norm, w_out=w_out, ffn2_norm=ffn2_norm, ffn2_w13=ffn2_w13, ffn2_w2=ffn2_w2)
    return _trunk(x_prompt, x_sample, cache_nsa, cache_diff, state_win, state_ml_c, state_ml_n, state_ml_m,
                  state_ml_conv, page_table, params)
```
